```python
import jax
import jax.numpy as jnp
from jax import lax
import numpy as np

D_MODEL = 2048
BATCH = 4
SEQ = 8192
DEPTH = 1
DEC_BATCH = 8
DEC_SEQ = 32
PAST_LEN = 2048

CHUNK = 64
Q_BLOCK = 128
FOX_HEAD_DIM = 128
FOX_WIDTH = D_MODEL // 2
FOX_HEADS = FOX_WIDTH // FOX_HEAD_DIM
RWKV_HEAD_DIM = 64
RWKV_WIDTH = D_MODEL // 2
RWKV_HEADS = RWKV_WIDTH // RWKV_HEAD_DIM
RWKV_DECAY_RANK = 64
RWKV_ICLR_RANK = 64
RWKV_GATE_RANK = 160
RWKV_GN_EPS = 64e-5
N_GROUPS = 4
EXPERTS_PER_GROUP = 8
N_EXPERTS = N_GROUPS * EXPERTS_PER_GROUP
TOP_K = 2
D_EXPERT = 512
MAX_EXPERT_BLOCK = 512
GATE_COLS = 2 * D_MODEL
FOX_COLS = 3 * FOX_WIDTH + FOX_HEADS
RWKV_COLS = 3 * RWKV_WIDTH + RWKV_DECAY_RANK + RWKV_ICLR_RANK + RWKV_GATE_RANK
N_IN = GATE_COLS + FOX_COLS + RWKV_COLS
DN_ALPHA = (2.0 * DEPTH) ** 0.25
DN_BETA = (8.0 * DEPTH) ** -0.25
LN_EPS = 1e-5
QK_EPS = 1e-6
NEG_INF = -1e30

kernel_name = 'hybrid_fox_rwkv7_hmoe_stream_step'


def layer_norm(x, g, b):
    xf = x.astype(jnp.float32)
    mu = jnp.mean(xf, axis=-1, keepdims=True)
    var = jnp.mean(jnp.square(xf - mu), axis=-1, keepdims=True)
    y = (xf - mu) * lax.rsqrt(var + LN_EPS) * g.astype(jnp.float32) + b.astype(jnp.float32)
    return y.astype(x.dtype)


def rms_norm(x, g):
    xf = x.astype(jnp.float32)
    y = xf * lax.rsqrt(jnp.mean(jnp.square(xf), axis=-1, keepdims=True) + QK_EPS) * g.astype(jnp.float32)
    return y.astype(x.dtype)


def fox_attention(q, k, v, logf):
    B, T, H, Dh = q.shape
    L = k.shape[1]
    P = L - T
    F = jnp.cumsum(logf.astype(jnp.float32), axis=1).transpose(0, 2, 1)
    scale = FOX_HEAD_DIM ** -0.5
    outs = []
    for qs in range(0, T, Q_BLOCK):
        qe = min(qs + Q_BLOCK, T)
        nk = P + qe
        s = jnp.einsum('bqhd,bkhd->bhqk', q[:, qs:qe], k[:, :nk],
                       preferred_element_type=jnp.float32) * scale
        s = s + F[:, :, P + qs:P + qe, None] - F[:, :, None, :nk]
        qpos = P + qs + jnp.arange(qe - qs)
        mask = jnp.arange(nk)[None, :] <= qpos[:, None]
        s = jnp.where(mask, s, NEG_INF)
        pr = jax.nn.softmax(s, axis=-1).astype(v.dtype)
        outs.append(jnp.einsum('bhqk,bkhd->bqhd', pr, v[:, :nk]))
    return jnp.concatenate(outs, axis=1)


def _rwkv_step(S, inp):
    r, w, k, v, a, b = inp
    sa = jnp.einsum('bhij,bhj->bhi', S, a)
    S = S * w[:, :, None, :] + sa[..., None] * b[:, :, None, :] + v[..., None] * k[:, :, None, :]
    return S, jnp.einsum('bhij,bhj->bhi', S, r)


def rwkv7_time_mix(p, shift_prev, s0, lp):
    B, T, _ = p.shape
    f32 = jnp.float32
    pf = p.astype(f32)
    prev = jnp.concatenate([shift_prev.astype(f32), pf[:, :-1]], axis=1)
    xs = pf + (prev - pf) * lp['rwkv_mu'].astype(f32)
    c0 = 3 * RWKV_WIDTH
    r, k, v, wl, al, gl = jnp.split(
        xs, [RWKV_WIDTH, 2 * RWKV_WIDTH, c0, c0 + RWKV_DECAY_RANK, c0 + RWKV_DECAY_RANK + RWKV_ICLR_RANK], axis=-1)
    w = -jax.nn.softplus(-(lp['rwkv_w0'] + jnp.tanh(wl) @ lp['rwkv_w2']).astype(f32)) - 0.5
    decay = jnp.exp(-jnp.exp(w))
    a = jax.nn.sigmoid((lp['rwkv_a0'] + al @ lp['rwkv_a2']).astype(f32))
    g = (jax.nn.sigmoid(gl) @ lp['rwkv_g2']).astype(f32)
    heads = lambda t: t.reshape(B, T, RWKV_HEADS, RWKV_HEAD_DIM)
    kk = heads(k * lp['rwkv_k_k'])
    kk = kk * lax.rsqrt(jnp.maximum(jnp.sum(jnp.square(kk), axis=-1, keepdims=True), 1e-24))
    k = k * (1.0 + (a - 1.0) * lp['rwkv_k_a'])
    rh, wh, kh, vh, ah = heads(r), heads(decay), heads(k), heads(v), heads(a)
    tm = lambda t: jnp.swapaxes(t, 0, 1).astype(f32)
    s_last, y = lax.scan(_rwkv_step, s0.astype(f32),
                         (tm(rh), tm(wh), tm(kh), tm(vh), tm(-kk), tm(kk * ah)))
    y = jnp.swapaxes(y, 0, 1)
    mu = jnp.mean(y, axis=-1, keepdims=True)
    var = jnp.mean(jnp.square(y - mu), axis=-1, keepdims=True)
    y = ((y - mu) * lax.rsqrt(var + RWKV_GN_EPS)).reshape(B, T, RWKV_WIDTH)
    y = y * lp['rwkv_gn_g'].astype(f32) + lp['rwkv_gn_b'].astype(f32)
    bonus = jnp.sum(rh * kh * lp['rwkv_r_k'].astype(f32), axis=-1, keepdims=True) * vh
    y = (y + bonus.reshape(B, T, RWKV_WIDTH)) * g
    return y.astype(p.dtype), s_last, p[:, -1:]


def token_mixing(x, lp, past_k, past_v, past_logf, s0, shift_prev):
    B, T, _ = x.shape
    proj = x @ lp['w_in']
    gates, fox, rw = jnp.split(proj, [GATE_COLS, GATE_COLS + FOX_COLS], axis=-1)
    g_a, g_b = jnp.split(jax.nn.sigmoid(gates), 2, axis=-1)
    q, k, v, fl = jnp.split(fox, [FOX_WIDTH, 2 * FOX_WIDTH, 3 * FOX_WIDTH], axis=-1)
    hs = (B, T, FOX_HEADS, FOX_HEAD_DIM)
    q = rms_norm(q.reshape(hs), lp['fox_q_norm'])
    k = rms_norm(k.reshape(hs), lp['fox_k_norm'])
    v = v.reshape(hs)
    logf = jax.nn.log_sigmoid((fl + lp['fox_b_f']).astype(jnp.float32))
    o_a = fox_attention(q,
                        jnp.concatenate([past_k.astype(k.dtype), k], axis=1),
                        jnp.concatenate([past_v.astype(v.dtype), v], axis=1),
                        jnp.concatenate([past_logf.astype(jnp.float32), logf], axis=1))
    o_b, s_new, shift_new = rwkv7_time_mix(rw, shift_prev, s0, lp)
    merged = (g_a * (o_a.reshape(B, T, FOX_WIDTH) @ lp['w_branch_a'])
              + g_b * (o_b @ lp['w_branch_b']))
    return merged @ lp['w_out'], (k, v, logf.astype(x.dtype), s_new, shift_new)


def _expert_block_size(n_slots):
    target = max(1, n_slots // N_EXPERTS)
    return int(min(MAX_EXPERT_BLOCK, max(8, 1 << (target.bit_length() - 1))))


def grouped_experts(xf, eid, wts, w_gate, w_up, w_down):
    N, K = eid.shape
    NK = N * K
    blk = _expert_block_size(NK)
    nb = -(-(NK + N_EXPERTS * (blk - 1)) // blk)
    flat_e = eid.reshape(-1)
    order = jnp.argsort(flat_e)
    se = flat_e[order]
    counts = jnp.bincount(flat_e, length=N_EXPERTS)
    padded = (counts + blk - 1) // blk * blk
    pad_end = jnp.cumsum(padded)
    pad_start = pad_end - padded
    start = jnp.cumsum(counts) - counts
    dest = pad_start[se] + jnp.arange(NK) - start[se]
    slot_tok = jnp.zeros((nb * blk,), jnp.int32).at[dest].set((order // K).astype(jnp.int32))
    slot_w = jnp.zeros((nb * blk,), wts.dtype).at[dest].set(wts.reshape(-1)[order])
    blk_e = jnp.minimum(jnp.searchsorted(pad_end, jnp.arange(nb) * blk, side='right'), N_EXPERTS - 1)

    def run(args):
        e, tok, wt = args
        xb = xf[tok]
        h = jax.nn.silu(xb @ w_gate[e]) * (xb @ w_up[e])
        return (h @ w_down[e]) * wt[:, None]

    yb = lax.map(run, (blk_e, slot_tok.reshape(nb, blk), slot_w.reshape(nb, blk)))
    return jnp.zeros_like(xf).at[slot_tok].add(yb.reshape(nb * blk, xf.shape[1]))


def hier_moe(x, lp):
    B, T, D = x.shape
    xf = x.reshape(B * T, D)
    n = xf.shape[0]
    rows = jnp.arange(n)
    g_logits = (xf @ lp['router_group_w'] + lp['router_group_b']).astype(jnp.float32)
    grp = jnp.argmax(g_logits, axis=-1)
    p_grp = jax.nn.softmax(g_logits, axis=-1)[rows, grp][:, None]
    e_logits = (xf @ lp['router_expert_w'] + lp['router_expert_b']).astype(jnp.float32)
    e_logits = e_logits.reshape(n, N_GROUPS, EXPERTS_PER_GROUP)[rows, grp]
    top_v, top_i = lax.top_k(e_logits, TOP_K)
    wts = jax.nn.softmax(top_v, axis=-1) * p_grp
    eid = grp[:, None].astype(jnp.int32) * EXPERTS_PER_GROUP + top_i.astype(jnp.int32)
    y = grouped_experts(xf, eid, wts.astype(x.dtype), lp['moe_w_gate'], lp['moe_w_up'], lp['moe_w_down'])
    return y.reshape(B, T, D)


def setup_inputs(seed: int = 0) -> dict:
    key = jax.random.key(seed)
    ks = jax.random.split(key, 40)
    f32 = jnp.float32

    def nrm(i, shape, scale):
        return jax.random.normal(ks[i], shape, f32) * scale

    def unif(i, shape, lo, hi):
        return jax.random.uniform(ks[i], shape, f32, lo, hi)

    L, D, C = DEPTH, D_MODEL, RWKV_WIDTH
    return {
        'x_prompt': nrm(0, (BATCH, SEQ, D), 1.0),
        'x_sample': nrm(1, (DEC_BATCH, DEC_SEQ, D), 1.0),
        'cache_fox_k': nrm(2, (L, DEC_BATCH, PAST_LEN, FOX_HEADS, FOX_HEAD_DIM), 1.0),
        'cache_fox_v': nrm(3, (L, DEC_BATCH, PAST_LEN, FOX_HEADS, FOX_HEAD_DIM), 1.0),
        'cache_fox_logf': jax.nn.log_sigmoid(unif(4, (L, DEC_BATCH, PAST_LEN, FOX_HEADS), 2.0, 5.0)),
        'state_rwkv': nrm(5, (L, DEC_BATCH, RWKV_HEADS, RWKV_HEAD_DIM, RWKV_HEAD_DIM), 0.5),
        'state_rwkv_shift': nrm(6, (L, DEC_BATCH, 1, RWKV_COLS), 1.0),
        'ln_in_g': 1.0 + nrm(7, (D,), 0.02),
        'ln_in_b': nrm(8, (D,), 0.02),
        'w_in': nrm(9, (L, D, N_IN), D ** -0.5),
        'fox_b_f': unif(10, (L, FOX_HEADS), 2.0, 5.0),
        'fox_q_norm': 1.0 + nrm(11, (L, FOX_HEAD_DIM), 0.02),
        'fox_k_norm': 1.0 + nrm(12, (L, FOX_HEAD_DIM), 0.02),
        'rwkv_mu': unif(13, (L, RWKV_COLS), 0.0, 1.0),
        'rwkv_w0': unif(14, (L, C), -4.0, 0.0),
        'rwkv_w2': nrm(15, (L, RWKV_DECAY_RANK, C), 0.1),
        'rwkv_a0': nrm(16, (L, C), 0.5),
        'rwkv_a2': nrm(17, (L, RWKV_ICLR_RANK, C), 0.1),
        'rwkv_g2': nrm(18, (L, RWKV_GATE_RANK, C), RWKV_GATE_RANK ** -0.5),
        'rwkv_k_k': 0.85 + nrm(19, (L, C), 0.05),
        'rwkv_k_a': 1.0 + nrm(20, (L, C), 0.05),
        'rwkv_r_k': nrm(21, (L, RWKV_HEADS, RWKV_HEAD_DIM), 0.1),
        'rwkv_gn_g': 1.0 + nrm(22, (L, C), 0.02),
        'rwkv_gn_b': nrm(23, (L, C), 0.02),
        'w_branch_a': nrm(24, (L, FOX_WIDTH, D), FOX_WIDTH ** -0.5),
        'w_branch_b': nrm(25, (L, C, D), C ** -0.5),
        'w_out': nrm(26, (L, D, D), DN_BETA * D ** -0.5),
        'ln1_g': 1.0 + nrm(27, (L, D), 0.02),
        'ln1_b': nrm(28, (L, D), 0.02),
        'router_group_w': nrm(29, (L, D, N_GROUPS), D ** -0.5),
        'router_group_b': nrm(30, (L, N_GROUPS), 0.01),
        'router_expert_w': nrm(31, (L, D, N_EXPERTS), D ** -0.5),
        'router_expert_b': nrm(32, (L, N_EXPERTS), 0.01),
        'moe_w_gate': nrm(33, (L, N_EXPERTS, D, D_EXPERT), D ** -0.5),
        'moe_w_up': nrm(34, (L, N_EXPERTS, D, D_EXPERT), D ** -0.5),
        'moe_w_down': nrm(35, (L, N_EXPERTS, D_EXPERT, D), DN_BETA * D_EXPERT ** -0.5),
        'ln2_g': 1.0 + nrm(36, (L, D), 0.02),
        'ln2_b': nrm(37, (L, D), 0.02),
    }


def reference(x_prompt, x_sample, cache_fox_k, cache_fox_v, cache_fox_logf, state_rwkv, state_rwkv_shift,
              ln_in_g, ln_in_b, w_in, fox_b_f, fox_q_norm, fox_k_norm, rwkv_mu, rwkv_w0, rwkv_w2,
              rwkv_a0, rwkv_a2, rwkv_g2, rwkv_k_k, rwkv_k_a, rwkv_r_k, rwkv_gn_g, rwkv_gn_b,
              w_branch_a, w_branch_b, w_out, ln1_g, ln1_b, router_group_w, router_group_b,
              router_expert_w, router_expert_b, moe_w_gate, moe_w_up, moe_w_down, ln2_g, ln2_b):
    xp = layer_norm(x_prompt, ln_in_g, ln_in_b)
    xs = layer_norm(x_sample, ln_in_g, ln_in_b)
    bp = xp.shape[0]
    prompt_new = []
    sample_new = []
    for l in range(DEPTH):
        lp = {
            'w_in': w_in[l], 'fox_b_f': fox_b_f[l], 'fox_q_norm': fox_q_norm[l], 'fox_k_norm': fox_k_norm[l],
            'rwkv_mu': rwkv_mu[l], 'rwkv_w0': rwkv_w0[l], 'rwkv_w2': rwkv_w2[l], 'rwkv_a0': rwkv_a0[l],
            'rwkv_a2': rwkv_a2[l], 'rwkv_g2': rwkv_g2[l], 'rwkv_k_k': rwkv_k_k[l], 'rwkv_k_a': rwkv_k_a[l],
            'rwkv_r_k': rwkv_r_k[l], 'rwkv_gn_g': rwkv_gn_g[l], 'rwkv_gn_b': rwkv_gn_b[l],
            'w_branch_a': w_branch_a[l], 'w_branch_b': w_branch_b[l], 'w_out': w_out[l],
            'router_group_w': router_group_w[l], 'router_group_b': router_group_b[l],
            'router_expert_w': router_expert_w[l], 'router_expert_b': router_expert_b[l],
            'moe_w_gate': moe_w_gate[l], 'moe_w_up': moe_w_up[l], 'moe_w_down': moe_w_down[l],
        }
        out_p, new_p = token_mixing(
            xp, lp,
            jnp.zeros((bp, 0, FOX_HEADS, FOX_HEAD_DIM), xp.dtype),
            jnp.zeros((bp, 0, FOX_HEADS, FOX_HEAD_DIM), xp.dtype),
            jnp.zeros((bp, 0, FOX_HEADS), jnp.float32),
            jnp.zeros((bp, RWKV_HEADS, RWKV_HEAD_DIM, RWKV_HEAD_DIM), jnp.float32),
            jnp.zeros((bp, 1, RWKV_COLS), xp.dtype))
        xp = layer_norm(DN_ALPHA * xp + out_p, ln1_g[l], ln1_b[l])
        xp = layer_norm(DN_ALPHA * xp + hier_moe(xp, lp), ln2_g[l], ln2_b[l])
        out_s, new_s = token_mixing(xs, lp, cache_fox_k[l], cache_fox_v[l], cache_fox_logf[l],
                                    state_rwkv[l], state_rwkv_shift[l])
        xs = layer_norm(DN_ALPHA * xs + out_s, ln1_g[l], ln1_b[l])
        xs = layer_norm(DN_ALPHA * xs + hier_moe(xs, lp), ln2_g[l], ln2_b[l])
        prompt_new.append(new_p)
        sample_new.append(new_s)

    def stack(states, i):
        return jnp.stack([s[i] for s in states], axis=0)

    return (xp, xs,
            stack(prompt_new, 0), stack(prompt_new, 1), stack(prompt_new, 2), stack(prompt_new, 3), stack(prompt_new, 4),
            stack(sample_new, 0), stack(sample_new, 1), stack(sample_new, 2), stack(sample_new, 3), stack(sample_new, 4))
```

```python
import functools

import jax
import jax.numpy as jnp
from jax import lax
from jax.experimental import pallas as pl
from jax.experimental.pallas import tpu as pltpu

F32 = jnp.float32
BF16 = jnp.bfloat16

FOX_HEAD_DIM = 128
RWKV_HEAD_DIM = 64
RWKV_DECAY_RANK = 64
RWKV_ICLR_RANK = 64
RWKV_GATE_RANK = 160
RWKV_GN_EPS = 64e-5
N_GROUPS = 4
EXPERTS_PER_GROUP = 8
N_EXPERTS = N_GROUPS * EXPERTS_PER_GROUP
TOP_K = 2
LN_EPS = 1e-5
QK_EPS = 1e-6
NEG_INF = -1e30

LANES = 128
MXU_DIM = 256
VMEM_LIMIT_BYTES = 56 * 1024 * 1024

RWKV_CHUNK = 64
HEADS_PER_GROUP = MXU_DIM // RWKV_HEAD_DIM


def _params(semantics):
    return pltpu.CompilerParams(dimension_semantics=semantics, vmem_limit_bytes=VMEM_LIMIT_BYTES)


def _dot(a, b):
    return jnp.dot(a.astype(BF16), b.astype(BF16), preferred_element_type=F32)


def _dot_nt(a, b):
    return lax.dot_general(a.astype(BF16), b.astype(BF16), (((1,), (1,)), ((), ())),
                           preferred_element_type=F32)


def _split3(x):
    h1 = x.astype(BF16)
    r1 = x - h1.astype(F32)
    h2 = r1.astype(BF16)
    h3 = (r1 - h2.astype(F32)).astype(BF16)
    return h1, h2, h3


def _dot_exact_rhs(x, m_bf16):
    h1, h2, h3 = _split3(x)
    d = lambda h: jnp.dot(h, m_bf16, preferred_element_type=F32)
    return d(h1) + d(h2) + d(h3)


def _dot_exact_lhs(m_bf16, x):
    h1, h2, h3 = _split3(x)
    d = lambda h: jnp.dot(m_bf16, h, preferred_element_type=F32)
    return d(h1) + d(h2) + d(h3)


def _layer_norm(x, g, b):
    mu = jnp.mean(x, axis=-1, keepdims=True)
    xc = x - mu
    var = jnp.mean(xc * xc, axis=-1, keepdims=True)
    return xc * lax.rsqrt(var + LN_EPS) * g + b


def _sigmoid(x):
    return 1.0 / (1.0 + jnp.exp(-x))


def _log_sigmoid(x):
    return jnp.minimum(x, 0.0) - jnp.log(1.0 + jnp.exp(-jnp.abs(x)))


def _ln_cached(x_ref, g_ref, b_ref, xn_ref, j):
    @pl.when(j == 0)
    def _():
        xn_ref[...] = _layer_norm(x_ref[...], g_ref[...], b_ref[...]).astype(BF16)


def _gates_kernel(x_ref, g_ref, b_ref, w_ref, o_ref, xn_ref):
    _ln_cached(x_ref, g_ref, b_ref, xn_ref, pl.program_id(1))
    y = jnp.dot(xn_ref[...], w_ref[...], preferred_element_type=F32)
    o_ref[...] = _sigmoid(y).astype(o_ref.dtype)


def _rwkv_proj_kernel(x_ref, g_ref, b_ref, w_ref, o_ref, xn_ref):
    _ln_cached(x_ref, g_ref, b_ref, xn_ref, pl.program_id(1))
    o_ref[...] = jnp.dot(xn_ref[...], w_ref[...], preferred_element_type=F32)


def _ln_matmul(body, x, ln_g, ln_b, w, out_dtype, tm, tn, name):
    n, d = x.shape
    ncol = w.shape[1]
    return pl.pallas_call(
        body,
        grid=(n // tm, ncol // tn),
        in_specs=[
            pl.BlockSpec((tm, d), lambda i, j: (i, 0)),
            pl.BlockSpec((1, d), lambda i, j: (0, 0)),
            pl.BlockSpec((1, d), lambda i, j: (0, 0)),
            pl.BlockSpec((d, tn), lambda i, j: (0, j)),
        ],
        out_specs=pl.BlockSpec((tm, tn), lambda i, j: (i, j)),
        out_shape=jax.ShapeDtypeStruct((n, ncol), out_dtype),
        scratch_shapes=[pltpu.VMEM((tm, d), BF16)],
        compiler_params=_params(("parallel", "arbitrary")),
        name=name,
    )(x, ln_g, ln_b, w)


def _fox_proj_kernel(x_ref, g_ref, b_ref, w_ref, wf_ref, bf_ref, qn_ref, kn_ref,
                     q_ref, kf_ref, kb_ref, vf_ref, vb_ref, lf_ref, xn_ref, *, n_heads):
    j = pl.program_id(2)
    _ln_cached(x_ref, g_ref, b_ref, xn_ref, j)
    y = jnp.dot(xn_ref[...], w_ref[...], preferred_element_type=F32)

    def rms(yh, gain):
        ms = jnp.mean(yh * yh, axis=-1, keepdims=True)
        return yh * lax.rsqrt(ms + QK_EPS) * gain

    @pl.when(j == 0)
    def _():
        scale = FOX_HEAD_DIM ** -0.5
        for h in range(n_heads):
            yh = y[:, h * FOX_HEAD_DIM:(h + 1) * FOX_HEAD_DIM]
            q_ref[h] = (rms(yh, qn_ref[...]) * scale).astype(BF16)
        fl = jnp.dot(xn_ref[...], wf_ref[...], preferred_element_type=F32)
        lf = _log_sigmoid(fl + bf_ref[...])
        lf_ref[...] = lf[:, :n_heads]

    @pl.when(j == 1)
    def _():
        for h in range(n_heads):
            sl = slice(h * FOX_HEAD_DIM, (h + 1) * FOX_HEAD_DIM)
            kh = rms(y[:, sl], kn_ref[...])
            kf_ref[:, sl] = kh
            kb_ref[h] = kh.astype(BF16)

    @pl.when(j == 2)
    def _():
        vf_ref[...] = y
        for h in range(n_heads):
            vb_ref[h] = y[:, h * FOX_HEAD_DIM:(h + 1) * FOX_HEAD_DIM].astype(BF16)


def _fox_proj(x, ln_g, ln_b, w_qkv, w_f, b_f, q_norm, k_norm, tm):
    bsz, t, d = x.shape
    width = w_qkv.shape[1] // 3
    n_heads = width // FOX_HEAD_DIM
    hm = lambda: pl.BlockSpec((None, n_heads, tm, FOX_HEAD_DIM), lambda b, i, j: (b, 0, i, 0))
    tok = lambda n: pl.BlockSpec((None, tm, n), lambda b, i, j: (b, i, 0))
    const = lambda r, c: pl.BlockSpec((r, c), lambda b, i, j: (0, 0))
    hm_shape = jax.ShapeDtypeStruct((bsz, n_heads, t, FOX_HEAD_DIM), BF16)
    tok_shape = jax.ShapeDtypeStruct((bsz, t, width), F32)
    return pl.pallas_call(
        functools.partial(_fox_proj_kernel, n_heads=n_heads),
        grid=(bsz, t // tm, 3),
        in_specs=[
            pl.BlockSpec((None, tm, d), lambda b, i, j: (b, i, 0)),
            const(1, d), const(1, d),
            pl.BlockSpec((d, width), lambda b, i, j: (0, j)),
            const(d, LANES), const(1, LANES),
            const(1, FOX_HEAD_DIM), const(1, FOX_HEAD_DIM),
        ],
        out_specs=[hm(), tok(width), hm(), tok(width), hm(), tok(n_heads)],
        out_shape=[hm_shape, tok_shape, hm_shape, tok_shape, hm_shape,
                   jax.ShapeDtypeStruct((bsz, t, n_heads), F32)],
        scratch_shapes=[pltpu.VMEM((tm, d), BF16)],
        compiler_params=_params(("parallel", "parallel", "arbitrary")),
        name="ln_proj_fox",
    )(x, ln_g, ln_b, w_qkv, w_f, b_f, q_norm, k_norm)


def _cumsum_kernel(x_ref, o_ref):
    x = x_ref[...]
    rows = x.shape[0]
    li = lax.broadcasted_iota(jnp.int32, (LANES, LANES), 0)
    lj = lax.broadcasted_iota(jnp.int32, (LANES, LANES), 1)
    upper = jnp.where(li <= lj, 1.0, 0.0).astype(BF16)
    within = _dot_exact_rhs(x, upper)
    ri = lax.broadcasted_iota(jnp.int32, (rows, rows), 0)
    rj = lax.broadcasted_iota(jnp.int32, (rows, rows), 1)
    before = jnp.where(rj < ri, 1.0, 0.0).astype(BF16)
    total = jnp.broadcast_to(within[:, LANES - 1:LANES], (rows, LANES))
    o_ref[...] = within + _dot_exact_lhs(before, total)


def _cumsum_lanes(x):
    g, r, _ = x.shape
    return pl.pallas_call(
        _cumsum_kernel,
        grid=(g,),
        in_specs=[pl.BlockSpec((None, r, LANES), lambda i: (i, 0, 0))],
        out_specs=pl.BlockSpec((None, r, LANES), lambda i: (i, 0, 0)),
        out_shape=jax.ShapeDtypeStruct(x.shape, F32),
        compiler_params=_params(("parallel",)),
        name="cumsum",
    )(x)


def _attn_kernel(q_ref, k_ref, v_ref, nf_ref, o_ref, *, tq, tk, past):
    i = pl.program_id(2)
    q = q_ref[...]
    diag_start = past + i * tq
    n_full = diag_start // tk

    def update(carry, s, vblk):
        m, l, acc = carry
        m_new = jnp.maximum(m, jnp.max(s, axis=-1, keepdims=True))
        p = jnp.exp(s - m_new)
        alpha = jnp.exp(m - m_new)
        l = alpha * l + jnp.sum(p, axis=-1, keepdims=True)
        acc = alpha * acc + jnp.dot(p.astype(BF16), vblk, preferred_element_type=F32)
        return m_new, l, acc

    def full_block(j, carry):
        start = pl.multiple_of(j * tk, tk)
        s = _dot_nt(q, k_ref[pl.ds(start, tk), :]) + nf_ref[j]
        return update(carry, s, v_ref[pl.ds(start, tk), :])

    init = (jnp.full((tq, 1), NEG_INF, F32), jnp.zeros((tq, 1), F32),
            jnp.zeros((tq, FOX_HEAD_DIM), F32))
    carry = lax.fori_loop(0, n_full, full_block, init)

    start = pl.multiple_of(diag_start, tq)
    s = _dot_nt(q, k_ref[pl.ds(start, tq), :]) + nf_ref[n_full][:, :tq]
    row = lax.broadcasted_iota(jnp.int32, (tq, tq), 0)
    col = lax.broadcasted_iota(jnp.int32, (tq, tq), 1)
    s = jnp.where(col <= row, s, NEG_INF)
    _, l, acc = update(carry, s, v_ref[pl.ds(start, tq), :])
    o_ref[...] = (acc / l).astype(o_ref.dtype)


def _fox_attention(q, k, v, neg_f, tq, tk):
    bsz, n_heads, t_q, dh = q.shape
    length = k.shape[2]
    n_kblk = neg_f.shape[2]
    assert (length - t_q) % tk == 0 and (tq == tk or t_q == tq) and tq <= tk
    return pl.pallas_call(
        functools.partial(_attn_kernel, tq=tq, tk=tk, past=length - t_q),
        grid=(bsz, n_heads, t_q // tq),
        in_specs=[
            pl.BlockSpec((None, None, tq, dh), lambda b, h, i: (b, h, i, 0)),
            pl.BlockSpec((None, None, length, dh), lambda b, h, i: (b, h, 0, 0)),
            pl.BlockSpec((None, None, length, dh), lambda b, h, i: (b, h, 0, 0)),
            pl.BlockSpec((None, None, n_kblk, 1, tk), lambda b, h, i: (b, h, 0, 0, 0)),
        ],
        out_specs=pl.BlockSpec((None, tq, dh), lambda b, h, i: (b, i, h)),
        out_shape=jax.ShapeDtypeStruct((bsz, t_q, n_heads * dh), BF16),
        compiler_params=_params(("parallel", "parallel", "arbitrary")),
        name="fox_attn",
    )(q, k, v, neg_f)


def _rwkv_kernel(p_ref, shift_ref, s0_ref, mu_ref, w0_ref, a0_ref, kk_ref, ka_ref, rk_ref,
                 gng_ref, gnb_ref, w2_ref, a2_ref, g2_ref, o_ref, sout_ref,
                 state, prev_row, at_s, rt_s, bt_s, kt_s, v_s, g_s, y_s, *, tc, t_valid):
    c_len = RWKV_CHUNK
    t = pl.program_id(1)
    width = o_ref.shape[-1]
    n_groups = width // MXU_DIM

    @pl.when(t == 0)
    def _():
        state[...] = s0_ref[...]
        prev_row[...] = shift_ref[...]

    p = p_ref[...]
    ridx = lax.broadcasted_iota(jnp.int32, p.shape, 0)
    prev = jnp.where(ridx == 0, jnp.broadcast_to(prev_row[...], p.shape), pltpu.roll(p, 1, 0))
    prev_row[...] = p[tc - 1:tc, :]
    xs = p + (prev - p) * mu_ref[...]
    r = xs[:, 0:width]
    k = xs[:, width:2 * width]
    v = xs[:, 2 * width:3 * width]
    lora_in = xs[:, 3 * width:3 * width + LANES]
    gate_in = xs[:, 3 * width + LANES:]
    zw = _dot(jnp.tanh(lora_in), w2_ref[...])
    za = _dot(lora_in, a2_ref[...])
    g = _dot(_sigmoid(gate_in), g2_ref[...])
    zz = w0_ref[...] + zw
    w_log = jnp.minimum(zz, 0.0) - jnp.log(1.0 + jnp.exp(-jnp.abs(zz))) - 0.5
    lw = -jnp.exp(w_log)
    iclr = _sigmoid(a0_ref[...] + za)

    hr = lax.broadcasted_iota(jnp.int32, (MXU_DIM, MXU_DIM), 0) // RWKV_HEAD_DIM
    hc = lax.broadcasted_iota(jnp.int32, (MXU_DIM, MXU_DIM), 1) // RWKV_HEAD_DIM
    same_head = hr == hc
    ones_bd = jnp.where(same_head, 1.0, 0.0).astype(BF16)

    def head_sum(x):
        parts = [_dot_exact_rhs(x[:, gi * MXU_DIM:(gi + 1) * MXU_DIM], ones_bd) for gi in range(n_groups)]
        return jnp.concatenate(parts, axis=-1)

    kk = k * kk_ref[...]
    kk = kk * lax.rsqrt(jnp.maximum(head_sum(kk * kk), 1e-24))
    k = k * (1.0 + (iclr - 1.0) * ka_ref[...])
    if t_valid < tc:
        live = lax.broadcasted_iota(jnp.int32, (tc, width), 0) < t_valid
        lw = jnp.where(live, lw, 0.0)
        kk = jnp.where(live, kk, 0.0)
        k = jnp.where(live, k, 0.0)
        v = jnp.where(live, v, 0.0)

    ti = lax.broadcasted_iota(jnp.int32, (tc, tc), 0)
    tj = lax.broadcasted_iota(jnp.int32, (tc, tc), 1)
    tri = jnp.where(ti // c_len == tj // c_len, jnp.where(tj <= ti, 1.0, 0.0), 0.0).astype(BF16)
    gcum = _dot_exact_lhs(tri, lw)
    e_in = jnp.exp(gcum)
    e_inv = jnp.exp(-gcum)
    at_s[...] = -kk * jnp.exp(gcum - lw)
    rt_s[...] = r * e_in
    bt_s[...] = kk * iclr * e_inv
    kt_s[...] = k * e_inv
    v_s[...] = v
    g_s[...] = gcum
    bonus = head_sum(r * k * rk_ref[...]) * v

    row = lax.broadcasted_iota(jnp.int32, (c_len, MXU_DIM), 0)
    lane = lax.broadcasted_iota(jnp.int32, (c_len, MXU_DIM), 1) % c_len
    strict = lane < row
    incl = lane <= row
    eye_w = jnp.where(lane == row, 1.0, 0.0)
    bd_mask = same_head

    def bd(x):
        return jnp.where(bd_mask, jnp.concatenate([x] * HEADS_PER_GROUP, axis=0), 0.0).astype(BF16)

    def chunk(ci, carry):
        r0 = pl.multiple_of(ci * c_len, c_len)
        gend = jnp.exp(g_s[pl.ds(r0 + c_len - 1, 1), :])
        for gi in range(n_groups):
            ls = slice(gi * MXU_DIM, (gi + 1) * MXU_DIM)
            at = at_s[pl.ds(r0, c_len), ls]
            rt = rt_s[pl.ds(r0, c_len), ls]
            bt = bt_s[pl.ds(r0, c_len), ls]
            kt = kt_s[pl.ds(r0, c_len), ls]
            vv = v_s[pl.ds(r0, c_len), ls]
            ge = gend[:, ls]
            ar = jnp.concatenate([at, rt], axis=0)
            ab = _dot_nt(ar, bd(bt))
            ak = _dot_nt(ar, bd(kt))
            pw = jnp.where(strict, ab[:c_len], 0.0)
            a_rb = jnp.where(incl, ab[c_len:], 0.0)
            a_ak = jnp.where(strict, ak[:c_len], 0.0)
            a_rk = jnp.where(incl, ak[c_len:], 0.0)
            tm = eye_w + pw
            pw = _dot(pw, bd(pw))
            n_sq = c_len.bit_length() - 1
            for js in range(1, n_sq):
                bdp = bd(pw)
                if js < n_sq - 1:
                    tp = _dot(jnp.concatenate([tm, pw], axis=0), bdp)
                    tm = tm + tp[:c_len]
                    pw = tp[c_len:]
                else:
                    tm = tm + _dot(tm, bdp)
            a_hat = _dot(tm, bd(at))
            u_hat = _dot(tm, bd(_dot(a_ak, bd(vv))))
            st = state[gi]
            u = _dot_nt(a_hat, st) + u_hat
            y = _dot_nt(rt, st) + _dot(a_rb, bd(u)) + _dot(a_rk, bd(vv))
            y_s[pl.ds(r0, c_len), ls] = y
            uv_t = jnp.concatenate([u, vv], axis=0).T
            bk = jnp.concatenate([bt * ge, kt * ge], axis=0)
            state[gi] = st * ge + jnp.where(bd_mask, _dot(uv_t, bk), 0.0)
        return carry

    lax.fori_loop(0, tc // c_len, chunk, 0)

    y = y_s[...]
    inv_n = 1.0 / RWKV_HEAD_DIM
    mean = head_sum(y) * inv_n
    yc = y - mean
    var = head_sum(yc * yc) * inv_n
    yn = yc * lax.rsqrt(var + RWKV_GN_EPS) * gng_ref[...] + gnb_ref[...]
    o_ref[...] = ((yn + bonus) * g).astype(o_ref.dtype)

    @pl.when(t == pl.num_programs(1) - 1)
    def _():
        sout_ref[...] = state[...]


def _rwkv_mix(p, shift_prev, s0_bd, vecs, w2p, a2p, g2p, tc, t_valid):
    bsz, tp, cols = p.shape
    width = w2p.shape[1]
    n_groups = width // MXU_DIM
    vec_specs = [pl.BlockSpec((1, a.shape[1]), lambda b, t: (0, 0)) for a in vecs]
    mat = lambda a: pl.BlockSpec(a.shape, lambda b, t: (0, 0))
    st_spec = pl.BlockSpec((None, n_groups, MXU_DIM, MXU_DIM), lambda b, t: (b, 0, 0, 0))
    tile = lambda: pltpu.VMEM((tc, width), F32)
    return pl.pallas_call(
        functools.partial(_rwkv_kernel, tc=tc, t_valid=t_valid),
        grid=(bsz, tp // tc),
        in_specs=[pl.BlockSpec((None, tc, cols), lambda b, t: (b, t, 0)),
                  pl.BlockSpec((None, 1, cols), lambda b, t: (b, 0, 0)),
                  st_spec] + vec_specs + [mat(w2p), mat(a2p), mat(g2p)],
        out_specs=[pl.BlockSpec((None, tc, width), lambda b, t: (b, t, 0)), st_spec],
        out_shape=[jax.ShapeDtypeStruct((bsz, tp, width), BF16),
                   jax.ShapeDtypeStruct(s0_bd.shape, F32)],
        scratch_shapes=[pltpu.VMEM((n_groups, MXU_DIM, MXU_DIM), F32), pltpu.VMEM((1, cols), F32),
                        tile(), tile(), tile(), tile(), tile(), tile(), tile()],
        compiler_params=_params(("parallel", "arbitrary")),
        name="rwkv_mix",
    )(p, shift_prev, s0_bd, *vecs, w2p, a2p, g2p)


def _merge_kernel(x_ref, oa_ref, ob_ref, gt_ref, wa_ref, wb_ref, wo_ref, lig_ref, lib_ref,
                  l1g_ref, l1b_ref, wr_ref, rb_ref, x1_ref, ids_ref, wts_ref, *, alpha):
    d = x_ref.shape[-1]
    xn = _layer_norm(x_ref[...], lig_ref[...], lib_ref[...])
    ya = jnp.dot(oa_ref[...], wa_ref[...], preferred_element_type=F32)
    yb = jnp.dot(ob_ref[...], wb_ref[...], preferred_element_type=F32)
    merged = gt_ref[:, :d].astype(F32) * ya + gt_ref[:, d:].astype(F32) * yb
    out = jnp.dot(merged.astype(BF16), wo_ref[...], preferred_element_type=F32)
    x1 = _layer_norm(alpha * xn + out, l1g_ref[...], l1b_ref[...])
    x1_ref[...] = x1

    h1, h2, _ = _split3(x1)
    r1 = jnp.dot(h1, wr_ref[...], preferred_element_type=F32)
    r2 = jnp.dot(h2, wr_ref[...], preferred_element_type=F32)
    logits = (r1[:, :LANES] + (r1[:, LANES:] + r2[:, :LANES]) + r2[:, LANES:]) + rb_ref[...]
    lane = lax.broadcasted_iota(jnp.int32, logits.shape, 1).astype(F32)
    big = 1e9

    def first_max(vals):
        mx = jnp.max(vals, axis=-1, keepdims=True)
        idx = jnp.min(jnp.where(vals == mx, lane, big), axis=-1, keepdims=True)
        return mx, idx

    is_grp = lane < N_GROUPS
    gmax, grp = first_max(jnp.where(is_grp, logits, NEG_INF))
    p_grp = 1.0 / jnp.sum(jnp.where(is_grp, jnp.exp(logits - gmax), 0.0), axis=-1, keepdims=True)
    lo = N_GROUPS + grp * EXPERTS_PER_GROUP
    elog = jnp.where(lane >= lo, jnp.where(lane < lo + EXPERTS_PER_GROUP, logits, NEG_INF), NEG_INF)
    v1, i1 = first_max(elog)
    v2, i2 = first_max(jnp.where(lane == i1, NEG_INF, elog))
    e2 = jnp.exp(v2 - v1)
    w1 = p_grp / (1.0 + e2)
    w2 = p_grp * e2 / (1.0 + e2)
    ids = jnp.where(lane == 0, i1 - N_GROUPS, jnp.where(lane == 1, i2 - N_GROUPS, 0.0))
    ids_ref[...] = ids.astype(jnp.int32)
    wts_ref[...] = jnp.where(lane == 0, w1, jnp.where(lane == 1, w2, 0.0))


def _merge_out(x, oa, ob, gates, wa, wb, wo, lig, lib, l1g, l1b, w_router, rbias, alpha, tm):
    n, d = x.shape
    half = oa.shape[1]
    row = lambda c: pl.BlockSpec((tm, c), lambda i: (i, 0))
    const = lambda a: pl.BlockSpec(a.shape, lambda i: (0, 0))
    return pl.pallas_call(
        functools.partial(_merge_kernel, alpha=alpha),
        grid=(n // tm,),
        in_specs=[row(d), row(half), row(half), row(2 * d), const(wa), const(wb), const(wo),
                  const(lig), const(lib), const(l1g), const(l1b), const(w_router), const(rbias)],
        out_specs=[row(d), row(LANES), row(LANES)],
        out_shape=[jax.ShapeDtypeStruct((n, d), F32), jax.ShapeDtypeStruct((n, LANES), jnp.int32),
                   jax.ShapeDtypeStruct((n, LANES), F32)],
        compiler_params=_params(("parallel",)),
        name="merge_out",
    )(x, oa, ob, gates, wa, wb, wo, lig, lib, l1g, l1b, w_router, rbias)


def _moe_kernel(blk_e_ref, nvalid_ref, src_ref, dst_ref, x_hbm, wg_ref, wu_ref, wd_ref, y_hbm,
                xb, yb, sem_in, sem_out):
    j = pl.program_id(0)
    nv = nvalid_ref[j]

    def row_in(r):
        return pltpu.make_async_copy(x_hbm.at[pl.ds(src_ref[0, r], 1), :], xb.at[pl.ds(r, 1), :], sem_in)

    def row_out(r):
        return pltpu.make_async_copy(yb.at[pl.ds(r, 1), :], y_hbm.at[pl.ds(dst_ref[0, r], 1), :], sem_out)

    @pl.when(j == 0)
    def _():
        xb[...] = jnp.zeros_like(xb)

    @pl.when(nv > 0)
    def _():
        lax.fori_loop(0, nv, lambda r, c: (row_in(r).start(), c)[1], 0)
        lax.fori_loop(0, nv, lambda r, c: (row_in(r).wait(), c)[1], 0)
        xv = xb[...].astype(BF16)
        hg = jnp.dot(xv, wg_ref[...].astype(BF16), preferred_element_type=F32)
        hu = jnp.dot(xv, wu_ref[...].astype(BF16), preferred_element_type=F32)
        h = hg * _sigmoid(hg) * hu
        yb[...] = jnp.dot(h.astype(BF16), wd_ref[...].astype(BF16), preferred_element_type=F32)
        lax.fori_loop(0, nv, lambda r, c: (row_out(r).start(), c)[1], 0)
        lax.fori_loop(0, nv, lambda r, c: (row_out(r).wait(), c)[1], 0)


def _moe_experts(x1, blk_e, nvalid, src, dst, w_gate, w_up, w_down, blk):
    n, d = x1.shape
    nb = blk_e.shape[0]
    de = w_gate.shape[-1]
    grid_spec = pltpu.PrefetchScalarGridSpec(
        num_scalar_prefetch=2,
        grid=(nb,),
        in_specs=[
            pl.BlockSpec((None, 1, blk), lambda j, be, nv: (j, 0, 0), memory_space=pltpu.SMEM),
            pl.BlockSpec((None, 1, blk), lambda j, be, nv: (j, 0, 0), memory_space=pltpu.SMEM),
            pl.BlockSpec(memory_space=pl.ANY),
            pl.BlockSpec((None, d, de), lambda j, be, nv: (be[j], 0, 0)),
            pl.BlockSpec((None, d, de), lambda j, be, nv: (be[j], 0, 0)),
            pl.BlockSpec((None, de, d), lambda j, be, nv: (be[j], 0, 0)),
        ],
        out_specs=pl.BlockSpec(memory_space=pl.ANY),
        scratch_shapes=[pltpu.VMEM((blk, d), F32), pltpu.VMEM((blk, d), F32),
                        pltpu.SemaphoreType.DMA(()), pltpu.SemaphoreType.DMA(())],
    )
    return pl.pallas_call(
        _moe_kernel,
        grid_spec=grid_spec,
        out_shape=jax.ShapeDtypeStruct((TOP_K * n, d), F32),
        compiler_params=_params(("arbitrary",)),
        name="moe_experts",
    )(blk_e, nvalid, src, dst, x1, w_gate, w_up, w_down)


def _combine_kernel(x1_ref, y_ref, wts_ref, g_ref, b_ref, o_ref, *, alpha):
    w = wts_ref[...]
    y = w[:, 0:1] * y_ref[0] + w[:, 1:2] * y_ref[1]
    o_ref[...] = _layer_norm(alpha * x1_ref[...] + y, g_ref[...], b_ref[...])


def _moe_combine(x1, y2, wts, g, b, alpha, tm):
    n, d = x1.shape
    return pl.pallas_call(
        functools.partial(_combine_kernel, alpha=alpha),
        grid=(n // tm,),
        in_specs=[pl.BlockSpec((tm, d), lambda i: (i, 0)),
                  pl.BlockSpec((TOP_K, tm, d), lambda i: (0, i, 0)),
                  pl.BlockSpec((tm, LANES), lambda i: (i, 0)),
                  pl.BlockSpec((1, d), lambda i: (0, 0)), pl.BlockSpec((1, d), lambda i: (0, 0))],
        out_specs=pl.BlockSpec((tm, d), lambda i: (i, 0)),
        out_shape=jax.ShapeDtypeStruct((n, d), F32),
        compiler_params=_params(("parallel",)),
        name="moe_combine",
    )(x1, y2, wts, g, b)


def _dispatch_tables(eid, blk):
    n = eid.shape[0]
    nk = n * TOP_K
    nb = -(-(nk + N_EXPERTS * (blk - 1)) // blk)
    flat_e = eid.reshape(-1)
    onehot = (flat_e[:, None] == jnp.arange(N_EXPERTS, dtype=jnp.int32)[None, :]).astype(jnp.int32)
    ranks = jnp.cumsum(onehot, axis=0)
    counts = ranks[-1]
    rank = jnp.take_along_axis(ranks, flat_e[:, None], axis=1)[:, 0] - 1
    nblk_e = (counts + blk - 1) // blk
    blk_end = jnp.cumsum(nblk_e)
    blk_start = blk_end - nblk_e
    dest = blk_start[flat_e] * blk + rank
    slot = jnp.arange(nk, dtype=jnp.int32)
    src = jnp.zeros((nb * blk,), jnp.int32).at[dest].set(slot // TOP_K)
    dst = jnp.zeros((nb * blk,), jnp.int32).at[dest].set((slot % TOP_K) * n + slot // TOP_K)
    bidx = jnp.arange(nb, dtype=jnp.int32)
    blk_e = jnp.minimum(jnp.searchsorted(blk_end, bidx, side='right'), N_EXPERTS - 1).astype(jnp.int32)
    nvalid = jnp.clip(counts[blk_e] - (bidx - blk_start[blk_e]) * blk, 0, blk).astype(jnp.int32)
    nvalid = jnp.where(bidx < blk_end[-1], nvalid, 0)
    return blk_e, nvalid, src.reshape(nb, 1, blk), dst.reshape(nb, 1, blk)


def _pick(n, prefs):
    for p in prefs:
        if n % p == 0:
            return p
    return n


def _layer(x, past_k, past_v, past_logf, s0, shift_prev, wts, alpha):
    bsz, t, d = x.shape
    n = bsz * t
    xf = x.reshape(n, d)
    fox_width = wts['w_qkv'].shape[1] // 3
    n_heads = fox_width // FOX_HEAD_DIM
    rw_width = wts['w2p'].shape[1]
    rw_heads = rw_width // RWKV_HEAD_DIM
    rw_cols = wts['rwkv_cols']
    past = past_k.shape[1]

    tm = _pick(n, (512, 256))
    gates = _ln_matmul(_gates_kernel, xf, wts['ln_in_g'], wts['ln_in_b'], wts['w_gates'], BF16,
                       tm, 1024, "ln_proj_gates")
    p_rw = _ln_matmul(_rwkv_proj_kernel, xf, wts['ln_in_g'], wts['ln_in_b'], wts['w_rwkv'], F32,
                      tm, wts['w_rwkv'].shape[1] // 3, "ln_proj_rwkv")
    q, k_f, k_b, v_f, v_b, logf = _fox_proj(x, wts['ln_in_g'], wts['ln_in_b'], wts['w_qkv'], wts['w_f'],
                                            wts['b_f'], wts['q_norm'], wts['k_norm'], _pick(t, (512, 256, 32)))

    length = past + t
    tq = _pick(t, (512, 256, 32))
    tk = 512 if past else tq
    lpad = -(-length // tk) * tk
    lf_all = jnp.concatenate([past_logf.astype(F32), logf], axis=1)
    lf_all = jnp.pad(jnp.swapaxes(lf_all, 1, 2), ((0, 0), (0, 0), (0, lpad - length)))
    csum = _cumsum_lanes(lf_all.reshape(bsz * n_heads, lpad // LANES, LANES)).reshape(bsz, n_heads, lpad)
    neg_f = jnp.where(jnp.arange(lpad) < length, -csum, NEG_INF).reshape(bsz, n_heads, lpad // tk, 1, tk)
    if past:
        to_hm = lambda c: jnp.swapaxes(c, 1, 2).astype(BF16)
        k_all = jnp.concatenate([to_hm(past_k), k_b], axis=2)
        v_all = jnp.concatenate([to_hm(past_v), v_b], axis=2)
    else:
        k_all, v_all = k_b, v_b
    o_a = _fox_attention(q, k_all, v_all, neg_f, tq, tk)

    tp = -(-t // RWKV_CHUNK) * RWKV_CHUNK
    tc = _pick(tp, (256, 128, 64))
    p3 = p_rw.reshape(bsz, t, -1)
    shift_new = p3[:, t - 1:t, :rw_cols]
    if tp != t:
        p3 = jnp.pad(p3, ((0, 0), (0, tp - t), (0, 0)))
    shift_in = jnp.pad(shift_prev.astype(F32), ((0, 0), (0, 0), (0, p3.shape[-1] - rw_cols)))
    n_grp = rw_heads // HEADS_PER_GROUP
    eye = jnp.eye(HEADS_PER_GROUP, dtype=F32)
    s0_g = s0.astype(F32).reshape(bsz, n_grp, HEADS_PER_GROUP, RWKV_HEAD_DIM, RWKV_HEAD_DIM)
    s0_bd = jnp.einsum('bghvk,hj->bghvjk', s0_g, eye).reshape(bsz, n_grp, MXU_DIM, MXU_DIM)
    o_b, s_bd = _rwkv_mix(p3, shift_in, s0_bd, wts['rwkv_vecs'], wts['w2p'], wts['a2p'], wts['g2p'], tc, t)
    s_new = jnp.einsum('bghvjk,hj->bghvk',
                       s_bd.reshape(bsz, n_grp, HEADS_PER_GROUP, RWKV_HEAD_DIM, HEADS_PER_GROUP, RWKV_HEAD_DIM),
                       eye).reshape(bsz, rw_heads, RWKV_HEAD_DIM, RWKV_HEAD_DIM)
    o_b = o_b[:, :t].reshape(n, rw_width)

    x1, ids, rw = _merge_out(xf, o_a.reshape(n, fox_width), o_b, gates, wts['w_a'], wts['w_b'], wts['w_o'],
                             wts['ln_in_g'], wts['ln_in_b'], wts['ln1_g'], wts['ln1_b'],
                             wts['w_router'], wts['b_router'], alpha, _pick(n, (256,)))
    blk = _pick(n * TOP_K // N_EXPERTS, (512, 256, 128, 64, 32, 16, 8))
    tables = _dispatch_tables(ids[:, :TOP_K], blk)
    y2 = _moe_experts(x1, *tables, wts['moe_w_gate'], wts['moe_w_up'], wts['moe_w_down'], blk)
    y = _moe_combine(x1, y2.reshape(TOP_K, n, d), rw, wts['ln2_g'], wts['ln2_b'], alpha, _pick(n, (512, 256)))

    hd = (bsz, t, n_heads, FOX_HEAD_DIM)
    return y.reshape(bsz, t, d), (k_f.reshape(hd), v_f.reshape(hd), logf, s_new, shift_new)


def _prepare_weights(l, ln_in_g, ln_in_b, w_in, fox_b_f, fox_q_norm, fox_k_norm, rwkv_mu, rwkv_w0, rwkv_w2,
                     rwkv_a0, rwkv_a2, rwkv_g2, rwkv_k_k, rwkv_k_a, rwkv_r_k, rwkv_gn_g, rwkv_gn_b,
                     w_branch_a, w_branch_b, w_out, ln1_g, ln1_b, router_group_w, router_group_b,
                     router_expert_w, router_expert_b, moe_w_gate, moe_w_up, moe_w_down, ln2_g, ln2_b):
    d = w_in.shape[1]
    fox_width = w_branch_a.shape[1]
    rw_width = w_branch_b.shape[1]
    n_heads = fox_width // FOX_HEAD_DIM
    gate_cols = 2 * d
    fox_cols = 3 * fox_width + n_heads
    rw_cols = 3 * rw_width + RWKV_DECAY_RANK + RWKV_ICLR_RANK + RWKV_GATE_RANK
    row = lambda a: a.astype(F32).reshape(1, -1)
    w = w_in[l]
    w_fox = w[:, gate_cols:gate_cols + fox_cols]
    w_rw = w[:, gate_cols + fox_cols:]
    lora = RWKV_DECAY_RANK + RWKV_ICLR_RANK
    assert lora == LANES
    gate_pad = -(-RWKV_GATE_RANK // LANES) * LANES
    cols_pad = 3 * rw_width + lora + gate_pad
    pad_c = cols_pad - rw_cols
    zeros = lambda r: jnp.zeros((r, rw_width), F32)
    wr = jnp.concatenate([router_group_w[l], router_expert_w[l]], axis=1).astype(F32)
    wr = jnp.pad(wr, ((0, 0), (0, LANES - wr.shape[1])))
    rb = jnp.concatenate([router_group_b[l], router_expert_b[l]]).astype(F32)
    return {
        'ln_in_g': row(ln_in_g), 'ln_in_b': row(ln_in_b),
        'w_gates': w[:, :gate_cols].astype(BF16),
        'w_qkv': w_fox[:, :3 * fox_width].astype(BF16),
        'w_f': jnp.pad(w_fox[:, 3 * fox_width:], ((0, 0), (0, LANES - n_heads))).astype(BF16),
        'b_f': jnp.pad(row(fox_b_f[l]), ((0, 0), (0, LANES - n_heads))),
        'q_norm': row(fox_q_norm[l]), 'k_norm': row(fox_k_norm[l]),
        'w_rwkv': jnp.pad(w_rw, ((0, 0), (0, pad_c))).astype(BF16),
        'rwkv_cols': rw_cols,
        'rwkv_vecs': [jnp.pad(row(rwkv_mu[l]), ((0, 0), (0, pad_c))), row(rwkv_w0[l]), row(rwkv_a0[l]),
                      row(rwkv_k_k[l]), row(rwkv_k_a[l]), row(rwkv_r_k[l]), row(rwkv_gn_g[l]),
                      row(rwkv_gn_b[l])],
        'w2p': jnp.concatenate([rwkv_w2[l].astype(F32), zeros(RWKV_ICLR_RANK)]).astype(BF16),
        'a2p': jnp.concatenate([zeros(RWKV_DECAY_RANK), rwkv_a2[l].astype(F32)]).astype(BF16),
        'g2p': jnp.concatenate([rwkv_g2[l].astype(F32), zeros(gate_pad - RWKV_GATE_RANK)]).astype(BF16),
        'w_a': w_branch_a[l].astype(BF16), 'w_b': w_branch_b[l].astype(BF16), 'w_o': w_out[l].astype(BF16),
        'ln1_g': row(ln1_g[l]), 'ln1_b': row(ln1_b[l]),
        'w_router': jnp.concatenate(_split3(wr)[:2], axis=1),
        'b_router': jnp.pad(row(rb), ((0, 0), (0, LANES - rb.shape[0]))),
        'moe_w_gate': moe_w_gate[l], 'moe_w_up': moe_w_up[l], 'moe_w_down': moe_w_down[l],
        'ln2_g': row(ln2_g[l]), 'ln2_b': row(ln2_b[l]),
    }


def kernel(x_prompt, x_sample, cache_fox_k, cache_fox_v, cache_fox_logf, state_rwkv, state_rwkv_shift,
           ln_in_g, ln_in_b, w_in, fox_b_f, fox_q_norm, fox_k_norm, rwkv_mu, rwkv_w0, rwkv_w2,
           rwkv_a0, rwkv_a2, rwkv_g2, rwkv_k_k, rwkv_k_a, rwkv_r_k, rwkv_gn_g, rwkv_gn_b,
           w_branch_a, w_branch_b, w_out, ln1_g, ln1_b, router_group_w, router_group_b,
           router_expert_w, router_expert_b, moe_w_gate, moe_w_up, moe_w_down, ln2_g, ln2_b):
    depth = w_in.shape[0]
    assert depth == 1, "the entry LayerNorm is fused into the layer's projections: single-layer trunk only"
    alpha = (2.0 * depth) ** 0.25
    bp = x_prompt.shape[0]
    n_fox_heads = fox_b_f.shape[1]
    rw_heads, rw_dim = state_rwkv.shape[2], state_rwkv.shape[3]
    rw_cols = state_rwkv_shift.shape[-1]
    wts = _prepare_weights(0, ln_in_g, ln_in_b, w_in, fox_b_f, fox_q_norm, fox_k_norm, rwkv_mu, rwkv_w0,
                           rwkv_w2, rwkv_a0, rwkv_a2, rwkv_g2, rwkv_k_k, rwkv_k_a, rwkv_r_k, rwkv_gn_g,
                           rwkv_gn_b, w_branch_a, w_branch_b, w_out, ln1_g, ln1_b, router_group_w,
                           router_group_b, router_expert_w, router_expert_b, moe_w_gate, moe_w_up,
                           moe_w_down, ln2_g, ln2_b)
    xp, new_p = _layer(x_prompt, jnp.zeros((bp, 0, n_fox_heads, FOX_HEAD_DIM), F32),
                       jnp.zeros((bp, 0, n_fox_heads, FOX_HEAD_DIM), F32),
                       jnp.zeros((bp, 0, n_fox_heads), F32),
                       jnp.zeros((bp, rw_heads, rw_dim, rw_dim), F32),
                       jnp.zeros((bp, 1, rw_cols), F32), wts, alpha)
    xs, new_s = _layer(x_sample, cache_fox_k[0], cache_fox_v[0], cache_fox_logf[0], state_rwkv[0],
                       state_rwkv_shift[0], wts, alpha)
    return (xp, xs) + tuple(a[None] for a in new_p) + tuple(a[None] for a in new_s)
```

```python
import functools

import jax
import jax.numpy as jnp
from jax import lax
from jax.experimental import pallas as pl
from jax.experimental.pallas import tpu as pltpu

F32 = jnp.float32
BF16 = jnp.bfloat16

FOX_HEAD_DIM = 128
RWKV_HEAD_DIM = 64
RWKV_DECAY_RANK = 64
RWKV_ICLR_RANK = 64
RWKV_GATE_RANK = 160
RWKV_GN_EPS = 64e-5
N_GROUPS = 4
EXPERTS_PER_GROUP = 8
N_EXPERTS = N_GROUPS * EXPERTS_PER_GROUP
TOP_K = 2
LN_EPS = 1e-5
QK_EPS = 1e-6
NEG_INF = -1e30
LOG2_E = 1.4426950408889634

LANES = 128
MXU_DIM = 256
VMEM_LIMIT_BYTES = 56 * 1024 * 1024

ATTN_BLOCK = 512
RWKV_CHUNK = 64
HEADS_PER_GROUP = MXU_DIM // RWKV_HEAD_DIM


def _params(semantics):
    return pltpu.CompilerParams(dimension_semantics=semantics, vmem_limit_bytes=VMEM_LIMIT_BYTES)


def _dot(a, b):
    return jnp.dot(a.astype(BF16), b.astype(BF16), preferred_element_type=F32)


def _dot_nt(a, b):
    return lax.dot_general(a.astype(BF16), b.astype(BF16), (((1,), (1,)), ((), ())),
                           preferred_element_type=F32)


def _split3(x):
    h1 = x.astype(BF16)
    r1 = x - h1.astype(F32)
    h2 = r1.astype(BF16)
    h3 = (r1 - h2.astype(F32)).astype(BF16)
    return h1, h2, h3


def _dot_exact_rhs(x, m_bf16):
    h1, h2, h3 = _split3(x)
    d = lambda h: jnp.dot(h, m_bf16, preferred_element_type=F32)
    return d(h1) + d(h2) + d(h3)


def _dot_exact_lhs(m_bf16, x):
    h1, h2, h3 = _split3(x)
    d = lambda h: jnp.dot(m_bf16, h, preferred_element_type=F32)
    return d(h1) + d(h2) + d(h3)


def _layer_norm(x, g, b):
    mu = jnp.mean(x, axis=-1, keepdims=True)
    xc = x - mu
    var = jnp.mean(xc * xc, axis=-1, keepdims=True)
    return xc * lax.rsqrt(var + LN_EPS) * g + b


def _sigmoid(x):
    return 1.0 / (1.0 + jnp.exp(-x))


def _log_sigmoid(x):
    return jnp.minimum(x, 0.0) - jnp.log(1.0 + jnp.exp(-jnp.abs(x)))


def _ln_cached(x_ref, g_ref, b_ref, xn_ref, j):
    @pl.when(j == 0)
    def _():
        xn_ref[...] = _layer_norm(x_ref[...], g_ref[...], b_ref[...]).astype(BF16)


def _gates_kernel(x_ref, g_ref, b_ref, w_ref, o_ref, xn_ref):
    _ln_cached(x_ref, g_ref, b_ref, xn_ref, pl.program_id(1))
    y = jnp.dot(xn_ref[...], w_ref[...], preferred_element_type=F32)
    o_ref[...] = _sigmoid(y).astype(o_ref.dtype)


def _rwkv_proj_kernel(x_ref, g_ref, b_ref, w_ref, o_ref, xn_ref):
    _ln_cached(x_ref, g_ref, b_ref, xn_ref, pl.program_id(1))
    o_ref[...] = jnp.dot(xn_ref[...], w_ref[...], preferred_element_type=F32)


def _ln_matmul(body, x, ln_g, ln_b, w, out_dtype, tm, tn, name):
    n, d = x.shape
    ncol = w.shape[1]
    return pl.pallas_call(
        body,
        grid=(n // tm, ncol // tn),
        in_specs=[
            pl.BlockSpec((tm, d), lambda i, j: (i, 0)),
            pl.BlockSpec((1, d), lambda i, j: (0, 0)),
            pl.BlockSpec((1, d), lambda i, j: (0, 0)),
            pl.BlockSpec((d, tn), lambda i, j: (0, j)),
        ],
        out_specs=pl.BlockSpec((tm, tn), lambda i, j: (i, j)),
        out_shape=jax.ShapeDtypeStruct((n, ncol), out_dtype),
        scratch_shapes=[pltpu.VMEM((tm, d), BF16)],
        compiler_params=_params(("parallel", "arbitrary")),
        name=name,
    )(x, ln_g, ln_b, w)


def _fox_proj_kernel(x_ref, g_ref, b_ref, w_ref, wf_ref, bf_ref, qn_ref, kn_ref,
                     qt_ref, kf_ref, kb_ref, vf_ref, vt_ref, lf_ref, xn_ref, *, n_heads):
    j = pl.program_id(2)
    _ln_cached(x_ref, g_ref, b_ref, xn_ref, j)
    y = jnp.dot(xn_ref[...], w_ref[...], preferred_element_type=F32)

    def rms(yh, gain):
        ms = jnp.mean(yh * yh, axis=-1, keepdims=True)
        return yh * lax.rsqrt(ms + QK_EPS) * gain

    @pl.when(j == 0)
    def _():
        scale = FOX_HEAD_DIM ** -0.5 * LOG2_E
        for h in range(n_heads):
            yh = y[:, h * FOX_HEAD_DIM:(h + 1) * FOX_HEAD_DIM]
            qt_ref[h] = (rms(yh, qn_ref[...]) * scale).T.astype(BF16)
        fl = jnp.dot(xn_ref[...], wf_ref[...], preferred_element_type=F32)
        lf_ref[...] = _log_sigmoid(fl + bf_ref[...])

    @pl.when(j == 1)
    def _():
        for h in range(n_heads):
            sl = slice(h * FOX_HEAD_DIM, (h + 1) * FOX_HEAD_DIM)
            kh = rms(y[:, sl], kn_ref[...])
            kf_ref[:, sl] = kh
            kb_ref[h] = kh.astype(BF16)

    @pl.when(j == 2)
    def _():
        vf_ref[...] = y
        for h in range(n_heads):
            vt_ref[h] = y[:, h * FOX_HEAD_DIM:(h + 1) * FOX_HEAD_DIM].T.astype(BF16)


def _fox_proj(x, ln_g, ln_b, w_qkv, w_f, b_f, q_norm, k_norm, tm):
    bsz, t, d = x.shape
    width = w_qkv.shape[1] // 3
    n_heads = width // FOX_HEAD_DIM
    hm = pl.BlockSpec((None, n_heads, tm, FOX_HEAD_DIM), lambda b, i, j: (b, 0, i, 0))
    tr = lambda: pl.BlockSpec((None, n_heads, None, FOX_HEAD_DIM, tm), lambda b, i, j: (b, 0, i, 0, 0))
    tok = lambda n: pl.BlockSpec((None, tm, n), lambda b, i, j: (b, i, 0))
    const = lambda r, c: pl.BlockSpec((r, c), lambda b, i, j: (0, 0))
    tr_shape = jax.ShapeDtypeStruct((bsz, n_heads, t // tm, FOX_HEAD_DIM, tm), BF16)
    tok_shape = jax.ShapeDtypeStruct((bsz, t, width), F32)
    return pl.pallas_call(
        functools.partial(_fox_proj_kernel, n_heads=n_heads),
        grid=(bsz, t // tm, 3),
        in_specs=[
            pl.BlockSpec((None, tm, d), lambda b, i, j: (b, i, 0)),
            const(1, d), const(1, d),
            pl.BlockSpec((d, width), lambda b, i, j: (0, j)),
            const(d, LANES), const(1, LANES),
            const(1, FOX_HEAD_DIM), const(1, FOX_HEAD_DIM),
        ],
        out_specs=[tr(), tok(width), hm, tok(width), tr(), tok(LANES)],
        out_shape=[tr_shape, tok_shape, jax.ShapeDtypeStruct((bsz, n_heads, t, FOX_HEAD_DIM), BF16),
                   tok_shape, tr_shape, jax.ShapeDtypeStruct((bsz, t, LANES), F32)],
        scratch_shapes=[pltpu.VMEM((tm, d), BF16)],
        compiler_params=_params(("parallel", "parallel", "arbitrary")),
        name="ln_proj_fox",
    )(x, ln_g, ln_b, w_qkv, w_f, b_f, q_norm, k_norm)


def _fgate_bias_kernel(lf_ref, o_ref, carry, *, length, n_heads):
    t = pl.program_id(1)
    tt = lf_ref.shape[0]

    @pl.when(t == 0)
    def _():
        carry[...] = jnp.zeros_like(carry)

    ri = lax.broadcasted_iota(jnp.int32, (tt, tt), 0)
    rj = lax.broadcasted_iota(jnp.int32, (tt, tt), 1)
    tri = jnp.where(rj <= ri, 1.0, 0.0).astype(BF16)
    csum = _dot_exact_lhs(tri, lf_ref[...]) + carry[...]
    carry[...] = csum[tt - 1:tt, :]
    pos = t * tt + lax.broadcasted_iota(jnp.int32, csum.shape, 0)
    neg = jnp.where(pos < length, -LOG2_E * csum, NEG_INF)
    pieces = jnp.concatenate(_split3(neg), axis=1)
    sr = lax.broadcasted_iota(jnp.int32, (3 * LANES, LANES), 0)
    sc = lax.broadcasted_iota(jnp.int32, (3 * LANES, LANES), 1)
    for h in range(n_heads):
        sel = jnp.where(sr % LANES == h, jnp.where(sr // LANES == sc, 1.0, 0.0), 0.0).astype(BF16)
        o_ref[h] = jnp.dot(pieces, sel, preferred_element_type=F32).astype(BF16)


def _fgate_bias(lf, length, n_heads, tt):
    bsz, lp, _ = lf.shape
    return pl.pallas_call(
        functools.partial(_fgate_bias_kernel, length=length, n_heads=n_heads),
        grid=(bsz, lp // tt),
        in_specs=[pl.BlockSpec((None, tt, LANES), lambda b, t: (b, t, 0))],
        out_specs=pl.BlockSpec((None, n_heads, tt, LANES), lambda b, t: (b, 0, t, 0)),
        out_shape=jax.ShapeDtypeStruct((bsz, n_heads, lp, LANES), BF16),
        scratch_shapes=[pltpu.VMEM((1, LANES), F32)],
        compiler_params=_params(("parallel", "arbitrary")),
        name="fgate_bias",
    )(lf)


N_BIAS_PIECES = 3


ATTN_HEADS_PER_STEP = 2


def _attn_kernel(qt_ref, k_ref, nf_ref, vt_ref, o_ref, *, tq, tk, past):
    i = pl.program_id(2)
    n_heads = qt_ref.shape[0]
    tw = min(tq, MXU_DIM)
    chains = [(h, c) for h in range(n_heads) for c in range(tq // tw)]
    ones_rows = jnp.where(lax.broadcasted_iota(jnp.int32, (LANES, tw), 0) < N_BIAS_PIECES, 1.0, 0.0)
    qa = [jnp.concatenate([qt_ref[h, :, c * tw:(c + 1) * tw], ones_rows.astype(BF16)], axis=0)
          for h, c in chains]
    n_full = (past + i * tq) // tk

    def update(carry, s, vt):
        m, l, acc = carry
        m_new = jnp.maximum(m, jnp.max(s, axis=0, keepdims=True))
        p = jnp.exp2(s - m_new)
        alpha = jnp.exp2(m - m_new)
        l = alpha * l + jnp.sum(p, axis=0, keepdims=True)
        acc = alpha * acc + jnp.dot(vt, p.astype(BF16), preferred_element_type=F32)
        return m_new, l, acc

    def block(j, carry, causal=False):
        start = pl.multiple_of(j * tk, tk)
        ka = [jnp.concatenate([k_ref[h, pl.ds(start, tk), :], nf_ref[h, pl.ds(start, tk), :]], axis=1)
              for h in range(n_heads)]
        ss = [jnp.dot(ka[h], qa[n], preferred_element_type=F32) for n, (h, c) in enumerate(chains)]
        if causal:
            key = lax.broadcasted_iota(jnp.int32, (tk, tw), 0)
            qry = lax.broadcasted_iota(jnp.int32, (tk, tw), 1)
            ss = [jnp.where(key <= qry + c * tw, s, NEG_INF) for s, (h, c) in zip(ss, chains)]
        return tuple(update(carry[n], ss[n], vt_ref[h, j]) for n, (h, c) in enumerate(chains))

    init = (jnp.full((1, tw), NEG_INF, F32), jnp.zeros((1, tw), F32), jnp.zeros((FOX_HEAD_DIM, tw), F32))
    carry = lax.fori_loop(0, n_full, block, tuple(init for _ in chains))
    carry = block(n_full, carry, causal=True)
    for n, (h, c) in enumerate(chains):
        _, l, acc = carry[n]
        o_ref[c * tw:(c + 1) * tw, h * FOX_HEAD_DIM:(h + 1) * FOX_HEAD_DIM] = (acc / l).T.astype(o_ref.dtype)


def _fox_attention(qt, k, nf, vt, past, tk):
    bsz, n_heads, nq, dh, tq = qt.shape
    lp = k.shape[2]
    hps = ATTN_HEADS_PER_STEP
    assert past % tk == 0 and (tq == tk or nq == 1) and tq <= tk and lp % tk == 0 and n_heads % hps == 0
    whole = lambda a: pl.BlockSpec((None, hps) + a.shape[2:], lambda b, h, i: (b, h) + (0,) * (a.ndim - 2))
    return pl.pallas_call(
        functools.partial(_attn_kernel, tq=tq, tk=tk, past=past),
        grid=(bsz, n_heads // hps, nq),
        in_specs=[pl.BlockSpec((None, hps, None, dh, tq), lambda b, h, i: (b, h, i, 0, 0)),
                  whole(k), whole(nf), whole(vt)],
        out_specs=pl.BlockSpec((None, tq, hps * dh), lambda b, h, i: (b, i, h)),
        out_shape=jax.ShapeDtypeStruct((bsz, nq * tq, n_heads * dh), BF16),
        compiler_params=_params(("parallel", "parallel", "arbitrary")),
        name="fox_attn",
    )(qt, k, nf, vt)


def _rwkv_kernel(p_ref, shift_ref, s0_ref, mu_ref, w0_ref, a0_ref, kk_ref, ka_ref, rk_ref,
                 gng_ref, gnb_ref, w2_ref, a2_ref, g2_ref, o_ref, sout_ref,
                 state, prev_row, *, tc, t_valid):
    c_len = RWKV_CHUNK
    t = pl.program_id(1)
    width = o_ref.shape[-1]
    n_groups = width // MXU_DIM

    @pl.when(t == 0)
    def _():
        state[...] = s0_ref[...]
        prev_row[...] = shift_ref[...]

    p = p_ref[...]
    ridx = lax.broadcasted_iota(jnp.int32, p.shape, 0)
    prev = jnp.where(ridx == 0, jnp.broadcast_to(prev_row[...], p.shape), pltpu.roll(p, 1, 0))
    prev_row[...] = p[tc - 1:tc, :]
    xs = p + (prev - p) * mu_ref[...]
    r = xs[:, 0:width]
    k = xs[:, width:2 * width]
    v = xs[:, 2 * width:3 * width]
    lora_in = xs[:, 3 * width:3 * width + LANES]
    gate_in = xs[:, 3 * width + LANES:]
    zw = _dot(jnp.tanh(lora_in), w2_ref[...])
    za = _dot(lora_in, a2_ref[...])
    g = _dot(_sigmoid(gate_in), g2_ref[...])
    zz = w0_ref[...] + zw
    w_log = jnp.minimum(zz, 0.0) - jnp.log(1.0 + jnp.exp(-jnp.abs(zz))) - 0.5
    lw = -jnp.exp(w_log)
    iclr = _sigmoid(a0_ref[...] + za)

    hr = lax.broadcasted_iota(jnp.int32, (MXU_DIM, MXU_DIM), 0) // RWKV_HEAD_DIM
    hc = lax.broadcasted_iota(jnp.int32, (MXU_DIM, MXU_DIM), 1) // RWKV_HEAD_DIM
    same_head = hr == hc
    bd_f32 = jnp.where(same_head, 1.0, 0.0)
    ones_bd = bd_f32.astype(BF16)

    def head_sum(x):
        hi = x.astype(BF16)
        lo = (x - hi.astype(F32)).astype(BF16)
        parts = []
        for gi in range(n_groups):
            ls = slice(gi * MXU_DIM, (gi + 1) * MXU_DIM)
            parts.append(jnp.dot(hi[:, ls], ones_bd, preferred_element_type=F32)
                         + jnp.dot(lo[:, ls], ones_bd, preferred_element_type=F32))
        return jnp.concatenate(parts, axis=-1)

    kk = k * kk_ref[...]
    kk = kk * lax.rsqrt(jnp.maximum(head_sum(kk * kk), 1e-24))
    k = k * (1.0 + (iclr - 1.0) * ka_ref[...])
    if t_valid < tc:
        live = lax.broadcasted_iota(jnp.int32, (tc, width), 0) < t_valid
        lw = jnp.where(live, lw, 0.0)
        kk = jnp.where(live, kk, 0.0)
        k = jnp.where(live, k, 0.0)
        v = jnp.where(live, v, 0.0)

    ti = lax.broadcasted_iota(jnp.int32, (tc, tc), 0)
    tj = lax.broadcasted_iota(jnp.int32, (tc, tc), 1)
    tri = jnp.where(ti // c_len == tj // c_len, jnp.where(tj <= ti, 1.0, 0.0), 0.0).astype(BF16)
    gcum = _dot_exact_lhs(tri, lw)
    e_in = jnp.exp(gcum)
    e_inv = jnp.exp(-gcum)
    at_all = -kk * jnp.exp(gcum - lw)
    rt_all = r * e_in
    bt_all = kk * iclr * e_inv
    kt_all = k * e_inv
    bonus = head_sum(r * k * rk_ref[...]) * v

    row = lax.broadcasted_iota(jnp.int32, (c_len, MXU_DIM), 0)
    lane = lax.broadcasted_iota(jnp.int32, (c_len, MXU_DIM), 1) % c_len
    strict = jnp.where(lane < row, 1.0, 0.0)
    incl = jnp.where(lane <= row, 1.0, 0.0)
    eye_w = jnp.where(lane == row, 1.0, 0.0)

    def bd(x):
        return jnp.concatenate([x.astype(BF16)] * HEADS_PER_GROUP, axis=0) * ones_bd

    n_sq = c_len.bit_length() - 1
    n_chunks = tc // c_len
    units = [(ci, gi) for ci in range(n_chunks) for gi in range(n_groups)]

    def cut(x, u):
        ci, gi = u
        return x[ci * c_len:(ci + 1) * c_len, gi * MXU_DIM:(gi + 1) * MXU_DIM]

    at = [cut(at_all, u) for u in units]
    rt = [cut(rt_all, u) for u in units]
    bt = [cut(bt_all, u) for u in units]
    kt = [cut(kt_all, u) for u in units]
    vv = [cut(v, u) for u in units]
    ar = [jnp.concatenate([a, r_], axis=0) for a, r_ in zip(at, rt)]
    ab = [_dot_nt(x, bd(b_)) for x, b_ in zip(ar, bt)]
    ak = [_dot_nt(x, bd(k_)) for x, k_ in zip(ar, kt)]
    pw = [x[:c_len] * strict for x in ab]
    a_rb = [x[c_len:] * incl for x in ab]
    a_ak = [x[:c_len] * strict for x in ak]
    a_rk = [x[c_len:] * incl for x in ak]
    tm = [eye_w + x for x in pw]
    pw = [_dot(x, bd(x)) for x in pw]
    for js in range(1, n_sq):
        if js < n_sq - 1:
            tp = [_dot(jnp.concatenate([t_, x], axis=0), bd(x)) for t_, x in zip(tm, pw)]
            tm = [t_ + y_[:c_len] for t_, y_ in zip(tm, tp)]
            pw = [y_[c_len:] for y_ in tp]
        else:
            tm = [t_ + _dot(t_, bd(x)) for t_, x in zip(tm, pw)]
    bdv = [bd(x) for x in vv]
    a_hat = [_dot(t_, bd(a)) for t_, a in zip(tm, at)]
    av = [_dot(x, b_) for x, b_ in zip(a_ak, bdv)]
    u_hat = [_dot(t_, bd(x)) for t_, x in zip(tm, av)]
    r_hat = [r_ + _dot(x, bd(a)) for r_, x, a in zip(rt, a_rb, a_hat)]
    y_hat = [_dot(x, bd(uh)) + _dot(z, b_) for x, uh, z, b_ in zip(a_rb, u_hat, a_rk, bdv)]
    lhs = [jnp.concatenate([a, r_], axis=0) for a, r_ in zip(a_hat, r_hat)]

    st = [state[gi] for gi in range(n_groups)]
    y_rows = []
    for ci in range(n_chunks):
        gend = jnp.exp(gcum[(ci + 1) * c_len - 1:(ci + 1) * c_len, :])
        us = [ci * n_groups + gi for gi in range(n_groups)]
        ge = [gend[:, gi * MXU_DIM:(gi + 1) * MXU_DIM] for gi in range(n_groups)]
        uy = [_dot_nt(lhs[u], st[gi]) for gi, u in enumerate(us)]
        uu = [uy[gi][:c_len] + u_hat[u] for gi, u in enumerate(us)]
        y_rows.append(jnp.concatenate([uy[gi][c_len:] + y_hat[u] for gi, u in enumerate(us)], axis=1))
        uv_t = [jnp.concatenate([uu[gi], vv[u]], axis=0).T for gi, u in enumerate(us)]
        bk = [jnp.concatenate([bt[u] * ge[gi], kt[u] * ge[gi]], axis=0) for gi, u in enumerate(us)]
        st = [st[gi] * ge[gi] + _dot(uv_t[gi], bk[gi]) * bd_f32 for gi in range(n_groups)]
    for gi in range(n_groups):
        state[gi] = st[gi]

    y = jnp.concatenate(y_rows, axis=0)
    inv_n = 1.0 / RWKV_HEAD_DIM
    mean = head_sum(y) * inv_n
    yc = y - mean
    var = head_sum(yc * yc) * inv_n
    yn = yc * lax.rsqrt(var + RWKV_GN_EPS) * gng_ref[...] + gnb_ref[...]
    o_ref[...] = ((yn + bonus) * g).astype(o_ref.dtype)

    @pl.when(t == pl.num_programs(1) - 1)
    def _():
        sout_ref[...] = state[...]


def _rwkv_mix(p, shift_prev, s0_bd, vecs, w2p, a2p, g2p, tc, t_valid):
    bsz, tp, cols = p.shape
    width = w2p.shape[1]
    n_groups = width // MXU_DIM
    vec_specs = [pl.BlockSpec((1, a.shape[1]), lambda b, t: (0, 0)) for a in vecs]
    mat = lambda a: pl.BlockSpec(a.shape, lambda b, t: (0, 0))
    st_spec = pl.BlockSpec((None, n_groups, MXU_DIM, MXU_DIM), lambda b, t: (b, 0, 0, 0))
    return pl.pallas_call(
        functools.partial(_rwkv_kernel, tc=tc, t_valid=t_valid),
        grid=(bsz, tp // tc),
        in_specs=[pl.BlockSpec((None, tc, cols), lambda b, t: (b, t, 0)),
                  pl.BlockSpec((None, 1, cols), lambda b, t: (b, 0, 0)),
                  st_spec] + vec_specs + [mat(w2p), mat(a2p), mat(g2p)],
        out_specs=[pl.BlockSpec((None, tc, width), lambda b, t: (b, t, 0)), st_spec],
        out_shape=[jax.ShapeDtypeStruct((bsz, tp, width), BF16),
                   jax.ShapeDtypeStruct(s0_bd.shape, F32)],
        scratch_shapes=[pltpu.VMEM((n_groups, MXU_DIM, MXU_DIM), F32), pltpu.VMEM((1, cols), F32)],
        compiler_params=_params(("parallel", "arbitrary")),
        name="rwkv_mix",
    )(p, shift_prev, s0_bd, *vecs, w2p, a2p, g2p)


def _merge_kernel(x_ref, oa_ref, ob_ref, gt_ref, wa_ref, wb_ref, wo_ref, lig_ref, lib_ref,
                  l1g_ref, l1b_ref, wr_ref, rb_ref, x1_ref, ids_ref, wts_ref, *, alpha):
    d = x_ref.shape[-1]
    xn = _layer_norm(x_ref[...], lig_ref[...], lib_ref[...])
    ya = jnp.dot(oa_ref[...], wa_ref[...], preferred_element_type=F32)
    yb = jnp.dot(ob_ref[...], wb_ref[...], preferred_element_type=F32)
    merged = gt_ref[:, :d].astype(F32) * ya + gt_ref[:, d:].astype(F32) * yb
    out = jnp.dot(merged.astype(BF16), wo_ref[...], preferred_element_type=F32)
    x1 = _layer_norm(alpha * xn + out, l1g_ref[...], l1b_ref[...])
    x1_ref[...] = x1

    h1, h2, _ = _split3(x1)
    r1 = jnp.dot(h1, wr_ref[...], preferred_element_type=F32)
    r2 = jnp.dot(h2, wr_ref[...], preferred_element_type=F32)
    logits = (r1[:, :LANES] + (r1[:, LANES:] + r2[:, :LANES]) + r2[:, LANES:]) + rb_ref[...]
    lane = lax.broadcasted_iota(jnp.int32, logits.shape, 1).astype(F32)
    big = 1e9

    def first_max(vals):
        mx = jnp.max(vals, axis=-1, keepdims=True)
        idx = jnp.min(jnp.where(vals == mx, lane, big), axis=-1, keepdims=True)
        return mx, idx

    is_grp = lane < N_GROUPS
    gmax, grp = first_max(jnp.where(is_grp, logits, NEG_INF))
    p_grp = 1.0 / jnp.sum(jnp.where(is_grp, jnp.exp(logits - gmax), 0.0), axis=-1, keepdims=True)
    lo = N_GROUPS + grp * EXPERTS_PER_GROUP
    elog = jnp.where(lane >= lo, jnp.where(lane < lo + EXPERTS_PER_GROUP, logits, NEG_INF), NEG_INF)
    v1, i1 = first_max(elog)
    v2, i2 = first_max(jnp.where(lane == i1, NEG_INF, elog))
    e2 = jnp.exp(v2 - v1)
    w1 = p_grp / (1.0 + e2)
    w2 = p_grp * e2 / (1.0 + e2)
    ids = jnp.where(lane == 0, i1 - N_GROUPS, jnp.where(lane == 1, i2 - N_GROUPS, 0.0))
    ids_ref[...] = ids.astype(jnp.int32)
    wts_ref[...] = jnp.where(lane == 0, w1, jnp.where(lane == 1, w2, 0.0))


def _merge_out(x, oa, ob, gates, wa, wb, wo, lig, lib, l1g, l1b, w_router, rbias, alpha, tm):
    n, d = x.shape
    half = oa.shape[1]
    row = lambda c: pl.BlockSpec((tm, c), lambda i: (i, 0))
    const = lambda a: pl.BlockSpec(a.shape, lambda i: (0, 0))
    return pl.pallas_call(
        functools.partial(_merge_kernel, alpha=alpha),
        grid=(n // tm,),
        in_specs=[row(d), row(half), row(half), row(2 * d), const(wa), const(wb), const(wo),
                  const(lig), const(lib), const(l1g), const(l1b), const(w_router), const(rbias)],
        out_specs=[row(d), row(LANES), row(LANES)],
        out_shape=[jax.ShapeDtypeStruct((n, d), F32), jax.ShapeDtypeStruct((n, LANES), jnp.int32),
                   jax.ShapeDtypeStruct((n, LANES), F32)],
        compiler_params=_params(("parallel",)),
        name="merge_out",
    )(x, oa, ob, gates, wa, wb, wo, lig, lib, l1g, l1b, w_router, rbias)


def _moe_kernel(blk_e_ref, nvalid_ref, src_ref, dst_ref, x_hbm, wg_ref, wu_ref, wd_ref, y_hbm,
                xb, yb, sem_in, sem_out):
    j = pl.program_id(0)
    nv = nvalid_ref[j]

    def row_in(r):
        return pltpu.make_async_copy(x_hbm.at[pl.ds(src_ref[0, r], 1), :], xb.at[pl.ds(r, 1), :], sem_in)

    def row_out(r):
        return pltpu.make_async_copy(yb.at[pl.ds(r, 1), :], y_hbm.at[pl.ds(dst_ref[0, r], 1), :], sem_out)

    @pl.when(j == 0)
    def _():
        xb[...] = jnp.zeros_like(xb)

    @pl.when(nv > 0)
    def _():
        lax.fori_loop(0, nv, lambda r, c: (row_in(r).start(), c)[1], 0)
        lax.fori_loop(0, nv, lambda r, c: (row_in(r).wait(), c)[1], 0)
        xv = xb[...].astype(BF16)
        hg = jnp.dot(xv, wg_ref[...].astype(BF16), preferred_element_type=F32)
        hu = jnp.dot(xv, wu_ref[...].astype(BF16), preferred_element_type=F32)
        h = hg * _sigmoid(hg) * hu
        yb[...] = jnp.dot(h.astype(BF16), wd_ref[...].astype(BF16), preferred_element_type=F32)
        lax.fori_loop(0, nv, lambda r, c: (row_out(r).start(), c)[1], 0)
        lax.fori_loop(0, nv, lambda r, c: (row_out(r).wait(), c)[1], 0)


def _moe_experts(x1, blk_e, nvalid, src, dst, w_gate, w_up, w_down, blk):
    n, d = x1.shape
    nb = blk_e.shape[0]
    de = w_gate.shape[-1]
    grid_spec = pltpu.PrefetchScalarGridSpec(
        num_scalar_prefetch=2,
        grid=(nb,),
        in_specs=[
            pl.BlockSpec((None, 1, blk), lambda j, be, nv: (j, 0, 0), memory_space=pltpu.SMEM),
            pl.BlockSpec((None, 1, blk), lambda j, be, nv: (j, 0, 0), memory_space=pltpu.SMEM),
            pl.BlockSpec(memory_space=pl.ANY),
            pl.BlockSpec((None, d, de), lambda j, be, nv: (be[j], 0, 0)),
            pl.BlockSpec((None, d, de), lambda j, be, nv: (be[j], 0, 0)),
            pl.BlockSpec((None, de, d), lambda j, be, nv: (be[j], 0, 0)),
        ],
        out_specs=pl.BlockSpec(memory_space=pl.ANY),
        scratch_shapes=[pltpu.VMEM((blk, d), F32), pltpu.VMEM((blk, d), F32),
                        pltpu.SemaphoreType.DMA(()), pltpu.SemaphoreType.DMA(())],
    )
    return pl.pallas_call(
        _moe_kernel,
        grid_spec=grid_spec,
        out_shape=jax.ShapeDtypeStruct((TOP_K * n, d), F32),
        compiler_params=_params(("arbitrary",)),
        name="moe_experts",
    )(blk_e, nvalid, src, dst, x1, w_gate, w_up, w_down)


def _combine_kernel(x1_ref, y_ref, wts_ref, g_ref, b_ref, o_ref, *, alpha):
    w = wts_ref[...]
    y = w[:, 0:1] * y_ref[0] + w[:, 1:2] * y_ref[1]
    o_ref[...] = _layer_norm(alpha * x1_ref[...] + y, g_ref[...], b_ref[...])


def _moe_combine(x1, y2, wts, g, b, alpha, tm):
    n, d = x1.shape
    return pl.pallas_call(
        functools.partial(_combine_kernel, alpha=alpha),
        grid=(n // tm,),
        in_specs=[pl.BlockSpec((tm, d), lambda i: (i, 0)),
                  pl.BlockSpec((TOP_K, tm, d), lambda i: (0, i, 0)),
                  pl.BlockSpec((tm, LANES), lambda i: (i, 0)),
                  pl.BlockSpec((1, d), lambda i: (0, 0)), pl.BlockSpec((1, d), lambda i: (0, 0))],
        out_specs=pl.BlockSpec((tm, d), lambda i: (i, 0)),
        out_shape=jax.ShapeDtypeStruct((n, d), F32),
        compiler_params=_params(("parallel",)),
        name="moe_combine",
    )(x1, y2, wts, g, b)


def _dispatch_tables(eid, blk):
    n = eid.shape[0]
    nk = n * TOP_K
    nb = -(-(nk + N_EXPERTS * (blk - 1)) // blk)
    flat_e = eid.reshape(-1)
    onehot = (flat_e[:, None] == jnp.arange(N_EXPERTS, dtype=jnp.int32)[None, :]).astype(jnp.int32)
    ranks = jnp.cumsum(onehot, axis=0)
    counts = ranks[-1]
    rank = jnp.take_along_axis(ranks, flat_e[:, None], axis=1)[:, 0] - 1
    nblk_e = (counts + blk - 1) // blk
    blk_end = jnp.cumsum(nblk_e)
    blk_start = blk_end - nblk_e
    dest = blk_start[flat_e] * blk + rank
    slot = jnp.arange(nk, dtype=jnp.int32)
    src = jnp.zeros((nb * blk,), jnp.int32).at[dest].set(slot // TOP_K)
    dst = jnp.zeros((nb * blk,), jnp.int32).at[dest].set((slot % TOP_K) * n + slot // TOP_K)
    bidx = jnp.arange(nb, dtype=jnp.int32)
    blk_e = jnp.minimum(jnp.searchsorted(blk_end, bidx, side='right'), N_EXPERTS - 1).astype(jnp.int32)
    nvalid = jnp.clip(counts[blk_e] - (bidx - blk_start[blk_e]) * blk, 0, blk).astype(jnp.int32)
    nvalid = jnp.where(bidx < blk_end[-1], nvalid, 0)
    return blk_e, nvalid, src.reshape(nb, 1, blk), dst.reshape(nb, 1, blk)


def _pick(n, prefs):
    for p in prefs:
        if n % p == 0:
            return p
    return n


def _layer(x, past_k, past_v, past_logf, s0, shift_prev, wts, alpha):
    bsz, t, d = x.shape
    n = bsz * t
    xf = x.reshape(n, d)
    fox_width = wts['w_qkv'].shape[1] // 3
    n_heads = fox_width // FOX_HEAD_DIM
    rw_width = wts['w2p'].shape[1]
    rw_heads = rw_width // RWKV_HEAD_DIM
    rw_cols = wts['rwkv_cols']
    past = past_k.shape[1]

    tm = _pick(n, (512, 256))
    gates = _ln_matmul(_gates_kernel, xf, wts['ln_in_g'], wts['ln_in_b'], wts['w_gates'], BF16,
                       tm, 1024, "ln_proj_gates")
    p_rw = _ln_matmul(_rwkv_proj_kernel, xf, wts['ln_in_g'], wts['ln_in_b'], wts['w_rwkv'], F32,
                      tm, wts['w_rwkv'].shape[1] // 3, "ln_proj_rwkv")
    t_fox = -(-t // LANES) * LANES
    tq = _pick(t_fox, (ATTN_BLOCK, 256, LANES))
    x_fox = x if t_fox == t else jnp.pad(x, ((0, 0), (0, t_fox - t), (0, 0)))
    qt, k_f, k_b, v_f, vt, lf = _fox_proj(x_fox, wts['ln_in_g'], wts['ln_in_b'], wts['w_qkv'], wts['w_f'],
                                          wts['b_f'], wts['q_norm'], wts['k_norm'], tq)
    k_f, v_f, logf = k_f[:, :t], v_f[:, :t], lf[:, :t, :n_heads]
    if past:
        tk = ATTN_BLOCK
        lpad = -(-(past + t) // tk) * tk
        grow = lambda a, ax: jnp.pad(a, [(0, lpad - a.shape[ax]) if i == ax else (0, 0) for i in range(a.ndim)])
        lf_past = jnp.pad(past_logf.astype(F32), ((0, 0), (0, 0), (0, LANES - n_heads)))
        lf_all = grow(jnp.concatenate([lf_past, lf[:, :t]], axis=1), 1)
        k_all = grow(jnp.concatenate([jnp.swapaxes(past_k, 1, 2).astype(BF16), k_b[:, :, :t]], axis=2), 2)
        vt_past = jnp.transpose(past_v, (0, 2, 3, 1)).astype(BF16)
        vt_all = grow(jnp.concatenate([vt_past, vt[:, :, 0, :, :t]], axis=3), 3)
        vt_all = jnp.swapaxes(vt_all.reshape(bsz, n_heads, FOX_HEAD_DIM, lpad // tk, tk), 2, 3)
    else:
        tk, lf_all, k_all, vt_all = tq, lf, k_b, vt
    nf = _fgate_bias(lf_all, past + t, n_heads, tk)
    o_a = _fox_attention(qt, k_all, nf, vt_all, past, tk)[:, :t]

    tp = -(-t // RWKV_CHUNK) * RWKV_CHUNK
    tc = _pick(tp, (256, 128, 64))
    p3 = p_rw.reshape(bsz, t, -1)
    shift_new = p3[:, t - 1:t, :rw_cols]
    if tp != t:
        p3 = jnp.pad(p3, ((0, 0), (0, tp - t), (0, 0)))
    shift_in = jnp.pad(shift_prev.astype(F32), ((0, 0), (0, 0), (0, p3.shape[-1] - rw_cols)))
    n_grp = rw_heads // HEADS_PER_GROUP
    eye = jnp.eye(HEADS_PER_GROUP, dtype=F32)
    s0_g = s0.astype(F32).reshape(bsz, n_grp, HEADS_PER_GROUP, RWKV_HEAD_DIM, RWKV_HEAD_DIM)
    s0_bd = jnp.einsum('bghvk,hj->bghvjk', s0_g, eye).reshape(bsz, n_grp, MXU_DIM, MXU_DIM)
    o_b, s_bd = _rwkv_mix(p3, shift_in, s0_bd, wts['rwkv_vecs'], wts['w2p'], wts['a2p'], wts['g2p'], tc, t)
    s_new = jnp.einsum('bghvjk,hj->bghvk',
                       s_bd.reshape(bsz, n_grp, HEADS_PER_GROUP, RWKV_HEAD_DIM, HEADS_PER_GROUP, RWKV_HEAD_DIM),
                       eye).reshape(bsz, rw_heads, RWKV_HEAD_DIM, RWKV_HEAD_DIM)
    o_b = o_b[:, :t].reshape(n, rw_width)

    x1, ids, rw = _merge_out(xf, o_a.reshape(n, fox_width), o_b, gates, wts['w_a'], wts['w_b'], wts['w_o'],
                             wts['ln_in_g'], wts['ln_in_b'], wts['ln1_g'], wts['ln1_b'],
                             wts['w_router'], wts['b_router'], alpha, _pick(n, (256,)))
    blk = _pick(n * TOP_K // N_EXPERTS, (512, 256, 128, 64, 32, 16, 8))
    tables = _dispatch_tables(ids[:, :TOP_K], blk)
    y2 = _moe_experts(x1, *tables, wts['moe_w_gate'], wts['moe_w_up'], wts['moe_w_down'], blk)
    y = _moe_combine(x1, y2.reshape(TOP_K, n, d), rw, wts['ln2_g'], wts['ln2_b'], alpha, _pick(n, (512, 256)))

    hd = (bsz, t, n_heads, FOX_HEAD_DIM)
    return y.reshape(bsz, t, d), (k_f.reshape(hd), v_f.reshape(hd), logf, s_new, shift_new)


def _prepare_weights(l, ln_in_g, ln_in_b, w_in, fox_b_f, fox_q_norm, fox_k_norm, rwkv_mu, rwkv_w0, rwkv_w2,
                     rwkv_a0, rwkv_a2, rwkv_g2, rwkv_k_k, rwkv_k_a, rwkv_r_k, rwkv_gn_g, rwkv_gn_b,
                     w_branch_a, w_branch_b, w_out, ln1_g, ln1_b, router_group_w, router_group_b,
                     router_expert_w, router_expert_b, moe_w_gate, moe_w_up, moe_w_down, ln2_g, ln2_b):
    d = w_in.shape[1]
    fox_width = w_branch_a.shape[1]
    rw_width = w_branch_b.shape[1]
    n_heads = fox_width // FOX_HEAD_DIM
    gate_cols = 2 * d
    fox_cols = 3 * fox_width + n_heads
    rw_cols = 3 * rw_width + RWKV_DECAY_RANK + RWKV_ICLR_RANK + RWKV_GATE_RANK
    row = lambda a: a.astype(F32).reshape(1, -1)
    w = w_in[l]
    w_fox = w[:, gate_cols:gate_cols + fox_cols]
    w_rw = w[:, gate_cols + fox_cols:]
    lora = RWKV_DECAY_RANK + RWKV_ICLR_RANK
    assert lora == LANES
    gate_pad = -(-RWKV_GATE_RANK // LANES) * LANES
    cols_pad = 3 * rw_width + lora + gate_pad
    pad_c = cols_pad - rw_cols
    zeros = lambda r: jnp.zeros((r, rw_width), F32)
    wr = jnp.concatenate([router_group_w[l], router_expert_w[l]], axis=1).astype(F32)
    wr = jnp.pad(wr, ((0, 0), (0, LANES - wr.shape[1])))
    rb = jnp.concatenate([router_group_b[l], router_expert_b[l]]).astype(F32)
    return {
        'ln_in_g': row(ln_in_g), 'ln_in_b': row(ln_in_b),
        'w_gates': w[:, :gate_cols].astype(BF16),
        'w_qkv': w_fox[:, :3 * fox_width].astype(BF16),
        'w_f': jnp.pad(w_fox[:, 3 * fox_width:], ((0, 0), (0, LANES - n_heads))).astype(BF16),
        'b_f': jnp.pad(row(fox_b_f[l]), ((0, 0), (0, LANES - n_heads))),
        'q_norm': row(fox_q_norm[l]), 'k_norm': row(fox_k_norm[l]),
        'w_rwkv': jnp.pad(w_rw, ((0, 0), (0, pad_c))).astype(BF16),
        'rwkv_cols': rw_cols,
        'rwkv_vecs': [jnp.pad(row(rwkv_mu[l]), ((0, 0), (0, pad_c))), row(rwkv_w0[l]), row(rwkv_a0[l]),
                      row(rwkv_k_k[l]), row(rwkv_k_a[l]), row(rwkv_r_k[l]), row(rwkv_gn_g[l]),
                      row(rwkv_gn_b[l])],
        'w2p': jnp.concatenate([rwkv_w2[l].astype(F32), zeros(RWKV_ICLR_RANK)]).astype(BF16),
        'a2p': jnp.concatenate([zeros(RWKV_DECAY_RANK), rwkv_a2[l].astype(F32)]).astype(BF16),
        'g2p': jnp.concatenate([rwkv_g2[l].astype(F32), zeros(gate_pad - RWKV_GATE_RANK)]).astype(BF16),
        'w_a': w_branch_a[l].astype(BF16), 'w_b': w_branch_b[l].astype(BF16), 'w_o': w_out[l].astype(BF16),
        'ln1_g': row(ln1_g[l]), 'ln1_b': row(ln1_b[l]),
        'w_router': jnp.concatenate(_split3(wr)[:2], axis=1),
        'b_router': jnp.pad(row(rb), ((0, 0), (0, LANES - rb.shape[0]))),
        'moe_w_gate': moe_w_gate[l], 'moe_w_up': moe_w_up[l], 'moe_w_down': moe_w_down[l],
        'ln2_g': row(ln2_g[l]), 'ln2_b': row(ln2_b[l]),
    }


def kernel(x_prompt, x_sample, cache_fox_k, cache_fox_v, cache_fox_logf, state_rwkv, state_rwkv_shift,
           ln_in_g, ln_in_b, w_in, fox_b_f, fox_q_norm, fox_k_norm, rwkv_mu, rwkv_w0, rwkv_w2,
           rwkv_a0, rwkv_a2, rwkv_g2, rwkv_k_k, rwkv_k_a, rwkv_r_k, rwkv_gn_g, rwkv_gn_b,
           w_branch_a, w_branch_b, w_out, ln1_g, ln1_b, router_group_w, router_group_b,
           router_expert_w, router_expert_b, moe_w_gate, moe_w_up, moe_w_down, ln2_g, ln2_b):
    depth = w_in.shape[0]
    assert depth == 1, "the entry LayerNorm is fused into the layer's projections: single-layer trunk only"
    alpha = (2.0 * depth) ** 0.25
    bp = x_prompt.shape[0]
    n_fox_heads = fox_b_f.shape[1]
    rw_heads, rw_dim = state_rwkv.shape[2], state_rwkv.shape[3]
    rw_cols = state_rwkv_shift.shape[-1]
    wts = _prepare_weights(0, ln_in_g, ln_in_b, w_in, fox_b_f, fox_q_norm, fox_k_norm, rwkv_mu, rwkv_w0,
                           rwkv_w2, rwkv_a0, rwkv_a2, rwkv_g2, rwkv_k_k, rwkv_k_a, rwkv_r_k, rwkv_gn_g,
                           rwkv_gn_b, w_branch_a, w_branch_b, w_out, ln1_g, ln1_b, router_group_w,
                           router_group_b, router_expert_w, router_expert_b, moe_w_gate, moe_w_up,
                           moe_w_down, ln2_g, ln2_b)
    xp, new_p = _layer(x_prompt, jnp.zeros((bp, 0, n_fox_heads, FOX_HEAD_DIM), F32),
                       jnp.zeros((bp, 0, n_fox_heads, FOX_HEAD_DIM), F32),
                       jnp.zeros((bp, 0, n_fox_heads), F32),
                       jnp.zeros((bp, rw_heads, rw_dim, rw_dim), F32),
                       jnp.zeros((bp, 1, rw_cols), F32), wts, alpha)
    xs, new_s = _layer(x_sample, cache_fox_k[0], cache_fox_v[0], cache_fox_logf[0], state_rwkv[0],
                       state_rwkv_shift[0], wts, alpha)
    return (xp, xs) + tuple(a[None] for a in new_p) + tuple(a[None] for a in new_s)
```

```python
import functools

import jax
import jax.numpy as jnp
from jax import lax
from jax.experimental import pallas as pl
from jax.experimental.pallas import tpu as pltpu

F32 = jnp.float32
BF16 = jnp.bfloat16

FOX_HEAD_DIM = 128
RWKV_HEAD_DIM = 64
RWKV_DECAY_RANK = 64
RWKV_ICLR_RANK = 64
RWKV_GATE_RANK = 160
RWKV_GN_EPS = 64e-5
N_GROUPS = 4
EXPERTS_PER_GROUP = 8
N_EXPERTS = N_GROUPS * EXPERTS_PER_GROUP
TOP_K = 2
LN_EPS = 1e-5
QK_EPS = 1e-6
NEG_INF = -1e30
LOG2_E = 1.4426950408889634

LANES = 128
MXU_DIM = 256
VMEM_LIMIT_BYTES = 56 * 1024 * 1024

PROJ_ROWS = 1024
ATTN_BLOCK = 512
RWKV_CHUNK = 64
HEADS_PER_GROUP = MXU_DIM // RWKV_HEAD_DIM


def _params(semantics):
    return pltpu.CompilerParams(dimension_semantics=semantics, vmem_limit_bytes=VMEM_LIMIT_BYTES)


def _dot(a, b):
    return jnp.dot(a.astype(BF16), b.astype(BF16), preferred_element_type=F32)


def _dot_nt(a, b):
    return lax.dot_general(a.astype(BF16), b.astype(BF16), (((1,), (1,)), ((), ())),
                           preferred_element_type=F32)


def _split3(x):
    h1 = x.astype(BF16)
    r1 = x - h1.astype(F32)
    h2 = r1.astype(BF16)
    h3 = (r1 - h2.astype(F32)).astype(BF16)
    return h1, h2, h3


def _dot_exact_rhs(x, m_bf16):
    h1, h2, h3 = _split3(x)
    d = lambda h: jnp.dot(h, m_bf16, preferred_element_type=F32)
    return d(h1) + d(h2) + d(h3)


def _dot_exact_lhs(m_bf16, x):
    h1, h2, h3 = _split3(x)
    d = lambda h: jnp.dot(m_bf16, h, preferred_element_type=F32)
    return d(h1) + d(h2) + d(h3)


def _layer_norm(x, g, b):
    mu = jnp.mean(x, axis=-1, keepdims=True)
    xc = x - mu
    var = jnp.mean(xc * xc, axis=-1, keepdims=True)
    return xc * lax.rsqrt(var + LN_EPS) * g + b


def _sigmoid(x):
    return 1.0 / (1.0 + jnp.exp(-x))


def _log_sigmoid(x):
    return jnp.minimum(x, 0.0) - jnp.log(1.0 + jnp.exp(-jnp.abs(x)))


def _ln_kernel(x_ref, g_ref, b_ref, o_ref):
    o_ref[...] = _layer_norm(x_ref[...], g_ref[...], b_ref[...]).astype(o_ref.dtype)


def _entry_norm(x, ln_g, ln_b, tm):
    n, d = x.shape
    return pl.pallas_call(
        _ln_kernel,
        grid=(n // tm,),
        in_specs=[pl.BlockSpec((tm, d), lambda i: (i, 0)),
                  pl.BlockSpec((1, d), lambda i: (0, 0)), pl.BlockSpec((1, d), lambda i: (0, 0))],
        out_specs=pl.BlockSpec((tm, d), lambda i: (i, 0)),
        out_shape=jax.ShapeDtypeStruct((n, d), BF16),
        compiler_params=_params(("parallel",)),
        name="entry_norm",
    )(x, ln_g, ln_b)


def _gates_kernel(x_ref, w_ref, o_ref):
    y = jnp.dot(x_ref[...], w_ref[...], preferred_element_type=F32)
    o_ref[...] = _sigmoid(y).astype(o_ref.dtype)


def _rwkv_proj_kernel(x_ref, w_ref, o_ref):
    o_ref[...] = jnp.dot(x_ref[...], w_ref[...], preferred_element_type=F32)


def _proj_matmul(body, xn, w, out_dtype, tm, tn, name):
    n, d = xn.shape
    ncol = w.shape[1]
    return pl.pallas_call(
        body,
        grid=(n // tm, ncol // tn),
        in_specs=[pl.BlockSpec((tm, d), lambda i, j: (i, 0)),
                  pl.BlockSpec((d, tn), lambda i, j: (0, j))],
        out_specs=pl.BlockSpec((tm, tn), lambda i, j: (i, j)),
        out_shape=jax.ShapeDtypeStruct((n, ncol), out_dtype),
        compiler_params=_params(("parallel", "parallel")),
        name=name,
    )(xn, w)


def _fox_proj_kernel(xn_ref, w_ref, wf_ref, bf_ref, qn_ref, kn_ref,
                     qt_ref, kf_ref, kb_ref, vf_ref, vt_ref, lf_ref, *, n_heads):
    j = pl.program_id(2)
    y = jnp.dot(xn_ref[...], w_ref[...], preferred_element_type=F32)

    def rms(yh, gain):
        ms = jnp.mean(yh * yh, axis=-1, keepdims=True)
        return yh * lax.rsqrt(ms + QK_EPS) * gain

    @pl.when(j == 0)
    def _():
        scale = FOX_HEAD_DIM ** -0.5 * LOG2_E
        for h in range(n_heads):
            yh = y[:, h * FOX_HEAD_DIM:(h + 1) * FOX_HEAD_DIM]
            qt_ref[h] = (rms(yh, qn_ref[...]) * scale).T.astype(BF16)
        fl = jnp.dot(xn_ref[...], wf_ref[...], preferred_element_type=F32)
        lf_ref[...] = _log_sigmoid(fl + bf_ref[...])

    @pl.when(j == 1)
    def _():
        for h in range(n_heads):
            sl = slice(h * FOX_HEAD_DIM, (h + 1) * FOX_HEAD_DIM)
            kh = rms(y[:, sl], kn_ref[...])
            kf_ref[:, h, :] = kh
            kb_ref[h] = kh.astype(BF16)

    @pl.when(j == 2)
    def _():
        for h in range(n_heads):
            vh = y[:, h * FOX_HEAD_DIM:(h + 1) * FOX_HEAD_DIM]
            vf_ref[:, h, :] = vh
            vt_ref[h] = vh.T.astype(BF16)


def _fox_proj(xn, w_qkv, w_f, b_f, q_norm, k_norm, tm):
    bsz, t, d = xn.shape
    width = w_qkv.shape[1] // 3
    n_heads = width // FOX_HEAD_DIM
    hm = pl.BlockSpec((None, n_heads, tm, FOX_HEAD_DIM), lambda b, i, j: (b, 0, i, 0))
    tr = lambda: pl.BlockSpec((None, n_heads, None, FOX_HEAD_DIM, tm), lambda b, i, j: (b, 0, i, 0, 0))
    tok4 = pl.BlockSpec((None, tm, n_heads, FOX_HEAD_DIM), lambda b, i, j: (b, i, 0, 0))
    const = lambda r, c: pl.BlockSpec((r, c), lambda b, i, j: (0, 0))
    tr_shape = jax.ShapeDtypeStruct((bsz, n_heads, t // tm, FOX_HEAD_DIM, tm), BF16)
    tok_shape = jax.ShapeDtypeStruct((bsz, t, n_heads, FOX_HEAD_DIM), F32)
    return pl.pallas_call(
        functools.partial(_fox_proj_kernel, n_heads=n_heads),
        grid=(bsz, t // tm, 3),
        in_specs=[
            pl.BlockSpec((None, tm, d), lambda b, i, j: (b, i, 0)),
            pl.BlockSpec((d, width), lambda b, i, j: (0, j)),
            const(d, LANES), const(1, LANES),
            const(1, FOX_HEAD_DIM), const(1, FOX_HEAD_DIM),
        ],
        out_specs=[tr(), tok4, hm, tok4, tr(), pl.BlockSpec((None, tm, LANES), lambda b, i, j: (b, i, 0))],
        out_shape=[tr_shape, tok_shape, jax.ShapeDtypeStruct((bsz, n_heads, t, FOX_HEAD_DIM), BF16),
                   tok_shape, tr_shape, jax.ShapeDtypeStruct((bsz, t, LANES), F32)],
        compiler_params=_params(("parallel", "parallel", "arbitrary")),
        name="proj_fox",
    )(xn, w_qkv, w_f, b_f, q_norm, k_norm)


def _fgate_bias_kernel(lf_ref, o_ref, carry, *, length, n_heads):
    t = pl.program_id(1)
    tt = lf_ref.shape[0]

    @pl.when(t == 0)
    def _():
        carry[...] = jnp.zeros_like(carry)

    ri = lax.broadcasted_iota(jnp.int32, (tt, tt), 0)
    rj = lax.broadcasted_iota(jnp.int32, (tt, tt), 1)
    tri = jnp.where(rj <= ri, 1.0, 0.0).astype(BF16)
    csum = _dot_exact_lhs(tri, lf_ref[...]) + carry[...]
    carry[...] = csum[tt - 1:tt, :]
    pos = t * tt + lax.broadcasted_iota(jnp.int32, csum.shape, 0)
    neg = jnp.where(pos < length, -LOG2_E * csum, NEG_INF)
    pieces = jnp.concatenate(_split3(neg), axis=1)
    sr = lax.broadcasted_iota(jnp.int32, (3 * LANES, LANES), 0)
    sc = lax.broadcasted_iota(jnp.int32, (3 * LANES, LANES), 1)
    for h in range(n_heads):
        sel = jnp.where(sr % LANES == h, jnp.where(sr // LANES == sc, 1.0, 0.0), 0.0).astype(BF16)
        o_ref[h] = jnp.dot(pieces, sel, preferred_element_type=F32).astype(BF16)


def _fgate_bias(lf, length, n_heads, tt):
    bsz, lp, _ = lf.shape
    return pl.pallas_call(
        functools.partial(_fgate_bias_kernel, length=length, n_heads=n_heads),
        grid=(bsz, lp // tt),
        in_specs=[pl.BlockSpec((None, tt, LANES), lambda b, t: (b, t, 0))],
        out_specs=pl.BlockSpec((None, n_heads, tt, LANES), lambda b, t: (b, 0, t, 0)),
        out_shape=jax.ShapeDtypeStruct((bsz, n_heads, lp, LANES), BF16),
        scratch_shapes=[pltpu.VMEM((1, LANES), F32)],
        compiler_params=_params(("parallel", "arbitrary")),
        name="fgate_bias",
    )(lf)


N_BIAS_PIECES = 3


ATTN_HEADS_PER_STEP = 2


def _attn_kernel(qt_ref, k_ref, nf_ref, vt_ref, o_ref, *, tq, tk, past):
    i = pl.program_id(2)
    n_heads = qt_ref.shape[0]
    tw = min(tq, MXU_DIM)
    chains = [(h, c) for h in range(n_heads) for c in range(tq // tw)]
    ones_rows = jnp.where(lax.broadcasted_iota(jnp.int32, (LANES, tw), 0) < N_BIAS_PIECES, 1.0, 0.0)
    qa = [jnp.concatenate([qt_ref[h, :, c * tw:(c + 1) * tw], ones_rows.astype(BF16)], axis=0)
          for h, c in chains]
    n_full = (past + i * tq) // tk

    def update(carry, s, vt):
        m, l, acc = carry
        m_new = jnp.maximum(m, jnp.max(s, axis=0, keepdims=True))
        p = jnp.exp2(s - m_new)
        alpha = jnp.exp2(m - m_new)
        l = alpha * l + jnp.sum(p, axis=0, keepdims=True)
        acc = alpha * acc + jnp.dot(vt, p.astype(BF16), preferred_element_type=F32)
        return m_new, l, acc

    def scores(j, causal=False):
        start = pl.multiple_of(j * tk, tk)
        ka = [jnp.concatenate([k_ref[h, pl.ds(start, tk), :], nf_ref[h, pl.ds(start, tk), :]], axis=1)
              for h in range(n_heads)]
        ss = [jnp.dot(ka[h], qa[n], preferred_element_type=F32) for n, (h, c) in enumerate(chains)]
        if causal:
            key = lax.broadcasted_iota(jnp.int32, (tk, tw), 0)
            qry = lax.broadcasted_iota(jnp.int32, (tk, tw), 1)
            ss = [jnp.where(key <= qry + c * tw, s, NEG_INF) for s, (h, c) in zip(ss, chains)]
        return ss

    def absorb(carry, ss, j):
        return tuple(update(carry[n], ss[n], vt_ref[h, j]) for n, (h, c) in enumerate(chains))

    def two_blocks(pair, carry):
        sa, sb = scores(2 * pair), scores(2 * pair + 1)
        return absorb(absorb(carry, sa, 2 * pair), sb, 2 * pair + 1)

    init = (jnp.full((1, tw), NEG_INF, F32), jnp.zeros((1, tw), F32), jnp.zeros((FOX_HEAD_DIM, tw), F32))
    carry = lax.fori_loop(0, n_full // 2, two_blocks, tuple(init for _ in chains))
    last_full = jnp.maximum(n_full - 1, 0)
    carry = lax.cond(n_full % 2 == 1, lambda c: absorb(c, scores(last_full), last_full), lambda c: c, carry)
    carry = absorb(carry, scores(n_full, causal=True), n_full)
    for n, (h, c) in enumerate(chains):
        _, l, acc = carry[n]
        o_ref[c * tw:(c + 1) * tw, h * FOX_HEAD_DIM:(h + 1) * FOX_HEAD_DIM] = (acc / l).T.astype(o_ref.dtype)


def _fox_attention(qt, k, nf, vt, past, tk):
    bsz, n_heads, nq, dh, tq = qt.shape
    lp = k.shape[2]
    hps = ATTN_HEADS_PER_STEP
    assert past % tk == 0 and (tq == tk or nq == 1) and tq <= tk and lp % tk == 0 and n_heads % hps == 0
    whole = lambda a: pl.BlockSpec((None, hps) + a.shape[2:], lambda b, h, i: (b, h) + (0,) * (a.ndim - 2))
    return pl.pallas_call(
        functools.partial(_attn_kernel, tq=tq, tk=tk, past=past),
        grid=(bsz, n_heads // hps, nq),
        in_specs=[pl.BlockSpec((None, hps, None, dh, tq), lambda b, h, i: (b, h, i, 0, 0)),
                  whole(k), whole(nf), whole(vt)],
        out_specs=pl.BlockSpec((None, tq, hps * dh), lambda b, h, i: (b, i, h)),
        out_shape=jax.ShapeDtypeStruct((bsz, nq * tq, n_heads * dh), BF16),
        compiler_params=_params(("parallel", "parallel", "arbitrary")),
        name="fox_attn",
    )(qt, k, nf, vt)


def _rwkv_kernel(p_ref, shift_ref, s0_ref, mu_ref, w0_ref, a0_ref, kk_ref, ka_ref, rk_ref,
                 gng_ref, gnb_ref, w2_ref, a2_ref, g2_ref, o_ref, sout_ref,
                 state, prev_row, *, tc, t_valid):
    c_len = RWKV_CHUNK
    t = pl.program_id(1)
    width = o_ref.shape[-1]
    n_groups = width // MXU_DIM

    @pl.when(t == 0)
    def _():
        state[...] = s0_ref[...]
        prev_row[...] = shift_ref[...]

    p = p_ref[...]
    ridx = lax.broadcasted_iota(jnp.int32, p.shape, 0)
    prev = jnp.where(ridx == 0, jnp.broadcast_to(prev_row[...], p.shape), pltpu.roll(p, 1, 0))
    prev_row[...] = p[tc - 1:tc, :]
    xs = p + (prev - p) * mu_ref[...]
    r = xs[:, 0:width]
    k = xs[:, width:2 * width]
    v = xs[:, 2 * width:3 * width]
    lora_in = xs[:, 3 * width:3 * width + LANES]
    gate_in = xs[:, 3 * width + LANES:]
    zw = _dot(jnp.tanh(lora_in), w2_ref[...])
    za = _dot(lora_in, a2_ref[...])
    g = _dot(_sigmoid(gate_in), g2_ref[...])
    zz = w0_ref[...] + zw
    w_log = jnp.minimum(zz, 0.0) - jnp.log(1.0 + jnp.exp(-jnp.abs(zz))) - 0.5
    lw = -jnp.exp(w_log)
    iclr = _sigmoid(a0_ref[...] + za)

    hr = lax.broadcasted_iota(jnp.int32, (MXU_DIM, MXU_DIM), 0) // RWKV_HEAD_DIM
    hc = lax.broadcasted_iota(jnp.int32, (MXU_DIM, MXU_DIM), 1) // RWKV_HEAD_DIM
    same_head = hr == hc
    bd_f32 = jnp.where(same_head, 1.0, 0.0)
    ones_bd = bd_f32.astype(BF16)

    def head_sum(x):
        hi = x.astype(BF16)
        lo = (x - hi.astype(F32)).astype(BF16)
        parts = []
        for gi in range(n_groups):
            ls = slice(gi * MXU_DIM, (gi + 1) * MXU_DIM)
            parts.append(jnp.dot(hi[:, ls], ones_bd, preferred_element_type=F32)
                         + jnp.dot(lo[:, ls], ones_bd, preferred_element_type=F32))
        return jnp.concatenate(parts, axis=-1)

    kk = k * kk_ref[...]
    kk = kk * lax.rsqrt(jnp.maximum(head_sum(kk * kk), 1e-24))
    k = k * (1.0 + (iclr - 1.0) * ka_ref[...])
    if t_valid < tc:
        live = lax.broadcasted_iota(jnp.int32, (tc, width), 0) < t_valid
        lw = jnp.where(live, lw, 0.0)
        kk = jnp.where(live, kk, 0.0)
        k = jnp.where(live, k, 0.0)
        v = jnp.where(live, v, 0.0)

    ti = lax.broadcasted_iota(jnp.int32, (tc, tc), 0)
    tj = lax.broadcasted_iota(jnp.int32, (tc, tc), 1)
    tri = jnp.where(ti // c_len == tj // c_len, jnp.where(tj <= ti, 1.0, 0.0), 0.0).astype(BF16)
    gcum = _dot_exact_lhs(tri, lw)
    e_in = jnp.exp(gcum)
    e_inv = jnp.exp(-gcum)
    at_all = -kk * jnp.exp(gcum - lw)
    rt_all = r * e_in
    bt_all = kk * iclr * e_inv
    kt_all = k * e_inv
    bonus = head_sum(r * k * rk_ref[...]) * v

    row = lax.broadcasted_iota(jnp.int32, (c_len, MXU_DIM), 0)
    lane = lax.broadcasted_iota(jnp.int32, (c_len, MXU_DIM), 1) % c_len
    strict = jnp.where(lane < row, 1.0, 0.0)
    incl = jnp.where(lane <= row, 1.0, 0.0)
    eye_w = jnp.where(lane == row, 1.0, 0.0)

    def bd(x):
        return jnp.concatenate([x.astype(BF16)] * HEADS_PER_GROUP, axis=0) * ones_bd

    n_sq = c_len.bit_length() - 1
    n_chunks = tc // c_len
    units = [(ci, gi) for ci in range(n_chunks) for gi in range(n_groups)]

    def cut(x, u):
        ci, gi = u
        return x[ci * c_len:(ci + 1) * c_len, gi * MXU_DIM:(gi + 1) * MXU_DIM]

    at = [cut(at_all, u) for u in units]
    rt = [cut(rt_all, u) for u in units]
    bt = [cut(bt_all, u) for u in units]
    kt = [cut(kt_all, u) for u in units]
    vv = [cut(v, u) for u in units]
    ar = [jnp.concatenate([a, r_], axis=0) for a, r_ in zip(at, rt)]
    ab = [_dot_nt(x, bd(b_)) for x, b_ in zip(ar, bt)]
    ak = [_dot_nt(x, bd(k_)) for x, k_ in zip(ar, kt)]
    pw = [x[:c_len] * strict for x in ab]
    a_rb = [x[c_len:] * incl for x in ab]
    a_ak = [x[:c_len] * strict for x in ak]
    a_rk = [x[c_len:] * incl for x in ak]
    tm = [eye_w + x for x in pw]
    pw = [_dot(x, bd(x)) for x in pw]
    for js in range(1, n_sq):
        if js < n_sq - 1:
            tp = [_dot(jnp.concatenate([t_, x], axis=0), bd(x)) for t_, x in zip(tm, pw)]
            tm = [t_ + y_[:c_len] for t_, y_ in zip(tm, tp)]
            pw = [y_[c_len:] for y_ in tp]
        else:
            tm = [t_ + _dot(t_, bd(x)) for t_, x in zip(tm, pw)]
    bdv = [bd(x) for x in vv]
    a_hat = [_dot(t_, bd(a)) for t_, a in zip(tm, at)]
    av = [_dot(x, b_) for x, b_ in zip(a_ak, bdv)]
    u_hat = [_dot(t_, bd(x)) for t_, x in zip(tm, av)]
    r_hat = [r_ + _dot(x, bd(a)) for r_, x, a in zip(rt, a_rb, a_hat)]
    y_hat = [_dot(x, bd(uh)) + _dot(z, b_) for x, uh, z, b_ in zip(a_rb, u_hat, a_rk, bdv)]
    lhs = [jnp.concatenate([a, r_], axis=0) for a, r_ in zip(a_hat, r_hat)]

    st = [state[gi] for gi in range(n_groups)]
    y_rows = []
    for ci in range(n_chunks):
        gend = jnp.exp(gcum[(ci + 1) * c_len - 1:(ci + 1) * c_len, :])
        us = [ci * n_groups + gi for gi in range(n_groups)]
        ge = [gend[:, gi * MXU_DIM:(gi + 1) * MXU_DIM] for gi in range(n_groups)]
        uy = [_dot_nt(lhs[u], st[gi]) for gi, u in enumerate(us)]
        uu = [uy[gi][:c_len] + u_hat[u] for gi, u in enumerate(us)]
        y_rows.append(jnp.concatenate([uy[gi][c_len:] + y_hat[u] for gi, u in enumerate(us)], axis=1))
        uv_t = [jnp.concatenate([uu[gi], vv[u]], axis=0).T for gi, u in enumerate(us)]
        bk = [jnp.concatenate([bt[u] * ge[gi], kt[u] * ge[gi]], axis=0) for gi, u in enumerate(us)]
        st = [st[gi] * ge[gi] + _dot(uv_t[gi], bk[gi]) * bd_f32 for gi in range(n_groups)]
    for gi in range(n_groups):
        state[gi] = st[gi]

    y = jnp.concatenate(y_rows, axis=0)
    inv_n = 1.0 / RWKV_HEAD_DIM
    mean = head_sum(y) * inv_n
    yc = y - mean
    var = head_sum(yc * yc) * inv_n
    yn = yc * lax.rsqrt(var + RWKV_GN_EPS) * gng_ref[...] + gnb_ref[...]
    o_ref[...] = ((yn + bonus) * g).astype(o_ref.dtype)

    @pl.when(t == pl.num_programs(1) - 1)
    def _():
        sout_ref[...] = state[...]


def _rwkv_mix(p, shift_prev, s0_bd, vecs, w2p, a2p, g2p, tc, t_valid):
    bsz, tp, cols = p.shape
    width = w2p.shape[1]
    n_groups = width // MXU_DIM
    vec_specs = [pl.BlockSpec((1, a.shape[1]), lambda b, t: (0, 0)) for a in vecs]
    mat = lambda a: pl.BlockSpec(a.shape, lambda b, t: (0, 0))
    st_spec = pl.BlockSpec((None, n_groups, MXU_DIM, MXU_DIM), lambda b, t: (b, 0, 0, 0))
    return pl.pallas_call(
        functools.partial(_rwkv_kernel, tc=tc, t_valid=t_valid),
        grid=(bsz, tp // tc),
        in_specs=[pl.BlockSpec((None, tc, cols), lambda b, t: (b, t, 0)),
                  pl.BlockSpec((None, 1, cols), lambda b, t: (b, 0, 0)),
                  st_spec] + vec_specs + [mat(w2p), mat(a2p), mat(g2p)],
        out_specs=[pl.BlockSpec((None, tc, width), lambda b, t: (b, t, 0)), st_spec],
        out_shape=[jax.ShapeDtypeStruct((bsz, tp, width), BF16),
                   jax.ShapeDtypeStruct(s0_bd.shape, F32)],
        scratch_shapes=[pltpu.VMEM((n_groups, MXU_DIM, MXU_DIM), F32), pltpu.VMEM((1, cols), F32)],
        compiler_params=_params(("parallel", "arbitrary")),
        name="rwkv_mix",
    )(p, shift_prev, s0_bd, *vecs, w2p, a2p, g2p)


def _merge_kernel(x_ref, oa_ref, ob_ref, gt_ref, wa_ref, wb_ref, wo_ref, lig_ref, lib_ref,
                  l1g_ref, l1b_ref, wr_ref, rb_ref, x1_ref, ids_ref, wts_ref, *, alpha):
    d = x_ref.shape[-1]
    xn = _layer_norm(x_ref[...], lig_ref[...], lib_ref[...])
    ya = jnp.dot(oa_ref[...], wa_ref[...], preferred_element_type=F32)
    yb = jnp.dot(ob_ref[...], wb_ref[...], preferred_element_type=F32)
    merged = gt_ref[:, :d].astype(F32) * ya + gt_ref[:, d:].astype(F32) * yb
    out = jnp.dot(merged.astype(BF16), wo_ref[...], preferred_element_type=F32)
    x1 = _layer_norm(alpha * xn + out, l1g_ref[...], l1b_ref[...])
    x1_ref[...] = x1

    h1, h2, _ = _split3(x1)
    r1 = jnp.dot(h1, wr_ref[...], preferred_element_type=F32)
    r2 = jnp.dot(h2, wr_ref[...], preferred_element_type=F32)
    logits = (r1[:, :LANES] + (r1[:, LANES:] + r2[:, :LANES]) + r2[:, LANES:]) + rb_ref[...]
    lane = lax.broadcasted_iota(jnp.int32, logits.shape, 1).astype(F32)
    big = 1e9

    def first_max(vals):
        mx = jnp.max(vals, axis=-1, keepdims=True)
        idx = jnp.min(jnp.where(vals == mx, lane, big), axis=-1, keepdims=True)
        return mx, idx

    is_grp = lane < N_GROUPS
    gmax, grp = first_max(jnp.where(is_grp, logits, NEG_INF))
    p_grp = 1.0 / jnp.sum(jnp.where(is_grp, jnp.exp(logits - gmax), 0.0), axis=-1, keepdims=True)
    lo = N_GROUPS + grp * EXPERTS_PER_GROUP
    elog = jnp.where(lane >= lo, jnp.where(lane < lo + EXPERTS_PER_GROUP, logits, NEG_INF), NEG_INF)
    v1, i1 = first_max(elog)
    v2, i2 = first_max(jnp.where(lane == i1, NEG_INF, elog))
    e2 = jnp.exp(v2 - v1)
    w1 = p_grp / (1.0 + e2)
    w2 = p_grp * e2 / (1.0 + e2)
    ids = jnp.where(lane == 0, i1 - N_GROUPS, jnp.where(lane == 1, i2 - N_GROUPS, 0.0))
    ids_ref[...] = ids.astype(jnp.int32)
    wts_ref[...] = jnp.where(lane == 0, w1, jnp.where(lane == 1, w2, 0.0))


def _merge_out(x, oa, ob, gates, wa, wb, wo, lig, lib, l1g, l1b, w_router, rbias, alpha, tm):
    n, d = x.shape
    half = oa.shape[1]
    row = lambda c: pl.BlockSpec((tm, c), lambda i: (i, 0))
    const = lambda a: pl.BlockSpec(a.shape, lambda i: (0, 0), pipeline_mode=pl.Buffered(1))
    return pl.pallas_call(
        functools.partial(_merge_kernel, alpha=alpha),
        grid=(n // tm,),
        in_specs=[row(d), row(half), row(half), row(2 * d), const(wa), const(wb), const(wo),
                  const(lig), const(lib), const(l1g), const(l1b), const(w_router), const(rbias)],
        out_specs=[row(d), row(LANES), row(LANES)],
        out_shape=[jax.ShapeDtypeStruct((n, d), F32), jax.ShapeDtypeStruct((n, LANES), jnp.int32),
                   jax.ShapeDtypeStruct((n, LANES), F32)],
        compiler_params=_params(("parallel",)),
        name="merge_out",
    )(x, oa, ob, gates, wa, wb, wo, lig, lib, l1g, l1b, w_router, rbias)


DMA_ISSUE_UNROLL = 8


def _wait_rows(make_row_copy, count):
    lax.fori_loop(0, count, lambda q, c: (make_row_copy(0).wait(), c)[1], 0, unroll=DMA_ISSUE_UNROLL)


def _dispatch_kernel(zblk_ref, dest_ref, x_ref, xs_hbm, zeros, sem, zsem):
    i = pl.program_id(0)
    tm = x_ref.shape[0]
    blk = zeros.shape[0]

    def zero_block(z):
        start = pl.multiple_of(zblk_ref[z] * blk, blk)
        return pltpu.make_async_copy(zeros, xs_hbm.at[pl.ds(start, blk), :], zsem)

    @pl.when(i == 0)
    def _():
        zeros[...] = jnp.zeros_like(zeros)
        for z in range(zblk_ref.shape[0]):
            pl.when(zblk_ref[z] >= 0)(lambda z=z: zero_block(z).start())
        for z in range(zblk_ref.shape[0]):
            pl.when(zblk_ref[z] >= 0)(lambda z=z: zero_block(z).wait())

    def to_slot(r, kslot):
        return pltpu.make_async_copy(x_ref.at[pl.ds(r, 1), :],
                                     xs_hbm.at[pl.ds(dest_ref[0, TOP_K * r + kslot], 1), :], sem)

    def body(r, c):
        for kslot in range(TOP_K):
            to_slot(r, kslot).start()
        return c
    lax.fori_loop(0, tm, body, 0, unroll=DMA_ISSUE_UNROLL)
    _wait_rows(lambda q: to_slot(0, 0), TOP_K * tm)


def _moe_dispatch(x1, zero_blocks, dest_tiles, n_rows, blk, tm):
    n, d = x1.shape
    grid_spec = pltpu.PrefetchScalarGridSpec(
        num_scalar_prefetch=1,
        grid=(n // tm,),
        in_specs=[pl.BlockSpec((None, 1, TOP_K * tm), lambda i, zb: (i, 0, 0), memory_space=pltpu.SMEM),
                  pl.BlockSpec((tm, d), lambda i, zb: (i, 0))],
        out_specs=pl.BlockSpec(memory_space=pl.ANY),
        scratch_shapes=[pltpu.VMEM((blk, d), F32), pltpu.SemaphoreType.DMA(()), pltpu.SemaphoreType.DMA(())],
    )
    return pl.pallas_call(
        _dispatch_kernel,
        grid_spec=grid_spec,
        out_shape=jax.ShapeDtypeStruct((n_rows, d), F32),
        compiler_params=_params(("arbitrary",)),
        name="moe_dispatch",
    )(zero_blocks, dest_tiles, x1)


def _experts_kernel(blk_e_ref, xblk_ref, nvalid_ref, xs_ref, wg_ref, wu_ref, wd_ref, ys_ref):
    j = pl.program_id(0)

    @pl.when(nvalid_ref[j] > 0)
    def _():
        xv = xs_ref[...].astype(BF16)
        hg = jnp.dot(xv, wg_ref[...].astype(BF16), preferred_element_type=F32)
        hu = jnp.dot(xv, wu_ref[...].astype(BF16), preferred_element_type=F32)
        h = hg * _sigmoid(hg) * hu
        ys_ref[...] = jnp.dot(h.astype(BF16), wd_ref[...].astype(BF16), preferred_element_type=F32)

    @pl.when(nvalid_ref[j] == 0)
    def _():
        ys_ref[...] = jnp.zeros_like(ys_ref)


def _moe_experts(xs, blk_e, xblk, nvalid, w_gate, w_up, w_down, blk):
    d = xs.shape[1]
    nb = blk_e.shape[0]
    de = w_gate.shape[-1]
    grid_spec = pltpu.PrefetchScalarGridSpec(
        num_scalar_prefetch=3,
        grid=(nb,),
        in_specs=[
            pl.BlockSpec((blk, d), lambda j, be, xb, nv: (xb[j], 0)),
            pl.BlockSpec((None, d, de), lambda j, be, xb, nv: (be[j], 0, 0)),
            pl.BlockSpec((None, d, de), lambda j, be, xb, nv: (be[j], 0, 0)),
            pl.BlockSpec((None, de, d), lambda j, be, xb, nv: (be[j], 0, 0)),
        ],
        out_specs=pl.BlockSpec((blk, d), lambda j, be, xb, nv: (j, 0)),
    )
    return pl.pallas_call(
        _experts_kernel,
        grid_spec=grid_spec,
        out_shape=jax.ShapeDtypeStruct((nb * blk, d), F32),
        compiler_params=_params(("arbitrary",)),
        name="moe_experts",
    )(blk_e, xblk, nvalid, xs, w_gate, w_up, w_down)


def _combine_kernel(cur_ref, nxt_ref, x1_ref, wts_ref, g_ref, b_ref, ys_hbm, o_ref, ybuf, sem, *, alpha):
    i = pl.program_id(0)
    tm = x1_ref.shape[0]
    slot = i % 2

    def row(tbl_ref, s, r, kslot):
        return pltpu.make_async_copy(ys_hbm.at[pl.ds(tbl_ref[0, TOP_K * r + kslot], 1), :],
                                     ybuf.at[s, pl.ds(kslot * tm + r, 1), :], sem.at[s])

    def fetch(tbl_ref, s):
        def body(r, c):
            for kslot in range(TOP_K):
                row(tbl_ref, s, r, kslot).start()
            return c
        lax.fori_loop(0, tm, body, 0, unroll=DMA_ISSUE_UNROLL)

    @pl.when(i == 0)
    def _():
        fetch(cur_ref, 0)

    @pl.when(i + 1 < pl.num_programs(0))
    def _():
        fetch(nxt_ref, 1 - slot)

    _wait_rows(lambda q: row(cur_ref, slot, 0, 0), TOP_K * tm)
    w = wts_ref[...]
    y = w[:, 0:1] * ybuf[slot, 0:tm, :] + w[:, 1:2] * ybuf[slot, tm:2 * tm, :]
    o_ref[...] = _layer_norm(alpha * x1_ref[...] + y, g_ref[...], b_ref[...])


def _moe_combine(x1, ys, dest_tiles, wts, g, b, alpha, tm):
    n, d = x1.shape
    last = n // tm - 1
    tbl = lambda f: pl.BlockSpec((None, 1, TOP_K * tm), lambda i: (f(i), 0, 0), memory_space=pltpu.SMEM)
    return pl.pallas_call(
        functools.partial(_combine_kernel, alpha=alpha),
        grid=(n // tm,),
        in_specs=[tbl(lambda i: i), tbl(lambda i: jnp.minimum(i + 1, last)),
                  pl.BlockSpec((tm, d), lambda i: (i, 0)),
                  pl.BlockSpec((tm, LANES), lambda i: (i, 0)),
                  pl.BlockSpec((1, d), lambda i: (0, 0)), pl.BlockSpec((1, d), lambda i: (0, 0)),
                  pl.BlockSpec(memory_space=pl.ANY)],
        out_specs=pl.BlockSpec((tm, d), lambda i: (i, 0)),
        out_shape=jax.ShapeDtypeStruct((n, d), F32),
        scratch_shapes=[pltpu.VMEM((2, TOP_K * tm, d), F32), pltpu.SemaphoreType.DMA((2,))],
        compiler_params=_params(("arbitrary",)),
        name="moe_combine",
    )(dest_tiles, dest_tiles, x1, wts, g, b, ys)


def _dispatch_tables(eid, blk):
    n = eid.shape[0]
    nk = n * TOP_K
    nb = -(-(nk + N_EXPERTS * (blk - 1)) // blk)
    flat_e = eid.reshape(-1)
    experts = jnp.arange(N_EXPERTS, dtype=jnp.int32)
    onehot = (flat_e[:, None] == experts[None, :]).astype(jnp.int32)
    ranks = jnp.cumsum(onehot, axis=0)
    counts = ranks[-1]
    rank = jnp.take_along_axis(ranks, flat_e[:, None], axis=1)[:, 0] - 1
    nblk_e = (counts + blk - 1) // blk
    blk_end = jnp.cumsum(nblk_e)
    blk_start = blk_end - nblk_e
    dest = blk_start[flat_e] * blk + rank
    n_used = blk_end[-1]
    tail = n_used + experts
    zero_blocks = jnp.concatenate([jnp.where(nblk_e > 0, blk_end - 1, -1), jnp.where(tail < nb, tail, -1)])
    bidx = jnp.arange(nb, dtype=jnp.int32)
    blk_e = jnp.minimum(jnp.searchsorted(blk_end, bidx, side='right'), N_EXPERTS - 1).astype(jnp.int32)
    nvalid = jnp.clip(counts[blk_e] - (bidx - blk_start[blk_e]) * blk, 0, blk).astype(jnp.int32)
    nvalid = jnp.where(bidx < n_used, nvalid, 0)
    xblk = jnp.minimum(bidx, n_used - 1).astype(jnp.int32)
    return dest.astype(jnp.int32), zero_blocks.astype(jnp.int32), blk_e, xblk, nvalid, nb * blk


def _pick(n, prefs):
    for p in prefs:
        if n % p == 0:
            return p
    return n


def _layer(x, past_k, past_v, past_logf, s0, shift_prev, wts, alpha):
    bsz, t, d = x.shape
    n = bsz * t
    xf = x.reshape(n, d)
    fox_width = wts['w_qkv'].shape[1] // 3
    n_heads = fox_width // FOX_HEAD_DIM
    rw_width = wts['w2p'].shape[1]
    rw_heads = rw_width // RWKV_HEAD_DIM
    rw_cols = wts['rwkv_cols']
    past = past_k.shape[1]

    xn = _entry_norm(xf, wts['ln_in_g'], wts['ln_in_b'], _pick(n, (512, 256)))
    tm = _pick(n, (PROJ_ROWS, 512, 256))
    gates = _proj_matmul(_gates_kernel, xn, wts['w_gates'], BF16, tm, 1024, "proj_gates")
    p_rw = _proj_matmul(_rwkv_proj_kernel, xn, wts['w_rwkv'], F32, tm, wts['w_rwkv'].shape[1] // 3, "proj_rwkv")
    t_fox = -(-t // LANES) * LANES
    tq = _pick(t_fox, (ATTN_BLOCK, 256, LANES))
    xn_fox = xn.reshape(bsz, t, d)
    if t_fox != t:
        xn_fox = jnp.pad(xn_fox, ((0, 0), (0, t_fox - t), (0, 0)))
    qt, k_f, k_b, v_f, vt, lf = _fox_proj(xn_fox, wts['w_qkv'], wts['w_f'], wts['b_f'], wts['q_norm'],
                                          wts['k_norm'], tq)
    k_f, v_f, logf = k_f[:, :t], v_f[:, :t], lf[:, :t, :n_heads]
    if past:
        tk = ATTN_BLOCK
        lpad = -(-(past + t) // tk) * tk
        grow = lambda a, ax: jnp.pad(a, [(0, lpad - a.shape[ax]) if i == ax else (0, 0) for i in range(a.ndim)])
        lf_past = jnp.pad(past_logf.astype(F32), ((0, 0), (0, 0), (0, LANES - n_heads)))
        lf_all = grow(jnp.concatenate([lf_past, lf[:, :t]], axis=1), 1)
        k_all = grow(jnp.concatenate([jnp.swapaxes(past_k, 1, 2).astype(BF16), k_b[:, :, :t]], axis=2), 2)
        vt_past = jnp.transpose(past_v, (0, 2, 3, 1)).astype(BF16)
        vt_all = grow(jnp.concatenate([vt_past, vt[:, :, 0, :, :t]], axis=3), 3)
        vt_all = jnp.swapaxes(vt_all.reshape(bsz, n_heads, FOX_HEAD_DIM, lpad // tk, tk), 2, 3)
    else:
        tk, lf_all, k_all, vt_all = tq, lf, k_b, vt
    nf = _fgate_bias(lf_all, past + t, n_heads, tk)
    o_a = _fox_attention(qt, k_all, nf, vt_all, past, tk)[:, :t]

    tp = -(-t // RWKV_CHUNK) * RWKV_CHUNK
    tc = _pick(tp, (256, 128, 64))
    p3 = p_rw.reshape(bsz, t, -1)
    shift_new = p3[:, t - 1:t, :rw_cols]
    if tp != t:
        p3 = jnp.pad(p3, ((0, 0), (0, tp - t), (0, 0)))
    shift_in = jnp.pad(shift_prev.astype(F32), ((0, 0), (0, 0), (0, p3.shape[-1] - rw_cols)))
    n_grp = rw_heads // HEADS_PER_GROUP
    eye = jnp.eye(HEADS_PER_GROUP, dtype=F32)
    s0_g = s0.astype(F32).reshape(bsz, n_grp, HEADS_PER_GROUP, RWKV_HEAD_DIM, RWKV_HEAD_DIM)
    s0_bd = jnp.einsum('bghvk,hj->bghvjk', s0_g, eye).reshape(bsz, n_grp, MXU_DIM, MXU_DIM)
    o_b, s_bd = _rwkv_mix(p3, shift_in, s0_bd, wts['rwkv_vecs'], wts['w2p'], wts['a2p'], wts['g2p'], tc, t)
    s_new = jnp.einsum('bghvjk,hj->bghvk',
                       s_bd.reshape(bsz, n_grp, HEADS_PER_GROUP, RWKV_HEAD_DIM, HEADS_PER_GROUP, RWKV_HEAD_DIM),
                       eye).reshape(bsz, rw_heads, RWKV_HEAD_DIM, RWKV_HEAD_DIM)
    o_b = o_b[:, :t].reshape(n, rw_width)

    x1, ids, rw = _merge_out(xf, o_a.reshape(n, fox_width), o_b, gates, wts['w_a'], wts['w_b'], wts['w_o'],
                             wts['ln_in_g'], wts['ln_in_b'], wts['ln1_g'], wts['ln1_b'],
                             wts['w_router'], wts['b_router'], alpha, _pick(n, (512, 256)))
    blk = _pick(n * TOP_K // N_EXPERTS, (512, 256, 128, 64, 32, 16, 8))
    dest, zero_blocks, blk_e, xblk, nvalid, n_rows = _dispatch_tables(ids[:, :TOP_K], blk)
    tm_d = _pick(n, (512, 256))
    tm_c = _pick(n, (256,))
    xs = _moe_dispatch(x1, zero_blocks, dest.reshape(-1, 1, TOP_K * tm_d), n_rows, blk, tm_d)
    ys = _moe_experts(xs, blk_e, xblk, nvalid, wts['moe_w_gate'], wts['moe_w_up'], wts['moe_w_down'], blk)
    y = _moe_combine(x1, ys, dest.reshape(-1, 1, TOP_K * tm_c), rw, wts['ln2_g'], wts['ln2_b'], alpha, tm_c)

    return y.reshape(bsz, t, d), (k_f, v_f, logf, s_new, shift_new)


def _prepare_weights(l, ln_in_g, ln_in_b, w_in, fox_b_f, fox_q_norm, fox_k_norm, rwkv_mu, rwkv_w0, rwkv_w2,
                     rwkv_a0, rwkv_a2, rwkv_g2, rwkv_k_k, rwkv_k_a, rwkv_r_k, rwkv_gn_g, rwkv_gn_b,
                     w_branch_a, w_branch_b, w_out, ln1_g, ln1_b, router_group_w, router_group_b,
                     router_expert_w, router_expert_b, moe_w_gate, moe_w_up, moe_w_down, ln2_g, ln2_b):
    d = w_in.shape[1]
    fox_width = w_branch_a.shape[1]
    rw_width = w_branch_b.shape[1]
    n_heads = fox_width // FOX_HEAD_DIM
    gate_cols = 2 * d
    fox_cols = 3 * fox_width + n_heads
    rw_cols = 3 * rw_width + RWKV_DECAY_RANK + RWKV_ICLR_RANK + RWKV_GATE_RANK
    row = lambda a: a.astype(F32).reshape(1, -1)
    w = w_in[l]
    w_fox = w[:, gate_cols:gate_cols + fox_cols]
    w_rw = w[:, gate_cols + fox_cols:]
    lora = RWKV_DECAY_RANK + RWKV_ICLR_RANK
    assert lora == LANES
    gate_pad = -(-RWKV_GATE_RANK // LANES) * LANES
    cols_pad = 3 * rw_width + lora + gate_pad
    pad_c = cols_pad - rw_cols
    zeros = lambda r: jnp.zeros((r, rw_width), F32)
    wr = jnp.concatenate([router_group_w[l], router_expert_w[l]], axis=1).astype(F32)
    wr = jnp.pad(wr, ((0, 0), (0, LANES - wr.shape[1])))
    rb = jnp.concatenate([router_group_b[l], router_expert_b[l]]).astype(F32)
    return {
        'ln_in_g': row(ln_in_g), 'ln_in_b': row(ln_in_b),
        'w_gates': w[:, :gate_cols].astype(BF16),
        'w_qkv': w_fox[:, :3 * fox_width].astype(BF16),
        'w_f': jnp.pad(w_fox[:, 3 * fox_width:], ((0, 0), (0, LANES - n_heads))).astype(BF16),
        'b_f': jnp.pad(row(fox_b_f[l]), ((0, 0), (0, LANES - n_heads))),
        'q_norm': row(fox_q_norm[l]), 'k_norm': row(fox_k_norm[l]),
        'w_rwkv': jnp.pad(w_rw, ((0, 0), (0, pad_c))).astype(BF16),
        'rwkv_cols': rw_cols,
        'rwkv_vecs': [jnp.pad(row(rwkv_mu[l]), ((0, 0), (0, pad_c))), row(rwkv_w0[l]), row(rwkv_a0[l]),
                      row(rwkv_k_k[l]), row(rwkv_k_a[l]), row(rwkv_r_k[l]), row(rwkv_gn_g[l]),
                      row(rwkv_gn_b[l])],
        'w2p': jnp.concatenate([rwkv_w2[l].astype(F32), zeros(RWKV_ICLR_RANK)]).astype(BF16),
        'a2p': jnp.concatenate([zeros(RWKV_DECAY_RANK), rwkv_a2[l].astype(F32)]).astype(BF16),
        'g2p': jnp.concatenate([rwkv_g2[l].astype(F32), zeros(gate_pad - RWKV_GATE_RANK)]).astype(BF16),
        'w_a': w_branch_a[l].astype(BF16), 'w_b': w_branch_b[l].astype(BF16), 'w_o': w_out[l].astype(BF16),
        'ln1_g': row(ln1_g[l]), 'ln1_b': row(ln1_b[l]),
        'w_router': jnp.concatenate(_split3(wr)[:2], axis=1),
        'b_router': jnp.pad(row(rb), ((0, 0), (0, LANES - rb.shape[0]))),
        'moe_w_gate': moe_w_gate[l], 'moe_w_up': moe_w_up[l], 'moe_w_down': moe_w_down[l],
        'ln2_g': row(ln2_g[l]), 'ln2_b': row(ln2_b[l]),
    }


def kernel(x_prompt, x_sample, cache_fox_k, cache_fox_v, cache_fox_logf, state_rwkv, state_rwkv_shift,
           ln_in_g, ln_in_b, w_in, fox_b_f, fox_q_norm, fox_k_norm, rwkv_mu, rwkv_w0, rwkv_w2,
           rwkv_a0, rwkv_a2, rwkv_g2, rwkv_k_k, rwkv_k_a, rwkv_r_k, rwkv_gn_g, rwkv_gn_b,
           w_branch_a, w_branch_b, w_out, ln1_g, ln1_b, router_group_w, router_group_b,
           router_expert_w, router_expert_b, moe_w_gate, moe_w_up, moe_w_down, ln2_g, ln2_b):
    depth = w_in.shape[0]
    assert depth == 1, "the entry LayerNorm is fused into the layer's projections: single-layer trunk only"
    alpha = (2.0 * depth) ** 0.25
    bp = x_prompt.shape[0]
    n_fox_heads = fox_b_f.shape[1]
    rw_heads, rw_dim = state_rwkv.shape[2], state_rwkv.shape[3]
    rw_cols = state_rwkv_shift.shape[-1]
    wts = _prepare_weights(0, ln_in_g, ln_in_b, w_in, fox_b_f, fox_q_norm, fox_k_norm, rwkv_mu, rwkv_w0,
                           rwkv_w2, rwkv_a0, rwkv_a2, rwkv_g2, rwkv_k_k, rwkv_k_a, rwkv_r_k, rwkv_gn_g,
                           rwkv_gn_b, w_branch_a, w_branch_b, w_out, ln1_g, ln1_b, router_group_w,
                           router_group_b, router_expert_w, router_expert_b, moe_w_gate, moe_w_up,
                           moe_w_down, ln2_g, ln2_b)
    xp, new_p = _layer(x_prompt, jnp.zeros((bp, 0, n_fox_heads, FOX_HEAD_DIM), F32),
                       jnp.zeros((bp, 0, n_fox_heads, FOX_HEAD_DIM), F32),
                       jnp.zeros((bp, 0, n_fox_heads), F32),
                       jnp.zeros((bp, rw_heads, rw_dim, rw_dim), F32),
                       jnp.zeros((bp, 1, rw_cols), F32), wts, alpha)
    xs, new_s = _layer(x_sample, cache_fox_k[0], cache_fox_v[0], cache_fox_logf[0], state_rwkv[0],
                       state_rwkv_shift[0], wts, alpha)
    return (xp, xs) + tuple(a[None] for a in new_p) + tuple(a[None] for a in new_s)
```

```python
import functools

import jax
import jax.numpy as jnp
from jax import lax
from jax.experimental import pallas as pl
from jax.experimental.pallas import tpu as pltpu

F32 = jnp.float32
BF16 = jnp.bfloat16

FOX_HEAD_DIM = 128
RWKV_HEAD_DIM = 64
RWKV_DECAY_RANK = 64
RWKV_ICLR_RANK = 64
RWKV_GATE_RANK = 160
RWKV_GN_EPS = 64e-5
N_GROUPS = 4
EXPERTS_PER_GROUP = 8
N_EXPERTS = N_GROUPS * EXPERTS_PER_GROUP
TOP_K = 2
LN_EPS = 1e-5
QK_EPS = 1e-6
NEG_INF = -1e30
LOG2_E = 1.4426950408889634
DECAY_SCALE = 0.6065306597126334

LANES = 128
MXU_DIM = 256
VMEM_LIMIT_BYTES = 56 * 1024 * 1024

PROJ_ROWS = 1024
ATTN_BLOCK = 512
RWKV_CHUNK = 64
HEADS_PER_GROUP = MXU_DIM // RWKV_HEAD_DIM


def _params(semantics):
    return pltpu.CompilerParams(dimension_semantics=semantics, vmem_limit_bytes=VMEM_LIMIT_BYTES)


def _dot(a, b):
    return jnp.dot(a.astype(BF16), b.astype(BF16), preferred_element_type=F32)


def _dot_nt(a, b):
    return lax.dot_general(a.astype(BF16), b.astype(BF16), (((1,), (1,)), ((), ())),
                           preferred_element_type=F32)


def _split3(x):
    h1 = x.astype(BF16)
    r1 = x - h1.astype(F32)
    h2 = r1.astype(BF16)
    h3 = (r1 - h2.astype(F32)).astype(BF16)
    return h1, h2, h3


def _dot_exact_rhs(x, m_bf16):
    h1, h2, h3 = _split3(x)
    d = lambda h: jnp.dot(h, m_bf16, preferred_element_type=F32)
    return d(h1) + d(h2) + d(h3)


def _dot_exact_lhs(m_bf16, x):
    h1, h2, h3 = _split3(x)
    d = lambda h: jnp.dot(m_bf16, h, preferred_element_type=F32)
    return d(h1) + d(h2) + d(h3)


def _layer_norm(x, g, b):
    mu = jnp.mean(x, axis=-1, keepdims=True)
    xc = x - mu
    var = jnp.mean(xc * xc, axis=-1, keepdims=True)
    return xc * lax.rsqrt(var + LN_EPS) * g + b


def _sigmoid(x):
    return 1.0 / (1.0 + jnp.exp(-x))


def _log_sigmoid(x):
    return jnp.minimum(x, 0.0) - jnp.log(1.0 + jnp.exp(-jnp.abs(x)))


def _ln_kernel(x_ref, g_ref, b_ref, o_ref):
    o_ref[...] = _layer_norm(x_ref[...], g_ref[...], b_ref[...]).astype(o_ref.dtype)


def _entry_norm(x, ln_g, ln_b, tm):
    n, d = x.shape
    return pl.pallas_call(
        _ln_kernel,
        grid=(n // tm,),
        in_specs=[pl.BlockSpec((tm, d), lambda i: (i, 0)),
                  pl.BlockSpec((1, d), lambda i: (0, 0)), pl.BlockSpec((1, d), lambda i: (0, 0))],
        out_specs=pl.BlockSpec((tm, d), lambda i: (i, 0)),
        out_shape=jax.ShapeDtypeStruct((n, d), BF16),
        compiler_params=_params(("parallel",)),
        name="entry_norm",
    )(x, ln_g, ln_b)


def _gates_kernel(x_ref, w_ref, o_ref):
    y = jnp.dot(x_ref[...], w_ref[...], preferred_element_type=F32)
    o_ref[...] = _sigmoid(y).astype(o_ref.dtype)


def _rwkv_proj_kernel(x_ref, w_ref, o_ref):
    o_ref[...] = jnp.dot(x_ref[...], w_ref[...], preferred_element_type=F32)


def _proj_matmul(body, xn, w, out_dtype, tm, tn, name):
    n, d = xn.shape
    ncol = w.shape[1]
    return pl.pallas_call(
        body,
        grid=(n // tm, ncol // tn),
        in_specs=[pl.BlockSpec((tm, d), lambda i, j: (i, 0)),
                  pl.BlockSpec((d, tn), lambda i, j: (0, j))],
        out_specs=pl.BlockSpec((tm, tn), lambda i, j: (i, j)),
        out_shape=jax.ShapeDtypeStruct((n, ncol), out_dtype),
        compiler_params=_params(("parallel", "parallel")),
        name=name,
    )(xn, w)


def _fox_proj_kernel(xn_ref, w_ref, wf_ref, bf_ref, qn_ref, kn_ref,
                     qt_ref, kf_ref, kb_ref, vf_ref, vt_ref, lf_ref, *, n_heads):
    j = pl.program_id(2)
    y = jnp.dot(xn_ref[...], w_ref[...], preferred_element_type=F32)

    def rms(yh, gain):
        ms = jnp.mean(yh * yh, axis=-1, keepdims=True)
        return yh * lax.rsqrt(ms + QK_EPS) * gain

    @pl.when(j == 0)
    def _():
        scale = FOX_HEAD_DIM ** -0.5 * LOG2_E
        for h in range(n_heads):
            yh = y[:, h * FOX_HEAD_DIM:(h + 1) * FOX_HEAD_DIM]
            qt_ref[h] = (rms(yh, qn_ref[...]) * scale).T.astype(BF16)
        fl = jnp.dot(xn_ref[...], wf_ref[...], preferred_element_type=F32)
        lf_ref[...] = _log_sigmoid(fl + bf_ref[...])

    @pl.when(j == 1)
    def _():
        for h in range(n_heads):
            sl = slice(h * FOX_HEAD_DIM, (h + 1) * FOX_HEAD_DIM)
            kh = rms(y[:, sl], kn_ref[...])
            kf_ref[:, sl] = kh
            kb_ref[h] = kh.astype(BF16)

    @pl.when(j == 2)
    def _():
        vf_ref[...] = y
        for h in range(n_heads):
            vt_ref[h] = y[:, h * FOX_HEAD_DIM:(h + 1) * FOX_HEAD_DIM].T.astype(BF16)


def _fox_proj(xn, w_qkv, w_f, b_f, q_norm, k_norm, tm):
    bsz, t, d = xn.shape
    width = w_qkv.shape[1] // 3
    n_heads = width // FOX_HEAD_DIM
    hm = pl.BlockSpec((None, n_heads, tm, FOX_HEAD_DIM), lambda b, i, j: (b, 0, i, 0))
    tr = lambda: pl.BlockSpec((None, n_heads, None, FOX_HEAD_DIM, tm), lambda b, i, j: (b, 0, i, 0, 0))
    tok = lambda c: pl.BlockSpec((None, tm, c), lambda b, i, j: (b, i, 0))
    const = lambda r, c: pl.BlockSpec((r, c), lambda b, i, j: (0, 0))
    tr_shape = jax.ShapeDtypeStruct((bsz, n_heads, t // tm, FOX_HEAD_DIM, tm), BF16)
    tok_shape = jax.ShapeDtypeStruct((bsz, t, width), F32)
    return pl.pallas_call(
        functools.partial(_fox_proj_kernel, n_heads=n_heads),
        grid=(bsz, t // tm, 3),
        in_specs=[
            pl.BlockSpec((None, tm, d), lambda b, i, j: (b, i, 0)),
            pl.BlockSpec((d, width), lambda b, i, j: (0, j)),
            const(d, LANES), const(1, LANES),
            const(1, FOX_HEAD_DIM), const(1, FOX_HEAD_DIM),
        ],
        out_specs=[tr(), tok(width), hm, tok(width), tr(), tok(LANES)],
        out_shape=[tr_shape, tok_shape, jax.ShapeDtypeStruct((bsz, n_heads, t, FOX_HEAD_DIM), BF16),
                   tok_shape, tr_shape, jax.ShapeDtypeStruct((bsz, t, LANES), F32)],
        compiler_params=_params(("parallel", "parallel", "arbitrary")),
        name="proj_fox",
    )(xn, w_qkv, w_f, b_f, q_norm, k_norm)


def _fgate_bias_kernel(lf_ref, o_ref, carry, *, length, n_heads):
    t = pl.program_id(1)
    tt = lf_ref.shape[0]

    @pl.when(t == 0)
    def _():
        carry[...] = jnp.zeros_like(carry)

    ri = lax.broadcasted_iota(jnp.int32, (tt, tt), 0)
    rj = lax.broadcasted_iota(jnp.int32, (tt, tt), 1)
    tri = jnp.where(rj <= ri, 1.0, 0.0).astype(BF16)
    csum = _dot_exact_lhs(tri, lf_ref[...]) + carry[...]
    carry[...] = csum[tt - 1:tt, :]
    pos = t * tt + lax.broadcasted_iota(jnp.int32, csum.shape, 0)
    neg = jnp.where(pos < length, -LOG2_E * csum, NEG_INF)
    pieces = jnp.concatenate(_split3(neg), axis=1)
    sr = lax.broadcasted_iota(jnp.int32, (3 * LANES, LANES), 0)
    sc = lax.broadcasted_iota(jnp.int32, (3 * LANES, LANES), 1)
    for h in range(n_heads):
        sel = jnp.where(sr % LANES == h, jnp.where(sr // LANES == sc, 1.0, 0.0), 0.0).astype(BF16)
        o_ref[h] = jnp.dot(pieces, sel, preferred_element_type=F32).astype(BF16)


def _fgate_bias(lf, length, n_heads, tt):
    bsz, lp, _ = lf.shape
    return pl.pallas_call(
        functools.partial(_fgate_bias_kernel, length=length, n_heads=n_heads),
        grid=(bsz, lp // tt),
        in_specs=[pl.BlockSpec((None, tt, LANES), lambda b, t: (b, t, 0))],
        out_specs=pl.BlockSpec((None, n_heads, tt, LANES), lambda b, t: (b, 0, t, 0)),
        out_shape=jax.ShapeDtypeStruct((bsz, n_heads, lp, LANES), BF16),
        scratch_shapes=[pltpu.VMEM((1, LANES), F32)],
        compiler_params=_params(("parallel", "arbitrary")),
        name="fgate_bias",
    )(lf)


N_BIAS_PIECES = 3


ATTN_HEADS_PER_STEP = 2


def _attn_kernel(qt_ref, k_ref, nf_ref, vt_ref, o_ref, *, tq, tk, past):
    i = pl.program_id(2)
    n_heads = qt_ref.shape[0]
    tw = min(tq, MXU_DIM)
    chains = [(h, c) for h in range(n_heads) for c in range(tq // tw)]
    ones_rows = jnp.where(lax.broadcasted_iota(jnp.int32, (LANES, tw), 0) < N_BIAS_PIECES, 1.0, 0.0)
    qa = [jnp.concatenate([qt_ref[h, :, c * tw:(c + 1) * tw], ones_rows.astype(BF16)], axis=0)
          for h, c in chains]
    n_full = (past + i * tq) // tk

    def update(carry, s, vt):
        m, l, acc = carry
        m_new = jnp.maximum(m, jnp.max(s, axis=0, keepdims=True))
        p = jnp.exp2(s - m_new)
        alpha = jnp.exp2(m - m_new)
        l = alpha * l + jnp.sum(p, axis=0, keepdims=True)
        acc = alpha * acc + jnp.dot(vt, p.astype(BF16), preferred_element_type=F32)
        return m_new, l, acc

    def scores(j, causal=False):
        start = pl.multiple_of(j * tk, tk)
        ka = [jnp.concatenate([k_ref[h, pl.ds(start, tk), :], nf_ref[h, pl.ds(start, tk), :]], axis=1)
              for h in range(n_heads)]
        ss = [jnp.dot(ka[h], qa[n], preferred_element_type=F32) for n, (h, c) in enumerate(chains)]
        if causal:
            key = lax.broadcasted_iota(jnp.int32, (tk, tw), 0)
            qry = lax.broadcasted_iota(jnp.int32, (tk, tw), 1)
            ss = [jnp.where(key <= qry + c * tw, s, NEG_INF) for s, (h, c) in zip(ss, chains)]
        return ss

    def absorb(carry, ss, j):
        return tuple(update(carry[n], ss[n], vt_ref[h, j]) for n, (h, c) in enumerate(chains))

    def two_blocks(pair, carry):
        sa, sb = scores(2 * pair), scores(2 * pair + 1)
        return absorb(absorb(carry, sa, 2 * pair), sb, 2 * pair + 1)

    init = (jnp.full((1, tw), NEG_INF, F32), jnp.zeros((1, tw), F32), jnp.zeros((FOX_HEAD_DIM, tw), F32))
    carry = lax.fori_loop(0, n_full // 2, two_blocks, tuple(init for _ in chains))
    last_full = jnp.maximum(n_full - 1, 0)
    carry = lax.cond(n_full % 2 == 1, lambda c: absorb(c, scores(last_full), last_full), lambda c: c, carry)
    carry = absorb(carry, scores(n_full, causal=True), n_full)
    for n, (h, c) in enumerate(chains):
        _, l, acc = carry[n]
        o_ref[c * tw:(c + 1) * tw, h * FOX_HEAD_DIM:(h + 1) * FOX_HEAD_DIM] = (acc / l).T.astype(o_ref.dtype)


def _fox_attention(qt, k, nf, vt, past, tk):
    bsz, n_heads, nq, dh, tq = qt.shape
    lp = k.shape[2]
    hps = ATTN_HEADS_PER_STEP
    assert past % tk == 0 and (tq == tk or nq == 1) and tq <= tk and lp % tk == 0 and n_heads % hps == 0
    whole = lambda a: pl.BlockSpec((None, hps) + a.shape[2:], lambda b, h, i: (b, h) + (0,) * (a.ndim - 2))
    return pl.pallas_call(
        functools.partial(_attn_kernel, tq=tq, tk=tk, past=past),
        grid=(bsz, n_heads // hps, nq),
        in_specs=[pl.BlockSpec((None, hps, None, dh, tq), lambda b, h, i: (b, h, i, 0, 0)),
                  whole(k), whole(nf), whole(vt)],
        out_specs=pl.BlockSpec((None, tq, hps * dh), lambda b, h, i: (b, i, h)),
        out_shape=jax.ShapeDtypeStruct((bsz, nq * tq, n_heads * dh), BF16),
        compiler_params=_params(("parallel", "parallel", "arbitrary")),
        name="fox_attn",
    )(qt, k, nf, vt)


def _rwkv_kernel(p_ref, shift_ref, s0_ref, mu_ref, w0_ref, a0_ref, kk_ref, ka_ref, rk_ref,
                 gng_ref, gnb_ref, w2_ref, a2_ref, g2_ref, o_ref, sout_ref,
                 state, prev_row, *, tc, t_valid):
    c_len = RWKV_CHUNK
    t = pl.program_id(1)
    width = o_ref.shape[-1]
    n_groups = width // MXU_DIM

    @pl.when(t == 0)
    def _():
        state[...] = s0_ref[...]
        prev_row[...] = shift_ref[...]

    p = p_ref[...]
    ridx = lax.broadcasted_iota(jnp.int32, p.shape, 0)
    prev = jnp.where(ridx == 0, jnp.broadcast_to(prev_row[...], p.shape), pltpu.roll(p, 1, 0))
    prev_row[...] = p[tc - 1:tc, :]
    xs = p + (prev - p) * mu_ref[...]
    r = xs[:, 0:width]
    k = xs[:, width:2 * width]
    v = xs[:, 2 * width:3 * width]
    lora_in = xs[:, 3 * width:3 * width + LANES]
    gate_in = xs[:, 3 * width + LANES:]
    zw = _dot(jnp.tanh(lora_in), w2_ref[...])
    za = _dot(lora_in, a2_ref[...])
    g = _dot(_sigmoid(gate_in), g2_ref[...])
    lw = -DECAY_SCALE * _sigmoid(w0_ref[...] + zw)
    iclr = _sigmoid(a0_ref[...] + za)

    hr = lax.broadcasted_iota(jnp.int32, (MXU_DIM, MXU_DIM), 0) // RWKV_HEAD_DIM
    hc = lax.broadcasted_iota(jnp.int32, (MXU_DIM, MXU_DIM), 1) // RWKV_HEAD_DIM
    same_head = hr == hc
    bd_f32 = jnp.where(same_head, 1.0, 0.0)
    ones_bd = bd_f32.astype(BF16)

    def head_sum(x, split=True):
        hi = x.astype(BF16)
        lo = (x - hi.astype(F32)).astype(BF16) if split else None
        parts = []
        for gi in range(n_groups):
            ls = slice(gi * MXU_DIM, (gi + 1) * MXU_DIM)
            part = jnp.dot(hi[:, ls], ones_bd, preferred_element_type=F32)
            if split:
                part = part + jnp.dot(lo[:, ls], ones_bd, preferred_element_type=F32)
            parts.append(part)
        return jnp.concatenate(parts, axis=-1)

    kk = k * kk_ref[...]
    kk = kk * lax.rsqrt(jnp.maximum(head_sum(kk * kk), 1e-24))
    k = k * (1.0 + (iclr - 1.0) * ka_ref[...])
    if t_valid < tc:
        live = lax.broadcasted_iota(jnp.int32, (tc, width), 0) < t_valid
        lw = jnp.where(live, lw, 0.0)
        kk = jnp.where(live, kk, 0.0)
        k = jnp.where(live, k, 0.0)
        v = jnp.where(live, v, 0.0)

    ti = lax.broadcasted_iota(jnp.int32, (tc, tc), 0)
    tj = lax.broadcasted_iota(jnp.int32, (tc, tc), 1)
    tri = jnp.where(ti // c_len == tj // c_len, jnp.where(tj <= ti, 1.0, 0.0), 0.0).astype(BF16)
    gcum = _dot_exact_lhs(tri, lw)
    e_in = jnp.exp(gcum)
    e_inv = jnp.exp(-gcum)
    at_all = -kk * jnp.exp(gcum - lw)
    rt_all = r * e_in
    bt_all = kk * iclr * e_inv
    kt_all = k * e_inv
    bonus = head_sum(r * k * rk_ref[...]) * v

    row = lax.broadcasted_iota(jnp.int32, (c_len, MXU_DIM), 0)
    lane = lax.broadcasted_iota(jnp.int32, (c_len, MXU_DIM), 1) % c_len
    strict = jnp.where(lane < row, 1.0, 0.0)
    incl = jnp.where(lane <= row, 1.0, 0.0)
    eye_w = jnp.where(lane == row, 1.0, 0.0)

    def bd(x):
        return jnp.concatenate([x.astype(BF16)] * HEADS_PER_GROUP, axis=0) * ones_bd

    n_sq = c_len.bit_length() - 1
    n_chunks = tc // c_len
    units = [(ci, gi) for ci in range(n_chunks) for gi in range(n_groups)]

    def cut(x, u):
        ci, gi = u
        return x[ci * c_len:(ci + 1) * c_len, gi * MXU_DIM:(gi + 1) * MXU_DIM]

    at = [cut(at_all, u) for u in units]
    rt = [cut(rt_all, u) for u in units]
    bt = [cut(bt_all, u) for u in units]
    kt = [cut(kt_all, u) for u in units]
    vv = [cut(v, u) for u in units]
    ar = [jnp.concatenate([a, r_], axis=0) for a, r_ in zip(at, rt)]
    ab = [_dot_nt(x, bd(b_)) for x, b_ in zip(ar, bt)]
    ak = [_dot_nt(x, bd(k_)) for x, k_ in zip(ar, kt)]
    pw = [x[:c_len] * strict for x in ab]
    a_rb = [x[c_len:] * incl for x in ab]
    a_ak = [x[:c_len] * strict for x in ak]
    a_rk = [x[c_len:] * incl for x in ak]
    tm = [eye_w + x for x in pw]
    pw = [_dot(x, bd(x)) for x in pw]
    for js in range(1, n_sq):
        if js < n_sq - 1:
            tp = [_dot(jnp.concatenate([t_, x], axis=0), bd(x)) for t_, x in zip(tm, pw)]
            tm = [t_ + y_[:c_len] for t_, y_ in zip(tm, tp)]
            pw = [y_[c_len:] for y_ in tp]
        else:
            tm = [t_ + _dot(t_, bd(x)) for t_, x in zip(tm, pw)]
    bdv = [bd(x) for x in vv]
    a_hat = [_dot(t_, bd(a)) for t_, a in zip(tm, at)]
    av = [_dot(x, b_) for x, b_ in zip(a_ak, bdv)]
    u_hat = [_dot(t_, bd(x)) for t_, x in zip(tm, av)]
    r_hat = [r_ + _dot(x, bd(a)) for r_, x, a in zip(rt, a_rb, a_hat)]
    y_hat = [_dot(x, bd(uh)) + _dot(z, b_) for x, uh, z, b_ in zip(a_rb, u_hat, a_rk, bdv)]
    lhs = [jnp.concatenate([a, r_], axis=0) for a, r_ in zip(a_hat, r_hat)]

    st = [state[gi] for gi in range(n_groups)]
    y_rows = []
    for ci in range(n_chunks):
        gend = jnp.exp(gcum[(ci + 1) * c_len - 1:(ci + 1) * c_len, :])
        us = [ci * n_groups + gi for gi in range(n_groups)]
        ge = [gend[:, gi * MXU_DIM:(gi + 1) * MXU_DIM] for gi in range(n_groups)]
        uy = [_dot_nt(lhs[u], st[gi]) for gi, u in enumerate(us)]
        uu = [uy[gi][:c_len] + u_hat[u] for gi, u in enumerate(us)]
        y_rows.append(jnp.concatenate([uy[gi][c_len:] + y_hat[u] for gi, u in enumerate(us)], axis=1))
        uv_t = [jnp.concatenate([uu[gi], vv[u]], axis=0).T for gi, u in enumerate(us)]
        bk = [jnp.concatenate([bt[u] * ge[gi], kt[u] * ge[gi]], axis=0) for gi, u in enumerate(us)]
        st = [st[gi] * ge[gi] + _dot(uv_t[gi], bk[gi]) * bd_f32 for gi in range(n_groups)]
    for gi in range(n_groups):
        state[gi] = st[gi]

    y = jnp.concatenate(y_rows, axis=0)
    inv_n = 1.0 / RWKV_HEAD_DIM
    mean = head_sum(y, split=False) * inv_n
    yc = y - mean
    var = head_sum(yc * yc, split=False) * inv_n
    yn = yc * lax.rsqrt(var + RWKV_GN_EPS) * gng_ref[...] + gnb_ref[...]
    o_ref[...] = ((yn + bonus) * g).astype(o_ref.dtype)

    @pl.when(t == pl.num_programs(1) - 1)
    def _():
        sout_ref[...] = state[...]


def _rwkv_mix(p, shift_prev, s0_bd, vecs, w2p, a2p, g2p, tc, t_valid):
    bsz, tp, cols = p.shape
    width = w2p.shape[1]
    n_groups = width // MXU_DIM
    vec_specs = [pl.BlockSpec((1, a.shape[1]), lambda b, t: (0, 0)) for a in vecs]
    mat = lambda a: pl.BlockSpec(a.shape, lambda b, t: (0, 0))
    st_spec = pl.BlockSpec((None, n_groups, MXU_DIM, MXU_DIM), lambda b, t: (b, 0, 0, 0))
    return pl.pallas_call(
        functools.partial(_rwkv_kernel, tc=tc, t_valid=t_valid),
        grid=(bsz, tp // tc),
        in_specs=[pl.BlockSpec((None, tc, cols), lambda b, t: (b, t, 0)),
                  pl.BlockSpec((None, 1, cols), lambda b, t: (b, 0, 0)),
                  st_spec] + vec_specs + [mat(w2p), mat(a2p), mat(g2p)],
        out_specs=[pl.BlockSpec((None, tc, width), lambda b, t: (b, t, 0)), st_spec],
        out_shape=[jax.ShapeDtypeStruct((bsz, tp, width), BF16),
                   jax.ShapeDtypeStruct(s0_bd.shape, F32)],
        scratch_shapes=[pltpu.VMEM((n_groups, MXU_DIM, MXU_DIM), F32), pltpu.VMEM((1, cols), F32)],
        compiler_params=_params(("parallel", "arbitrary")),
        name="rwkv_mix",
    )(p, shift_prev, s0_bd, *vecs, w2p, a2p, g2p)


def _merge_kernel(x_ref, oa_ref, ob_ref, gt_ref, wa_ref, wb_ref, wo_ref, lig_ref, lib_ref,
                  l1g_ref, l1b_ref, wr_ref, rb_ref, x1_ref, ids_ref, wts_ref, *, alpha):
    d = x_ref.shape[-1]
    xn = _layer_norm(x_ref[...], lig_ref[...], lib_ref[...])
    ya = jnp.dot(oa_ref[...], wa_ref[...], preferred_element_type=F32)
    yb = jnp.dot(ob_ref[...], wb_ref[...], preferred_element_type=F32)
    merged = gt_ref[:, :d].astype(F32) * ya + gt_ref[:, d:].astype(F32) * yb
    out = jnp.dot(merged.astype(BF16), wo_ref[...], preferred_element_type=F32)
    x1 = _layer_norm(alpha * xn + out, l1g_ref[...], l1b_ref[...])
    x1_ref[...] = x1

    h1, h2, _ = _split3(x1)
    r1 = jnp.dot(h1, wr_ref[...], preferred_element_type=F32)
    r2 = jnp.dot(h2, wr_ref[...], preferred_element_type=F32)
    logits = (r1[:, :LANES] + (r1[:, LANES:] + r2[:, :LANES]) + r2[:, LANES:]) + rb_ref[...]
    lane = lax.broadcasted_iota(jnp.int32, logits.shape, 1).astype(F32)
    big = 1e9

    def first_max(vals):
        mx = jnp.max(vals, axis=-1, keepdims=True)
        idx = jnp.min(jnp.where(vals == mx, lane, big), axis=-1, keepdims=True)
        return mx, idx

    is_grp = lane < N_GROUPS
    gmax, grp = first_max(jnp.where(is_grp, logits, NEG_INF))
    p_grp = 1.0 / jnp.sum(jnp.where(is_grp, jnp.exp(logits - gmax), 0.0), axis=-1, keepdims=True)
    lo = N_GROUPS + grp * EXPERTS_PER_GROUP
    elog = jnp.where(lane >= lo, jnp.where(lane < lo + EXPERTS_PER_GROUP, logits, NEG_INF), NEG_INF)
    v1, i1 = first_max(elog)
    v2, i2 = first_max(jnp.where(lane == i1, NEG_INF, elog))
    e2 = jnp.exp(v2 - v1)
    w1 = p_grp / (1.0 + e2)
    w2 = p_grp * e2 / (1.0 + e2)
    ids = jnp.where(lane == 0, i1 - N_GROUPS, jnp.where(lane == 1, i2 - N_GROUPS, 0.0))
    ids_ref[...] = ids.astype(jnp.int32)
    wts_ref[...] = jnp.where(lane == 0, w1, jnp.where(lane == 1, w2, 0.0))


def _merge_out(x, oa, ob, gates, wa, wb, wo, lig, lib, l1g, l1b, w_router, rbias, alpha, tm):
    n, d = x.shape
    half = oa.shape[1]
    row = lambda c: pl.BlockSpec((tm, c), lambda i: (i, 0))
    const = lambda a: pl.BlockSpec(a.shape, lambda i: (0, 0), pipeline_mode=pl.Buffered(1))
    return pl.pallas_call(
        functools.partial(_merge_kernel, alpha=alpha),
        grid=(n // tm,),
        in_specs=[row(d), row(half), row(half), row(2 * d), const(wa), const(wb), const(wo),
                  const(lig), const(lib), const(l1g), const(l1b), const(w_router), const(rbias)],
        out_specs=[row(d), row(LANES), row(LANES)],
        out_shape=[jax.ShapeDtypeStruct((n, d), F32), jax.ShapeDtypeStruct((n, LANES), jnp.int32),
                   jax.ShapeDtypeStruct((n, LANES), F32)],
        compiler_params=_params(("parallel",)),
        name="merge_out",
    )(x, oa, ob, gates, wa, wb, wo, lig, lib, l1g, l1b, w_router, rbias)


DMA_ISSUE_UNROLL = 8


def _wait_rows(make_row_copy, count):
    lax.fori_loop(0, count, lambda q, c: (make_row_copy(0).wait(), c)[1], 0, unroll=DMA_ISSUE_UNROLL)


def _dispatch_kernel(zblk_ref, dest_ref, x_ref, xs_hbm, zeros, sem, zsem):
    i = pl.program_id(0)
    tm = x_ref.shape[0]
    blk = zeros.shape[0]

    def zero_block(z):
        start = pl.multiple_of(zblk_ref[z] * blk, blk)
        return pltpu.make_async_copy(zeros, xs_hbm.at[pl.ds(start, blk), :], zsem)

    @pl.when(i == 0)
    def _():
        zeros[...] = jnp.zeros_like(zeros)
        for z in range(zblk_ref.shape[0]):
            pl.when(zblk_ref[z] >= 0)(lambda z=z: zero_block(z).start())
        for z in range(zblk_ref.shape[0]):
            pl.when(zblk_ref[z] >= 0)(lambda z=z: zero_block(z).wait())

    def to_slot(r, kslot):
        return pltpu.make_async_copy(x_ref.at[pl.ds(r, 1), :],
                                     xs_hbm.at[pl.ds(dest_ref[0, TOP_K * r + kslot], 1), :], sem)

    def body(r, c):
        for kslot in range(TOP_K):
            to_slot(r, kslot).start()
        return c
    lax.fori_loop(0, tm, body, 0, unroll=DMA_ISSUE_UNROLL)
    _wait_rows(lambda q: to_slot(0, 0), TOP_K * tm)


def _moe_dispatch(x1, zero_blocks, dest_tiles, n_rows, blk, tm):
    n, d = x1.shape
    grid_spec = pltpu.PrefetchScalarGridSpec(
        num_scalar_prefetch=1,
        grid=(n // tm,),
        in_specs=[pl.BlockSpec((None, 1, TOP_K * tm), lambda i, zb: (i, 0, 0), memory_space=pltpu.SMEM),
                  pl.BlockSpec((tm, d), lambda i, zb: (i, 0))],
        out_specs=pl.BlockSpec(memory_space=pl.ANY),
        scratch_shapes=[pltpu.VMEM((blk, d), F32), pltpu.SemaphoreType.DMA(()), pltpu.SemaphoreType.DMA(())],
    )
    return pl.pallas_call(
        _dispatch_kernel,
        grid_spec=grid_spec,
        out_shape=jax.ShapeDtypeStruct((n_rows, d), F32),
        compiler_params=_params(("arbitrary",)),
        name="moe_dispatch",
    )(zero_blocks, dest_tiles, x1)


def _experts_kernel(blk_e_ref, xblk_ref, nvalid_ref, xs_ref, wg_ref, wu_ref, wd_ref, ys_ref):
    j = pl.program_id(0)

    @pl.when(nvalid_ref[j] > 0)
    def _():
        xv = xs_ref[...].astype(BF16)
        hg = jnp.dot(xv, wg_ref[...].astype(BF16), preferred_element_type=F32)
        hu = jnp.dot(xv, wu_ref[...].astype(BF16), preferred_element_type=F32)
        h = hg * _sigmoid(hg) * hu
        ys_ref[...] = jnp.dot(h.astype(BF16), wd_ref[...].astype(BF16), preferred_element_type=F32)

    @pl.when(nvalid_ref[j] == 0)
    def _():
        ys_ref[...] = jnp.zeros_like(ys_ref)


def _moe_experts(xs, blk_e, xblk, nvalid, w_gate, w_up, w_down, blk):
    d = xs.shape[1]
    nb = blk_e.shape[0]
    de = w_gate.shape[-1]
    grid_spec = pltpu.PrefetchScalarGridSpec(
        num_scalar_prefetch=3,
        grid=(nb,),
        in_specs=[
            pl.BlockSpec((blk, d), lambda j, be, xb, nv: (xb[j], 0)),
            pl.BlockSpec((None, d, de), lambda j, be, xb, nv: (be[j], 0, 0)),
            pl.BlockSpec((None, d, de), lambda j, be, xb, nv: (be[j], 0, 0)),
            pl.BlockSpec((None, de, d), lambda j, be, xb, nv: (be[j], 0, 0)),
        ],
        out_specs=pl.BlockSpec((blk, d), lambda j, be, xb, nv: (j, 0)),
    )
    return pl.pallas_call(
        _experts_kernel,
        grid_spec=grid_spec,
        out_shape=jax.ShapeDtypeStruct((nb * blk, d), F32),
        compiler_params=_params(("arbitrary",)),
        name="moe_experts",
    )(blk_e, xblk, nvalid, xs, w_gate, w_up, w_down)


def _combine_kernel(cur_ref, nxt_ref, x1_ref, wts_ref, g_ref, b_ref, ys_hbm, o_ref, ybuf, sem, *, alpha):
    i = pl.program_id(0)
    tm = x1_ref.shape[0]
    slot = i % 2

    def row(tbl_ref, s, r, kslot):
        return pltpu.make_async_copy(ys_hbm.at[pl.ds(tbl_ref[0, TOP_K * r + kslot], 1), :],
                                     ybuf.at[s, pl.ds(kslot * tm + r, 1), :], sem.at[s])

    def fetch(tbl_ref, s):
        def body(r, c):
            for kslot in range(TOP_K):
                row(tbl_ref, s, r, kslot).start()
            return c
        lax.fori_loop(0, tm, body, 0, unroll=DMA_ISSUE_UNROLL)

    @pl.when(i == 0)
    def _():
        fetch(cur_ref, 0)

    @pl.when(i + 1 < pl.num_programs(0))
    def _():
        fetch(nxt_ref, 1 - slot)

    _wait_rows(lambda q: row(cur_ref, slot, 0, 0), TOP_K * tm)
    w = wts_ref[...]
    y = w[:, 0:1] * ybuf[slot, 0:tm, :] + w[:, 1:2] * ybuf[slot, tm:2 * tm, :]
    o_ref[...] = _layer_norm(alpha * x1_ref[...] + y, g_ref[...], b_ref[...])


def _moe_combine(x1, ys, dest_tiles, wts, g, b, alpha, tm):
    n, d = x1.shape
    last = n // tm - 1
    tbl = lambda f: pl.BlockSpec((None, 1, TOP_K * tm), lambda i: (f(i), 0, 0), memory_space=pltpu.SMEM)
    return pl.pallas_call(
        functools.partial(_combine_kernel, alpha=alpha),
        grid=(n // tm,),
        in_specs=[tbl(lambda i: i), tbl(lambda i: jnp.minimum(i + 1, last)),
                  pl.BlockSpec((tm, d), lambda i: (i, 0)),
                  pl.BlockSpec((tm, LANES), lambda i: (i, 0)),
                  pl.BlockSpec((1, d), lambda i: (0, 0)), pl.BlockSpec((1, d), lambda i: (0, 0)),
                  pl.BlockSpec(memory_space=pl.ANY)],
        out_specs=pl.BlockSpec((tm, d), lambda i: (i, 0)),
        out_shape=jax.ShapeDtypeStruct((n, d), F32),
        scratch_shapes=[pltpu.VMEM((2, TOP_K * tm, d), F32), pltpu.SemaphoreType.DMA((2,))],
        compiler_params=_params(("arbitrary",)),
        name="moe_combine",
    )(dest_tiles, dest_tiles, x1, wts, g, b, ys)


def _dispatch_tables(eid, blk):
    n = eid.shape[0]
    nk = n * TOP_K
    nb = -(-(nk + N_EXPERTS * (blk - 1)) // blk)
    experts = jnp.arange(N_EXPERTS, dtype=jnp.int32)
    chunk = LANES
    e2 = eid.reshape(nk // chunk, chunk)
    onehot = (e2[:, :, None] == experts).astype(jnp.int32)
    chunk_counts = onehot.sum(axis=1)
    chunk_base = jnp.cumsum(chunk_counts, axis=0) - chunk_counts
    earlier = jnp.arange(chunk)[None, :] < jnp.arange(chunk)[:, None]
    in_chunk = ((e2[:, :, None] == e2[:, None, :]) & earlier[None]).sum(axis=2)
    counts = chunk_counts.sum(axis=0)
    nblk_e = (counts + blk - 1) // blk
    blk_end = jnp.cumsum(nblk_e)
    blk_start = blk_end - nblk_e
    base = chunk_base + (blk_start * blk)[None, :]
    dest = ((onehot * base[:, None, :]).sum(axis=2) + in_chunk).reshape(-1)
    n_used = blk_end[-1]
    tail = n_used + experts
    zero_blocks = jnp.concatenate([jnp.where(nblk_e > 0, blk_end - 1, -1), jnp.where(tail < nb, tail, -1)])
    bidx = jnp.arange(nb, dtype=jnp.int32)
    blk_e = jnp.minimum(jnp.searchsorted(blk_end, bidx, side='right'), N_EXPERTS - 1).astype(jnp.int32)
    nvalid = jnp.clip(counts[blk_e] - (bidx - blk_start[blk_e]) * blk, 0, blk).astype(jnp.int32)
    nvalid = jnp.where(bidx < n_used, nvalid, 0)
    xblk = jnp.minimum(bidx, n_used - 1).astype(jnp.int32)
    return dest.astype(jnp.int32), zero_blocks.astype(jnp.int32), blk_e, xblk, nvalid, nb * blk


def _pick(n, prefs):
    for p in prefs:
        if n % p == 0:
            return p
    return n


def _layer(x, past_k, past_v, past_logf, s0, shift_prev, wts, alpha):
    bsz, t, d = x.shape
    n = bsz * t
    xf = x.reshape(n, d)
    fox_width = wts['w_qkv'].shape[1] // 3
    n_heads = fox_width // FOX_HEAD_DIM
    rw_width = wts['w2p'].shape[1]
    rw_heads = rw_width // RWKV_HEAD_DIM
    rw_cols = wts['rwkv_cols']
    past = past_k.shape[1]

    xn = _entry_norm(xf, wts['ln_in_g'], wts['ln_in_b'], _pick(n, (512, 256)))
    tm = _pick(n, (PROJ_ROWS, 512, 256))
    gates = _proj_matmul(_gates_kernel, xn, wts['w_gates'], BF16, tm, 1024, "proj_gates")
    p_rw = _proj_matmul(_rwkv_proj_kernel, xn, wts['w_rwkv'], F32, tm, wts['w_rwkv'].shape[1] // 3, "proj_rwkv")
    t_fox = -(-t // LANES) * LANES
    tq = _pick(t_fox, (ATTN_BLOCK, 256, LANES))
    xn_fox = xn.reshape(bsz, t, d)
    if t_fox != t:
        xn_fox = jnp.pad(xn_fox, ((0, 0), (0, t_fox - t), (0, 0)))
    qt, k_f, k_b, v_f, vt, lf = _fox_proj(xn_fox, wts['w_qkv'], wts['w_f'], wts['b_f'], wts['q_norm'],
                                          wts['k_norm'], tq)
    k_f, v_f, logf = k_f[:, :t], v_f[:, :t], lf[:, :t, :n_heads]
    if past:
        tk = ATTN_BLOCK
        lpad = -(-(past + t) // tk) * tk
        grow = lambda a, ax: jnp.pad(a, [(0, lpad - a.shape[ax]) if i == ax else (0, 0) for i in range(a.ndim)])
        lf_past = jnp.pad(past_logf.astype(F32), ((0, 0), (0, 0), (0, LANES - n_heads)))
        lf_all = grow(jnp.concatenate([lf_past, lf[:, :t]], axis=1), 1)
        k_all = grow(jnp.concatenate([jnp.swapaxes(past_k, 1, 2).astype(BF16), k_b[:, :, :t]], axis=2), 2)
        vt_past = jnp.transpose(past_v, (0, 2, 3, 1)).astype(BF16)
        vt_all = grow(jnp.concatenate([vt_past, vt[:, :, 0, :, :t]], axis=3), 3)
        vt_all = jnp.swapaxes(vt_all.reshape(bsz, n_heads, FOX_HEAD_DIM, lpad // tk, tk), 2, 3)
    else:
        tk, lf_all, k_all, vt_all = tq, lf, k_b, vt
    nf = _fgate_bias(lf_all, past + t, n_heads, tk)
    o_a = _fox_attention(qt, k_all, nf, vt_all, past, tk)[:, :t]

    tp = -(-t // RWKV_CHUNK) * RWKV_CHUNK
    tc = _pick(tp, (256, 128, 64))
    p3 = p_rw.reshape(bsz, t, -1)
    shift_new = p3[:, t - 1:t, :rw_cols]
    if tp != t:
        p3 = jnp.pad(p3, ((0, 0), (0, tp - t), (0, 0)))
    shift_in = jnp.pad(shift_prev.astype(F32), ((0, 0), (0, 0), (0, p3.shape[-1] - rw_cols)))
    n_grp = rw_heads // HEADS_PER_GROUP
    eye = jnp.eye(HEADS_PER_GROUP, dtype=F32)
    s0_g = s0.astype(F32).reshape(bsz, n_grp, HEADS_PER_GROUP, RWKV_HEAD_DIM, RWKV_HEAD_DIM)
    s0_bd = jnp.einsum('bghvk,hj->bghvjk', s0_g, eye).reshape(bsz, n_grp, MXU_DIM, MXU_DIM)
    o_b, s_bd = _rwkv_mix(p3, shift_in, s0_bd, wts['rwkv_vecs'], wts['w2p'], wts['a2p'], wts['g2p'], tc, t)
    s_new = jnp.einsum('bghvjk,hj->bghvk',
                       s_bd.reshape(bsz, n_grp, HEADS_PER_GROUP, RWKV_HEAD_DIM, HEADS_PER_GROUP, RWKV_HEAD_DIM),
                       eye).reshape(bsz, rw_heads, RWKV_HEAD_DIM, RWKV_HEAD_DIM)
    o_b = o_b[:, :t].reshape(n, rw_width)

    x1, ids, rw = _merge_out(xf, o_a.reshape(n, fox_width), o_b, gates, wts['w_a'], wts['w_b'], wts['w_o'],
                             wts['ln_in_g'], wts['ln_in_b'], wts['ln1_g'], wts['ln1_b'],
                             wts['w_router'], wts['b_router'], alpha, _pick(n, (512, 256)))
    blk = _pick(n * TOP_K // N_EXPERTS, (512, 256, 128, 64, 32, 16, 8))
    dest, zero_blocks, blk_e, xblk, nvalid, n_rows = _dispatch_tables(ids[:, :TOP_K], blk)
    tm_d = _pick(n, (512, 256))
    tm_c = _pick(n, (256,))
    xs = _moe_dispatch(x1, zero_blocks, dest.reshape(-1, 1, TOP_K * tm_d), n_rows, blk, tm_d)
    ys = _moe_experts(xs, blk_e, xblk, nvalid, wts['moe_w_gate'], wts['moe_w_up'], wts['moe_w_down'], blk)
    y = _moe_combine(x1, ys, dest.reshape(-1, 1, TOP_K * tm_c), rw, wts['ln2_g'], wts['ln2_b'], alpha, tm_c)

    hd = (bsz, t, n_heads, FOX_HEAD_DIM)
    return y.reshape(bsz, t, d), (k_f.reshape(hd), v_f.reshape(hd), logf, s_new, shift_new)


def _prepare_weights(l, ln_in_g, ln_in_b, w_in, fox_b_f, fox_q_norm, fox_k_norm, rwkv_mu, rwkv_w0, rwkv_w2,
                     rwkv_a0, rwkv_a2, rwkv_g2, rwkv_k_k, rwkv_k_a, rwkv_r_k, rwkv_gn_g, rwkv_gn_b,
                     w_branch_a, w_branch_b, w_out, ln1_g, ln1_b, router_group_w, router_group_b,
                     router_expert_w, router_expert_b, moe_w_gate, moe_w_up, moe_w_down, ln2_g, ln2_b):
    d = w_in.shape[1]
    fox_width = w_branch_a.shape[1]
    rw_width = w_branch_b.shape[1]
    n_heads = fox_width // FOX_HEAD_DIM
    gate_cols = 2 * d
    fox_cols = 3 * fox_width + n_heads
    rw_cols = 3 * rw_width + RWKV_DECAY_RANK + RWKV_ICLR_RANK + RWKV_GATE_RANK
    row = lambda a: a.astype(F32).reshape(1, -1)
    w = w_in[l]
    w_fox = w[:, gate_cols:gate_cols + fox_cols]
    w_rw = w[:, gate_cols + fox_cols:]
    lora = RWKV_DECAY_RANK + RWKV_ICLR_RANK
    assert lora == LANES
    gate_pad = -(-RWKV_GATE_RANK // LANES) * LANES
    cols_pad = 3 * rw_width + lora + gate_pad
    pad_c = cols_pad - rw_cols
    zeros = lambda r: jnp.zeros((r, rw_width), F32)
    wr = jnp.concatenate([router_group_w[l], router_expert_w[l]], axis=1).astype(F32)
    wr = jnp.pad(wr, ((0, 0), (0, LANES - wr.shape[1])))
    rb = jnp.concatenate([router_group_b[l], router_expert_b[l]]).astype(F32)
    return {
        'ln_in_g': row(ln_in_g), 'ln_in_b': row(ln_in_b),
        'w_gates': w[:, :gate_cols].astype(BF16),
        'w_qkv': w_fox[:, :3 * fox_width].astype(BF16),
        'w_f': jnp.pad(w_fox[:, 3 * fox_width:], ((0, 0), (0, LANES - n_heads))).astype(BF16),
        'b_f': jnp.pad(row(fox_b_f[l]), ((0, 0), (0, LANES - n_heads))),
        'q_norm': row(fox_q_norm[l]), 'k_norm': row(fox_k_norm[l]),
        'w_rwkv': jnp.pad(w_rw, ((0, 0), (0, pad_c))).astype(BF16),
        'rwkv_cols': rw_cols,
        'rwkv_vecs': [jnp.pad(row(rwkv_mu[l]), ((0, 0), (0, pad_c))), row(rwkv_w0[l]), row(rwkv_a0[l]),
                      row(rwkv_k_k[l]), row(rwkv_k_a[l]), row(rwkv_r_k[l]), row(rwkv_gn_g[l]),
                      row(rwkv_gn_b[l])],
        'w2p': jnp.concatenate([rwkv_w2[l].astype(F32), zeros(RWKV_ICLR_RANK)]).astype(BF16),
        'a2p': jnp.concatenate([zeros(RWKV_DECAY_RANK), rwkv_a2[l].astype(F32)]).astype(BF16),
        'g2p': jnp.concatenate([rwkv_g2[l].astype(F32), zeros(gate_pad - RWKV_GATE_RANK)]).astype(BF16),
        'w_a': w_branch_a[l].astype(BF16), 'w_b': w_branch_b[l].astype(BF16), 'w_o': w_out[l].astype(BF16),
        'ln1_g': row(ln1_g[l]), 'ln1_b': row(ln1_b[l]),
        'w_router': jnp.concatenate(_split3(wr)[:2], axis=1),
        'b_router': jnp.pad(row(rb), ((0, 0), (0, LANES - rb.shape[0]))),
        'moe_w_gate': moe_w_gate[l], 'moe_w_up': moe_w_up[l], 'moe_w_down': moe_w_down[l],
        'ln2_g': row(ln2_g[l]), 'ln2_b': row(ln2_b[l]),
    }


def kernel(x_prompt, x_sample, cache_fox_k, cache_fox_v, cache_fox_logf, state_rwkv, state_rwkv_shift,
           ln_in_g, ln_in_b, w_in, fox_b_f, fox_q_norm, fox_k_norm, rwkv_mu, rwkv_w0, rwkv_w2,
           rwkv_a0, rwkv_a2, rwkv_g2, rwkv_k_k, rwkv_k_a, rwkv_r_k, rwkv_gn_g, rwkv_gn_b,
           w_branch_a, w_branch_b, w_out, ln1_g, ln1_b, router_group_w, router_group_b,
           router_expert_w, router_expert_b, moe_w_gate, moe_w_up, moe_w_down, ln2_g, ln2_b):
    depth = w_in.shape[0]
    assert depth == 1, "the entry LayerNorm is fused into the layer's projections: single-layer trunk only"
    alpha = (2.0 * depth) ** 0.25
    bp = x_prompt.shape[0]
    n_fox_heads = fox_b_f.shape[1]
    rw_heads, rw_dim = state_rwkv.shape[2], state_rwkv.shape[3]
    rw_cols = state_rwkv_shift.shape[-1]
    wts = _prepare_weights(0, ln_in_g, ln_in_b, w_in, fox_b_f, fox_q_norm, fox_k_norm, rwkv_mu, rwkv_w0,
                           rwkv_w2, rwkv_a0, rwkv_a2, rwkv_g2, rwkv_k_k, rwkv_k_a, rwkv_r_k, rwkv_gn_g,
                           rwkv_gn_b, w_branch_a, w_branch_b, w_out, ln1_g, ln1_b, router_group_w,
                           router_group_b, router_expert_w, router_expert_b, moe_w_gate, moe_w_up,
                           moe_w_down, ln2_g, ln2_b)
    xp, new_p = _layer(x_prompt, jnp.zeros((bp, 0, n_fox_heads, FOX_HEAD_DIM), F32),
                       jnp.zeros((bp, 0, n_fox_heads, FOX_HEAD_DIM), F32),
                       jnp.zeros((bp, 0, n_fox_heads), F32),
                       jnp.zeros((bp, rw_heads, rw_dim, rw_dim), F32),
                       jnp.zeros((bp, 1, rw_cols), F32), wts, alpha)
    xs, new_s = _layer(x_sample, cache_fox_k[0], cache_fox_v[0], cache_fox_logf[0], state_rwkv[0],
                       state_rwkv_shift[0], wts, alpha)
    return (xp, xs) + tuple(a[None] for a in new_p) + tuple(a[None] for a in new_s)
```

```python
import functools

import jax
import jax.numpy as jnp
from jax import lax
from jax.experimental import pallas as pl
from jax.experimental.pallas import tpu as pltpu

F32 = jnp.float32
BF16 = jnp.bfloat16

FOX_HEAD_DIM = 128
RWKV_HEAD_DIM = 64
RWKV_DECAY_RANK = 64
RWKV_ICLR_RANK = 64
RWKV_GATE_RANK = 160
RWKV_GN_EPS = 64e-5
N_GROUPS = 4
EXPERTS_PER_GROUP = 8
N_EXPERTS = N_GROUPS * EXPERTS_PER_GROUP
TOP_K = 2
LN_EPS = 1e-5
QK_EPS = 1e-6
NEG_INF = -1e30
LOG2_E = 1.4426950408889634
DECAY_SCALE = 0.6065306597126334

LANES = 128
MXU_DIM = 256
VMEM_LIMIT_BYTES = 56 * 1024 * 1024

PROJ_ROWS = 1024
ATTN_BLOCK = 512
RWKV_CHUNK = 64
HEADS_PER_GROUP = MXU_DIM // RWKV_HEAD_DIM


def _params(semantics):
    return pltpu.CompilerParams(dimension_semantics=semantics, vmem_limit_bytes=VMEM_LIMIT_BYTES)


def _dot(a, b):
    return jnp.dot(a.astype(BF16), b.astype(BF16), preferred_element_type=F32)


def _dot_nt(a, b):
    return lax.dot_general(a.astype(BF16), b.astype(BF16), (((1,), (1,)), ((), ())),
                           preferred_element_type=F32)


def _split3(x):
    h1 = x.astype(BF16)
    r1 = x - h1.astype(F32)
    h2 = r1.astype(BF16)
    h3 = (r1 - h2.astype(F32)).astype(BF16)
    return h1, h2, h3


def _dot_exact_rhs(x, m_bf16):
    h1, h2, h3 = _split3(x)
    d = lambda h: jnp.dot(h, m_bf16, preferred_element_type=F32)
    return d(h1) + d(h2) + d(h3)


def _dot_exact_lhs(m_bf16, x):
    h1, h2, h3 = _split3(x)
    d = lambda h: jnp.dot(m_bf16, h, preferred_element_type=F32)
    return d(h1) + d(h2) + d(h3)


def _layer_norm(x, g, b):
    mu = jnp.mean(x, axis=-1, keepdims=True)
    xc = x - mu
    var = jnp.mean(xc * xc, axis=-1, keepdims=True)
    return xc * lax.rsqrt(var + LN_EPS) * g + b


def _sigmoid(x):
    return 1.0 / (1.0 + jnp.exp(-x))


def _log_sigmoid(x):
    return jnp.minimum(x, 0.0) - jnp.log(1.0 + jnp.exp(-jnp.abs(x)))


def _ln_kernel(x_ref, g_ref, b_ref, o_ref):
    o_ref[...] = _layer_norm(x_ref[...], g_ref[...], b_ref[...]).astype(o_ref.dtype)


def _entry_norm(x, ln_g, ln_b, tm):
    n, d = x.shape
    return pl.pallas_call(
        _ln_kernel,
        grid=(n // tm,),
        in_specs=[pl.BlockSpec((tm, d), lambda i: (i, 0)),
                  pl.BlockSpec((1, d), lambda i: (0, 0)), pl.BlockSpec((1, d), lambda i: (0, 0))],
        out_specs=pl.BlockSpec((tm, d), lambda i: (i, 0)),
        out_shape=jax.ShapeDtypeStruct((n, d), BF16),
        compiler_params=_params(("parallel",)),
        name="entry_norm",
    )(x, ln_g, ln_b)


def _gates_kernel(x_ref, w_ref, o_ref):
    y = jnp.dot(x_ref[...], w_ref[...], preferred_element_type=F32)
    o_ref[...] = _sigmoid(y).astype(o_ref.dtype)


def _rwkv_proj_kernel(x_ref, w_ref, o_ref):
    o_ref[...] = jnp.dot(x_ref[...], w_ref[...], preferred_element_type=F32)


def _proj_matmul(body, xn, w, out_dtype, tm, tn, name):
    n, d = xn.shape
    ncol = w.shape[1]
    return pl.pallas_call(
        body,
        grid=(n // tm, ncol // tn),
        in_specs=[pl.BlockSpec((tm, d), lambda i, j: (i, 0)),
                  pl.BlockSpec((d, tn), lambda i, j: (0, j))],
        out_specs=pl.BlockSpec((tm, tn), lambda i, j: (i, j)),
        out_shape=jax.ShapeDtypeStruct((n, ncol), out_dtype),
        compiler_params=_params(("parallel", "parallel")),
        name=name,
    )(xn, w)


def _fox_proj_kernel(xn_ref, w_ref, wf_ref, bf_ref, qn_ref, kn_ref,
                     qt_ref, kf_ref, kb_ref, vf_ref, vt_ref, lf_ref, *, n_heads):
    j = pl.program_id(2)
    y = jnp.dot(xn_ref[...], w_ref[...], preferred_element_type=F32)

    def rms(yh, gain):
        ms = jnp.mean(yh * yh, axis=-1, keepdims=True)
        return yh * lax.rsqrt(ms + QK_EPS) * gain

    @pl.when(j == 0)
    def _():
        scale = FOX_HEAD_DIM ** -0.5 * LOG2_E
        for h in range(n_heads):
            yh = y[:, h * FOX_HEAD_DIM:(h + 1) * FOX_HEAD_DIM]
            qt_ref[h] = (rms(yh, qn_ref[...]) * scale).astype(BF16).T
        fl = jnp.dot(xn_ref[...], wf_ref[...], preferred_element_type=F32)
        lf_ref[...] = _log_sigmoid(fl + bf_ref[...])

    @pl.when(j == 1)
    def _():
        for h in range(n_heads):
            sl = slice(h * FOX_HEAD_DIM, (h + 1) * FOX_HEAD_DIM)
            kh = rms(y[:, sl], kn_ref[...])
            kf_ref[:, sl] = kh
            kb_ref[h] = kh.astype(BF16)

    @pl.when(j == 2)
    def _():
        vf_ref[...] = y
        for h in range(n_heads):
            vt_ref[h] = y[:, h * FOX_HEAD_DIM:(h + 1) * FOX_HEAD_DIM].astype(BF16).T


def _fox_proj(xn, w_qkv, w_f, b_f, q_norm, k_norm, tm):
    bsz, t, d = xn.shape
    width = w_qkv.shape[1] // 3
    n_heads = width // FOX_HEAD_DIM
    hm = pl.BlockSpec((None, n_heads, tm, FOX_HEAD_DIM), lambda b, i, j: (b, 0, i, 0))
    tr = lambda: pl.BlockSpec((None, n_heads, None, FOX_HEAD_DIM, tm), lambda b, i, j: (b, 0, i, 0, 0))
    tok = lambda c: pl.BlockSpec((None, tm, c), lambda b, i, j: (b, i, 0))
    const = lambda r, c: pl.BlockSpec((r, c), lambda b, i, j: (0, 0))
    tr_shape = jax.ShapeDtypeStruct((bsz, n_heads, t // tm, FOX_HEAD_DIM, tm), BF16)
    tok_shape = jax.ShapeDtypeStruct((bsz, t, width), F32)
    return pl.pallas_call(
        functools.partial(_fox_proj_kernel, n_heads=n_heads),
        grid=(bsz, t // tm, 3),
        in_specs=[
            pl.BlockSpec((None, tm, d), lambda b, i, j: (b, i, 0)),
            pl.BlockSpec((d, width), lambda b, i, j: (0, j)),
            const(d, LANES), const(1, LANES),
            const(1, FOX_HEAD_DIM), const(1, FOX_HEAD_DIM),
        ],
        out_specs=[tr(), tok(width), hm, tok(width), tr(), tok(LANES)],
        out_shape=[tr_shape, tok_shape, jax.ShapeDtypeStruct((bsz, n_heads, t, FOX_HEAD_DIM), BF16),
                   tok_shape, tr_shape, jax.ShapeDtypeStruct((bsz, t, LANES), F32)],
        compiler_params=_params(("parallel", "parallel", "arbitrary")),
        name="proj_fox",
    )(xn, w_qkv, w_f, b_f, q_norm, k_norm)


def _fgate_bias_kernel(lf_ref, o_ref, carry, *, length, n_heads):
    t = pl.program_id(1)
    tt = lf_ref.shape[0]

    @pl.when(t == 0)
    def _():
        carry[...] = jnp.zeros_like(carry)

    ri = lax.broadcasted_iota(jnp.int32, (tt, tt), 0)
    rj = lax.broadcasted_iota(jnp.int32, (tt, tt), 1)
    tri = jnp.where(rj <= ri, 1.0, 0.0).astype(BF16)
    csum = _dot_exact_lhs(tri, lf_ref[...]) + carry[...]
    carry[...] = csum[tt - 1:tt, :]
    pos = t * tt + lax.broadcasted_iota(jnp.int32, csum.shape, 0)
    neg = jnp.where(pos < length, -LOG2_E * csum, NEG_INF)
    pieces = jnp.concatenate(_split3(neg), axis=1)
    sr = lax.broadcasted_iota(jnp.int32, (3 * LANES, LANES), 0)
    sc = lax.broadcasted_iota(jnp.int32, (3 * LANES, LANES), 1)
    for h in range(n_heads):
        sel = jnp.where(sr % LANES == h, jnp.where(sr // LANES == sc, 1.0, 0.0), 0.0).astype(BF16)
        o_ref[h] = jnp.dot(pieces, sel, preferred_element_type=F32).astype(BF16)


def _fgate_bias(lf, length, n_heads, tt):
    bsz, lp, _ = lf.shape
    return pl.pallas_call(
        functools.partial(_fgate_bias_kernel, length=length, n_heads=n_heads),
        grid=(bsz, lp // tt),
        in_specs=[pl.BlockSpec((None, tt, LANES), lambda b, t: (b, t, 0))],
        out_specs=pl.BlockSpec((None, n_heads, tt, LANES), lambda b, t: (b, 0, t, 0)),
        out_shape=jax.ShapeDtypeStruct((bsz, n_heads, lp, LANES), BF16),
        scratch_shapes=[pltpu.VMEM((1, LANES), F32)],
        compiler_params=_params(("parallel", "arbitrary")),
        name="fgate_bias",
    )(lf)


N_BIAS_PIECES = 3


ATTN_HEADS_PER_STEP = 2


def _attn_kernel(qt_ref, k_ref, nf_ref, vt_ref, o_ref, *, tq, tk, past):
    i = pl.program_id(2)
    n_heads = qt_ref.shape[0]
    tw = min(tq, MXU_DIM)
    chains = [(h, c) for h in range(n_heads) for c in range(tq // tw)]
    ones_rows = jnp.where(lax.broadcasted_iota(jnp.int32, (LANES, tw), 0) < N_BIAS_PIECES, 1.0, 0.0)
    qa = [jnp.concatenate([qt_ref[h, :, c * tw:(c + 1) * tw], ones_rows.astype(BF16)], axis=0)
          for h, c in chains]
    n_full = (past + i * tq) // tk

    def update(carry, s, vt):
        m, l, acc = carry
        m_new = jnp.maximum(m, jnp.max(s, axis=0, keepdims=True))
        p = jnp.exp2(s - m_new)
        alpha = jnp.exp2(m - m_new)
        l = alpha * l + jnp.sum(p, axis=0, keepdims=True)
        acc = alpha * acc + jnp.dot(vt, p.astype(BF16), preferred_element_type=F32)
        return m_new, l, acc

    def scores(j, causal=False):
        start = pl.multiple_of(j * tk, tk)
        ka = [jnp.concatenate([k_ref[h, pl.ds(start, tk), :], nf_ref[h, pl.ds(start, tk), :]], axis=1)
              for h in range(n_heads)]
        ss = [jnp.dot(ka[h], qa[n], preferred_element_type=F32) for n, (h, c) in enumerate(chains)]
        if causal:
            key = lax.broadcasted_iota(jnp.int32, (tk, tw), 0)
            qry = lax.broadcasted_iota(jnp.int32, (tk, tw), 1)
            ss = [jnp.where(key <= qry + c * tw, s, NEG_INF) for s, (h, c) in zip(ss, chains)]
        return ss

    def absorb(carry, ss, j):
        return tuple(update(carry[n], ss[n], vt_ref[h, j]) for n, (h, c) in enumerate(chains))

    def two_blocks(pair, carry):
        sa, sb = scores(2 * pair), scores(2 * pair + 1)
        return absorb(absorb(carry, sa, 2 * pair), sb, 2 * pair + 1)

    init = (jnp.full((1, tw), NEG_INF, F32), jnp.zeros((1, tw), F32), jnp.zeros((FOX_HEAD_DIM, tw), F32))
    carry = lax.fori_loop(0, n_full // 2, two_blocks, tuple(init for _ in chains))
    last_full = jnp.maximum(n_full - 1, 0)
    carry = lax.cond(n_full % 2 == 1, lambda c: absorb(c, scores(last_full), last_full), lambda c: c, carry)
    carry = absorb(carry, scores(n_full, causal=True), n_full)
    for n, (h, c) in enumerate(chains):
        _, l, acc = carry[n]
        o_ref[c * tw:(c + 1) * tw, h * FOX_HEAD_DIM:(h + 1) * FOX_HEAD_DIM] = (acc / l).T.astype(o_ref.dtype)


def _fox_attention(qt, k, nf, vt, past, tk):
    bsz, n_heads, nq, dh, tq = qt.shape
    lp = k.shape[2]
    hps = ATTN_HEADS_PER_STEP
    assert past % tk == 0 and (tq == tk or nq == 1) and tq <= tk and lp % tk == 0 and n_heads % hps == 0
    whole = lambda a: pl.BlockSpec((None, hps) + a.shape[2:], lambda b, h, i: (b, h) + (0,) * (a.ndim - 2))
    return pl.pallas_call(
        functools.partial(_attn_kernel, tq=tq, tk=tk, past=past),
        grid=(bsz, n_heads // hps, nq),
        in_specs=[pl.BlockSpec((None, hps, None, dh, tq), lambda b, h, i: (b, h, i, 0, 0)),
                  whole(k), whole(nf), whole(vt)],
        out_specs=pl.BlockSpec((None, tq, hps * dh), lambda b, h, i: (b, i, h)),
        out_shape=jax.ShapeDtypeStruct((bsz, nq * tq, n_heads * dh), BF16),
        compiler_params=_params(("parallel", "parallel", "arbitrary")),
        name="fox_attn",
    )(qt, k, nf, vt)


def _rwkv_kernel(p_ref, shift_ref, s0_ref, mu_ref, w0_ref, a0_ref, kk_ref, ka_ref, rk_ref,
                 gng_ref, gnb_ref, w2_ref, a2_ref, g2_ref, o_ref, sout_ref,
                 state, prev_row, *, tc, t_valid):
    c_len = RWKV_CHUNK
    t = pl.program_id(1)
    width = o_ref.shape[-1]
    n_groups = width // MXU_DIM

    @pl.when(t == 0)
    def _():
        state[...] = s0_ref[...]
        prev_row[...] = shift_ref[...]

    p = p_ref[...]
    ridx = lax.broadcasted_iota(jnp.int32, p.shape, 0)
    prev = jnp.where(ridx == 0, jnp.broadcast_to(prev_row[...], p.shape), pltpu.roll(p, 1, 0))
    prev_row[...] = p[tc - 1:tc, :]
    xs = p + (prev - p) * mu_ref[...]
    r = xs[:, 0:width]
    k = xs[:, width:2 * width]
    v = xs[:, 2 * width:3 * width]
    lora_in = xs[:, 3 * width:3 * width + LANES]
    gate_in = xs[:, 3 * width + LANES:]
    zw = _dot(jnp.tanh(lora_in), w2_ref[...])
    za = _dot(lora_in, a2_ref[...])
    g = _dot(_sigmoid(gate_in), g2_ref[...])
    lw = -DECAY_SCALE * _sigmoid(w0_ref[...] + zw)
    iclr = _sigmoid(a0_ref[...] + za)

    hr = lax.broadcasted_iota(jnp.int32, (MXU_DIM, MXU_DIM), 0) // RWKV_HEAD_DIM
    hc = lax.broadcasted_iota(jnp.int32, (MXU_DIM, MXU_DIM), 1) // RWKV_HEAD_DIM
    same_head = hr == hc
    bd_f32 = jnp.where(same_head, 1.0, 0.0)
    ones_bd = bd_f32.astype(BF16)

    def head_sum(x, split=True):
        hi = x.astype(BF16)
        lo = (x - hi.astype(F32)).astype(BF16) if split else None
        parts = []
        for gi in range(n_groups):
            ls = slice(gi * MXU_DIM, (gi + 1) * MXU_DIM)
            part = jnp.dot(hi[:, ls], ones_bd, preferred_element_type=F32)
            if split:
                part = part + jnp.dot(lo[:, ls], ones_bd, preferred_element_type=F32)
            parts.append(part)
        return jnp.concatenate(parts, axis=-1)

    kk = k * kk_ref[...]
    kk = kk * lax.rsqrt(jnp.maximum(head_sum(kk * kk), 1e-24))
    k = k * (1.0 + (iclr - 1.0) * ka_ref[...])
    if t_valid < tc:
        live = lax.broadcasted_iota(jnp.int32, (tc, width), 0) < t_valid
        lw = jnp.where(live, lw, 0.0)
        kk = jnp.where(live, kk, 0.0)
        k = jnp.where(live, k, 0.0)
        v = jnp.where(live, v, 0.0)

    ti = lax.broadcasted_iota(jnp.int32, (tc, tc), 0)
    tj = lax.broadcasted_iota(jnp.int32, (tc, tc), 1)
    tri = jnp.where(ti // c_len == tj // c_len, jnp.where(tj <= ti, 1.0, 0.0), 0.0).astype(BF16)
    gcum = _dot_exact_lhs(tri, lw)
    e_in = jnp.exp(gcum)
    e_inv = jnp.exp(-gcum)
    at_all = -kk * jnp.exp(gcum - lw)
    rt_all = r * e_in
    bt_all = kk * iclr * e_inv
    kt_all = k * e_inv
    bonus = head_sum(r * k * rk_ref[...]) * v

    row = lax.broadcasted_iota(jnp.int32, (c_len, MXU_DIM), 0)
    lane = lax.broadcasted_iota(jnp.int32, (c_len, MXU_DIM), 1) % c_len
    strict = jnp.where(lane < row, 1.0, 0.0)
    incl = jnp.where(lane <= row, 1.0, 0.0)
    eye_w = jnp.where(lane == row, 1.0, 0.0)

    def bd(x):
        return jnp.concatenate([x.astype(BF16)] * HEADS_PER_GROUP, axis=0) * ones_bd

    n_sq = c_len.bit_length() - 1
    n_chunks = tc // c_len
    units = [(ci, gi) for ci in range(n_chunks) for gi in range(n_groups)]

    def cut(x, u):
        ci, gi = u
        return x[ci * c_len:(ci + 1) * c_len, gi * MXU_DIM:(gi + 1) * MXU_DIM]

    at = [cut(at_all, u) for u in units]
    rt = [cut(rt_all, u) for u in units]
    bt = [cut(bt_all, u) for u in units]
    kt = [cut(kt_all, u) for u in units]
    vv = [cut(v, u) for u in units]
    ar = [jnp.concatenate([a, r_], axis=0) for a, r_ in zip(at, rt)]
    ab = [_dot_nt(x, bd(b_)) for x, b_ in zip(ar, bt)]
    ak = [_dot_nt(x, bd(k_)) for x, k_ in zip(ar, kt)]
    pw = [x[:c_len] * strict for x in ab]
    a_rb = [x[c_len:] * incl for x in ab]
    a_ak = [x[:c_len] * strict for x in ak]
    a_rk = [x[c_len:] * incl for x in ak]
    tm = [eye_w + x for x in pw]
    pw = [_dot(x, bd(x)) for x in pw]
    for js in range(1, n_sq):
        if js < n_sq - 1:
            tp = [_dot(jnp.concatenate([t_, x], axis=0), bd(x)) for t_, x in zip(tm, pw)]
            tm = [t_ + y_[:c_len] for t_, y_ in zip(tm, tp)]
            pw = [y_[c_len:] for y_ in tp]
        else:
            tm = [t_ + _dot(t_, bd(x)) for t_, x in zip(tm, pw)]
    bdv = [bd(x) for x in vv]
    a_hat = [_dot(t_, bd(a)) for t_, a in zip(tm, at)]
    av = [_dot(x, b_) for x, b_ in zip(a_ak, bdv)]
    u_hat = [_dot(t_, bd(x)) for t_, x in zip(tm, av)]
    r_hat = [r_ + _dot(x, bd(a)) for r_, x, a in zip(rt, a_rb, a_hat)]
    y_hat = [_dot(x, bd(uh)) + _dot(z, b_) for x, uh, z, b_ in zip(a_rb, u_hat, a_rk, bdv)]
    lhs = [jnp.concatenate([a, r_], axis=0) for a, r_ in zip(a_hat, r_hat)]

    st = [state[gi] for gi in range(n_groups)]
    y_rows = []
    for ci in range(n_chunks):
        gend = jnp.exp(gcum[(ci + 1) * c_len - 1:(ci + 1) * c_len, :])
        us = [ci * n_groups + gi for gi in range(n_groups)]
        ge = [gend[:, gi * MXU_DIM:(gi + 1) * MXU_DIM] for gi in range(n_groups)]
        uy = [_dot_nt(lhs[u], st[gi]) for gi, u in enumerate(us)]
        uu = [uy[gi][:c_len] + u_hat[u] for gi, u in enumerate(us)]
        y_rows.append(jnp.concatenate([uy[gi][c_len:] + y_hat[u] for gi, u in enumerate(us)], axis=1))
        uv_t = [jnp.concatenate([uu[gi], vv[u]], axis=0).T for gi, u in enumerate(us)]
        bk = [jnp.concatenate([bt[u] * ge[gi], kt[u] * ge[gi]], axis=0) for gi, u in enumerate(us)]
        st = [st[gi] * ge[gi] + _dot(uv_t[gi], bk[gi]) * bd_f32 for gi in range(n_groups)]
    for gi in range(n_groups):
        state[gi] = st[gi]

    y = jnp.concatenate(y_rows, axis=0)
    inv_n = 1.0 / RWKV_HEAD_DIM
    mean = head_sum(y, split=False) * inv_n
    yc = y - mean
    var = head_sum(yc * yc, split=False) * inv_n
    yn = yc * lax.rsqrt(var + RWKV_GN_EPS) * gng_ref[...] + gnb_ref[...]
    o_ref[...] = ((yn + bonus) * g).astype(o_ref.dtype)

    @pl.when(t == pl.num_programs(1) - 1)
    def _():
        sout_ref[...] = state[...]


def _rwkv_mix(p, shift_prev, s0_bd, vecs, w2p, a2p, g2p, tc, t_valid):
    bsz, tp, cols = p.shape
    width = w2p.shape[1]
    n_groups = width // MXU_DIM
    vec_specs = [pl.BlockSpec((1, a.shape[1]), lambda b, t: (0, 0)) for a in vecs]
    mat = lambda a: pl.BlockSpec(a.shape, lambda b, t: (0, 0))
    st_spec = pl.BlockSpec((None, n_groups, MXU_DIM, MXU_DIM), lambda b, t: (b, 0, 0, 0))
    return pl.pallas_call(
        functools.partial(_rwkv_kernel, tc=tc, t_valid=t_valid),
        grid=(bsz, tp // tc),
        in_specs=[pl.BlockSpec((None, tc, cols), lambda b, t: (b, t, 0)),
                  pl.BlockSpec((None, 1, cols), lambda b, t: (b, 0, 0)),
                  st_spec] + vec_specs + [mat(w2p), mat(a2p), mat(g2p)],
        out_specs=[pl.BlockSpec((None, tc, width), lambda b, t: (b, t, 0)), st_spec],
        out_shape=[jax.ShapeDtypeStruct((bsz, tp, width), BF16),
                   jax.ShapeDtypeStruct(s0_bd.shape, F32)],
        scratch_shapes=[pltpu.VMEM((n_groups, MXU_DIM, MXU_DIM), F32), pltpu.VMEM((1, cols), F32)],
        compiler_params=_params(("parallel", "arbitrary")),
        name="rwkv_mix",
    )(p, shift_prev, s0_bd, *vecs, w2p, a2p, g2p)


def _merge_kernel(x_ref, oa_ref, ob_ref, gt_ref, wa_ref, wb_ref, wo_ref, lig_ref, lib_ref,
                  l1g_ref, l1b_ref, wr_ref, rb_ref, x1_ref, ids_ref, wts_ref, *, alpha):
    d = x_ref.shape[-1]
    xn = _layer_norm(x_ref[...], lig_ref[...], lib_ref[...])
    ya = jnp.dot(oa_ref[...], wa_ref[...], preferred_element_type=F32)
    yb = jnp.dot(ob_ref[...], wb_ref[...], preferred_element_type=F32)
    merged = gt_ref[:, :d].astype(F32) * ya + gt_ref[:, d:].astype(F32) * yb
    out = jnp.dot(merged.astype(BF16), wo_ref[...], preferred_element_type=F32)
    x1 = _layer_norm(alpha * xn + out, l1g_ref[...], l1b_ref[...])
    x1_ref[...] = x1

    h1, h2, _ = _split3(x1)
    r1 = jnp.dot(h1, wr_ref[...], preferred_element_type=F32)
    r2 = jnp.dot(h2, wr_ref[...], preferred_element_type=F32)
    logits = (r1[:, :LANES] + (r1[:, LANES:] + r2[:, :LANES]) + r2[:, LANES:]) + rb_ref[...]
    lane = lax.broadcasted_iota(jnp.int32, logits.shape, 1).astype(F32)
    big = 1e9

    def first_max(vals):
        mx = jnp.max(vals, axis=-1, keepdims=True)
        idx = jnp.min(jnp.where(vals == mx, lane, big), axis=-1, keepdims=True)
        return mx, idx

    is_grp = lane < N_GROUPS
    gmax, grp = first_max(jnp.where(is_grp, logits, NEG_INF))
    p_grp = 1.0 / jnp.sum(jnp.where(is_grp, jnp.exp(logits - gmax), 0.0), axis=-1, keepdims=True)
    lo = N_GROUPS + grp * EXPERTS_PER_GROUP
    elog = jnp.where(lane >= lo, jnp.where(lane < lo + EXPERTS_PER_GROUP, logits, NEG_INF), NEG_INF)
    v1, i1 = first_max(elog)
    v2, i2 = first_max(jnp.where(lane == i1, NEG_INF, elog))
    e2 = jnp.exp(v2 - v1)
    w1 = p_grp / (1.0 + e2)
    w2 = p_grp * e2 / (1.0 + e2)
    ids = jnp.where(lane == 0, i1 - N_GROUPS, jnp.where(lane == 1, i2 - N_GROUPS, 0.0))
    ids_ref[...] = ids.astype(jnp.int32)
    wts_ref[...] = jnp.where(lane == 0, w1, jnp.where(lane == 1, w2, 0.0))


def _merge_out(x, oa, ob, gates, wa, wb, wo, lig, lib, l1g, l1b, w_router, rbias, alpha, tm):
    n, d = x.shape
    half = oa.shape[1]
    row = lambda c: pl.BlockSpec((tm, c), lambda i: (i, 0))
    const = lambda a: pl.BlockSpec(a.shape, lambda i: (0, 0), pipeline_mode=pl.Buffered(1))
    return pl.pallas_call(
        functools.partial(_merge_kernel, alpha=alpha),
        grid=(n // tm,),
        in_specs=[row(d), row(half), row(half), row(2 * d), const(wa), const(wb), const(wo),
                  const(lig), const(lib), const(l1g), const(l1b), const(w_router), const(rbias)],
        out_specs=[row(d), row(LANES), row(LANES)],
        out_shape=[jax.ShapeDtypeStruct((n, d), F32), jax.ShapeDtypeStruct((n, LANES), jnp.int32),
                   jax.ShapeDtypeStruct((n, LANES), F32)],
        compiler_params=_params(("parallel",)),
        name="merge_out",
    )(x, oa, ob, gates, wa, wb, wo, lig, lib, l1g, l1b, w_router, rbias)


DMA_ISSUE_UNROLL = 8
MOE_BLOCK = 512


def _wait_rows(make_row_copy, count):
    lax.fori_loop(0, count, lambda q, c: (make_row_copy(0).wait(), c)[1], 0, unroll=DMA_ISSUE_UNROLL)


def _dispatch_kernel(zblk_ref, dest_ref, *rest, steps):
    xs_hbm, zeros, sem, zsem = rest[-4:]
    x_refs = rest[:-4]
    i = pl.program_id(0)
    blk = zeros.shape[0]

    def zero_block(z):
        start = pl.multiple_of(zblk_ref[z] * blk, blk)
        return pltpu.make_async_copy(zeros, xs_hbm.at[pl.ds(start, blk), :], zsem)

    @pl.when(i == 0)
    def _():
        zeros[...] = jnp.zeros_like(zeros)
        for z in range(zblk_ref.shape[0]):
            pl.when(zblk_ref[z] >= 0)(lambda z=z: zero_block(z).start())
        for z in range(zblk_ref.shape[0]):
            pl.when(zblk_ref[z] >= 0)(lambda z=z: zero_block(z).wait())

    def scatter_rows(x_ref):
        tm = x_ref.shape[0]

        def to_slot(r, kslot):
            return pltpu.make_async_copy(x_ref.at[pl.ds(r, 1), :],
                                         xs_hbm.at[pl.ds(dest_ref[0, TOP_K * r + kslot], 1), :], sem)

        def body(r, c):
            for kslot in range(TOP_K):
                to_slot(r, kslot).start()
            return c
        lax.fori_loop(0, tm, body, 0, unroll=DMA_ISSUE_UNROLL)
        _wait_rows(lambda q: to_slot(0, 0), TOP_K * tm)

    for g, x_ref in enumerate(x_refs):
        in_group = jnp.logical_and(i >= steps[g], i < steps[g + 1])
        pl.when(in_group)(functools.partial(scatter_rows, x_ref))


def _moe_dispatch(x1s, tiles, zero_blocks, dest_steps, n_rows, blk):
    d = x1s[0].shape[1]
    steps = [0]
    for x1, tm in zip(x1s, tiles):
        steps.append(steps[-1] + x1.shape[0] // tm)

    def x_spec(g, tm):
        last = steps[g + 1] - steps[g] - 1
        return pl.BlockSpec((tm, d), lambda i, zb: (jnp.clip(i - steps[g], 0, last), 0))

    grid_spec = pltpu.PrefetchScalarGridSpec(
        num_scalar_prefetch=1,
        grid=(steps[-1],),
        in_specs=[pl.BlockSpec((None, 1, dest_steps.shape[-1]), lambda i, zb: (i, 0, 0),
                               memory_space=pltpu.SMEM)] + [x_spec(g, tm) for g, tm in enumerate(tiles)],
        out_specs=pl.BlockSpec(memory_space=pl.ANY),
        scratch_shapes=[pltpu.VMEM((blk, d), F32), pltpu.SemaphoreType.DMA(()), pltpu.SemaphoreType.DMA(())],
    )
    return pl.pallas_call(
        functools.partial(_dispatch_kernel, steps=tuple(steps)),
        grid_spec=grid_spec,
        out_shape=jax.ShapeDtypeStruct((n_rows, d), F32),
        compiler_params=_params(("arbitrary",)),
        name="moe_dispatch",
    )(zero_blocks, dest_steps, *x1s)


def _experts_kernel(blk_e_ref, xblk_ref, nvalid_ref, xs_ref, wg_ref, wu_ref, wd_ref, ys_ref):
    j = pl.program_id(0)

    @pl.when(nvalid_ref[j] > 0)
    def _():
        xv = xs_ref[...].astype(BF16)
        hg = jnp.dot(xv, wg_ref[...].astype(BF16), preferred_element_type=F32)
        hu = jnp.dot(xv, wu_ref[...].astype(BF16), preferred_element_type=F32)
        h = hg * _sigmoid(hg) * hu
        ys_ref[...] = jnp.dot(h.astype(BF16), wd_ref[...].astype(BF16), preferred_element_type=F32)

    @pl.when(nvalid_ref[j] == 0)
    def _():
        ys_ref[...] = jnp.zeros_like(ys_ref)


def _moe_experts(xs, blk_e, xblk, nvalid, w_gate, w_up, w_down, blk):
    d = xs.shape[1]
    nb = blk_e.shape[0]
    de = w_gate.shape[-1]
    grid_spec = pltpu.PrefetchScalarGridSpec(
        num_scalar_prefetch=3,
        grid=(nb,),
        in_specs=[
            pl.BlockSpec((blk, d), lambda j, be, xb, nv: (xb[j], 0)),
            pl.BlockSpec((None, d, de), lambda j, be, xb, nv: (be[j], 0, 0)),
            pl.BlockSpec((None, d, de), lambda j, be, xb, nv: (be[j], 0, 0)),
            pl.BlockSpec((None, de, d), lambda j, be, xb, nv: (be[j], 0, 0)),
        ],
        out_specs=pl.BlockSpec((blk, d), lambda j, be, xb, nv: (j, 0)),
    )
    return pl.pallas_call(
        _experts_kernel,
        grid_spec=grid_spec,
        out_shape=jax.ShapeDtypeStruct((nb * blk, d), F32),
        compiler_params=_params(("arbitrary",)),
        name="moe_experts",
    )(blk_e, xblk, nvalid, xs, w_gate, w_up, w_down)


def _combine_kernel(cur_ref, nxt_ref, x1_ref, wts_ref, g_ref, b_ref, ys_hbm, o_ref, ybuf, sem, *, alpha):
    i = pl.program_id(0)
    tm = x1_ref.shape[0]
    slot = i % 2

    def row(tbl_ref, s, r, kslot):
        return pltpu.make_async_copy(ys_hbm.at[pl.ds(tbl_ref[0, TOP_K * r + kslot], 1), :],
                                     ybuf.at[s, pl.ds(kslot * tm + r, 1), :], sem.at[s])

    def fetch(tbl_ref, s):
        def body(r, c):
            for kslot in range(TOP_K):
                row(tbl_ref, s, r, kslot).start()
            return c
        lax.fori_loop(0, tm, body, 0, unroll=DMA_ISSUE_UNROLL)

    @pl.when(i == 0)
    def _():
        fetch(cur_ref, 0)

    @pl.when(i + 1 < pl.num_programs(0))
    def _():
        fetch(nxt_ref, 1 - slot)

    _wait_rows(lambda q: row(cur_ref, slot, 0, 0), TOP_K * tm)
    w = wts_ref[...]
    y = w[:, 0:1] * ybuf[slot, 0:tm, :] + w[:, 1:2] * ybuf[slot, tm:2 * tm, :]
    o_ref[...] = _layer_norm(alpha * x1_ref[...] + y, g_ref[...], b_ref[...])


def _moe_combine(x1, ys, dest_tiles, wts, g, b, alpha, tm):
    n, d = x1.shape
    last = n // tm - 1
    tbl = lambda f: pl.BlockSpec((None, 1, TOP_K * tm), lambda i: (f(i), 0, 0), memory_space=pltpu.SMEM)
    return pl.pallas_call(
        functools.partial(_combine_kernel, alpha=alpha),
        grid=(n // tm,),
        in_specs=[tbl(lambda i: i), tbl(lambda i: jnp.minimum(i + 1, last)),
                  pl.BlockSpec((tm, d), lambda i: (i, 0)),
                  pl.BlockSpec((tm, LANES), lambda i: (i, 0)),
                  pl.BlockSpec((1, d), lambda i: (0, 0)), pl.BlockSpec((1, d), lambda i: (0, 0)),
                  pl.BlockSpec(memory_space=pl.ANY)],
        out_specs=pl.BlockSpec((tm, d), lambda i: (i, 0)),
        out_shape=jax.ShapeDtypeStruct((n, d), F32),
        scratch_shapes=[pltpu.VMEM((2, TOP_K * tm, d), F32), pltpu.SemaphoreType.DMA((2,))],
        compiler_params=_params(("arbitrary",)),
        name="moe_combine",
    )(dest_tiles, dest_tiles, x1, wts, g, b, ys)


def _dispatch_tables(eid, blk):
    n = eid.shape[0]
    nk = n * TOP_K
    nb = -(-(nk + N_EXPERTS * (blk - 1)) // blk)
    experts = jnp.arange(N_EXPERTS, dtype=jnp.int32)
    chunk = LANES
    e2 = eid.reshape(nk // chunk, chunk)
    onehot = (e2[:, :, None] == experts).astype(jnp.int32)
    chunk_counts = onehot.sum(axis=1)
    chunk_base = jnp.cumsum(chunk_counts, axis=0) - chunk_counts
    earlier = jnp.arange(chunk)[None, :] < jnp.arange(chunk)[:, None]
    in_chunk = ((e2[:, :, None] == e2[:, None, :]) & earlier[None]).sum(axis=2)
    counts = chunk_counts.sum(axis=0)
    nblk_e = (counts + blk - 1) // blk
    blk_end = jnp.cumsum(nblk_e)
    blk_start = blk_end - nblk_e
    base = chunk_base + (blk_start * blk)[None, :]
    dest = ((onehot * base[:, None, :]).sum(axis=2) + in_chunk).reshape(-1)
    n_used = blk_end[-1]
    tail = n_used + experts
    zero_blocks = jnp.concatenate([jnp.where(nblk_e > 0, blk_end - 1, -1), jnp.where(tail < nb, tail, -1)])
    bidx = jnp.arange(nb, dtype=jnp.int32)
    blk_e = jnp.minimum((blk_end[None, :] <= bidx[:, None]).sum(axis=1), N_EXPERTS - 1).astype(jnp.int32)
    mine = (blk_e[:, None] == experts[None, :]).astype(jnp.int32)
    left = (mine * (counts + blk_start * blk)[None, :]).sum(axis=1) - bidx * blk
    nvalid = jnp.clip(left, 0, blk).astype(jnp.int32)
    nvalid = jnp.where(bidx < n_used, nvalid, 0)
    xblk = jnp.minimum(bidx, n_used - 1).astype(jnp.int32)
    return dest.astype(jnp.int32), zero_blocks.astype(jnp.int32), blk_e, xblk, nvalid, nb * blk


def _pick(n, prefs):
    for p in prefs:
        if n % p == 0:
            return p
    return n


def _layer(x, past_k, past_v, past_logf, s0, shift_prev, wts, alpha):
    bsz, t, d = x.shape
    n = bsz * t
    xf = x.reshape(n, d)
    fox_width = wts['w_qkv'].shape[1] // 3
    n_heads = fox_width // FOX_HEAD_DIM
    rw_width = wts['w2p'].shape[1]
    rw_heads = rw_width // RWKV_HEAD_DIM
    rw_cols = wts['rwkv_cols']
    past = past_k.shape[1]

    xn = _entry_norm(xf, wts['ln_in_g'], wts['ln_in_b'], _pick(n, (512, 256)))
    tm = _pick(n, (PROJ_ROWS, 512, 256))
    gates = _proj_matmul(_gates_kernel, xn, wts['w_gates'], BF16, tm, 1024, "proj_gates")
    p_rw = _proj_matmul(_rwkv_proj_kernel, xn, wts['w_rwkv'], F32, tm, wts['w_rwkv'].shape[1] // 3, "proj_rwkv")
    t_fox = -(-t // LANES) * LANES
    tq = _pick(t_fox, (ATTN_BLOCK, 256, LANES))
    xn_fox = xn.reshape(bsz, t, d)
    if t_fox != t:
        xn_fox = jnp.pad(xn_fox, ((0, 0), (0, t_fox - t), (0, 0)))
    qt, k_f, k_b, v_f, vt, lf = _fox_proj(xn_fox, wts['w_qkv'], wts['w_f'], wts['b_f'], wts['q_norm'],
                                          wts['k_norm'], tq)
    k_f, v_f, logf = k_f[:, :t], v_f[:, :t], lf[:, :t, :n_heads]
    if past:
        tk = ATTN_BLOCK
        lpad = -(-(past + t) // tk) * tk
        grow = lambda a, ax: jnp.pad(a, [(0, lpad - a.shape[ax]) if i == ax else (0, 0) for i in range(a.ndim)])
        lf_past = jnp.pad(past_logf.astype(F32), ((0, 0), (0, 0), (0, LANES - n_heads)))
        lf_all = grow(jnp.concatenate([lf_past, lf[:, :t]], axis=1), 1)
        k_all = grow(jnp.concatenate([jnp.swapaxes(past_k, 1, 2).astype(BF16), k_b[:, :, :t]], axis=2), 2)
        vt_past = jnp.transpose(past_v, (0, 2, 3, 1)).astype(BF16)
        vt_all = grow(jnp.concatenate([vt_past, vt[:, :, 0, :, :t]], axis=3), 3)
        vt_all = jnp.swapaxes(vt_all.reshape(bsz, n_heads, FOX_HEAD_DIM, lpad // tk, tk), 2, 3)
    else:
        tk, lf_all, k_all, vt_all = tq, lf, k_b, vt
    nf = _fgate_bias(lf_all, past + t, n_heads, tk)
    o_a = _fox_attention(qt, k_all, nf, vt_all, past, tk)[:, :t]

    tp = -(-t // RWKV_CHUNK) * RWKV_CHUNK
    tc = _pick(tp, (256, 128, 64))
    p3 = p_rw.reshape(bsz, t, -1)
    shift_new = p3[:, t - 1:t, :rw_cols]
    if tp != t:
        p3 = jnp.pad(p3, ((0, 0), (0, tp - t), (0, 0)))
    shift_in = jnp.pad(shift_prev.astype(F32), ((0, 0), (0, 0), (0, p3.shape[-1] - rw_cols)))
    n_grp = rw_heads // HEADS_PER_GROUP
    eye = jnp.eye(HEADS_PER_GROUP, dtype=F32)
    s0_g = s0.astype(F32).reshape(bsz, n_grp, HEADS_PER_GROUP, RWKV_HEAD_DIM, RWKV_HEAD_DIM)
    s0_bd = jnp.einsum('bghvk,hj->bghvjk', s0_g, eye).reshape(bsz, n_grp, MXU_DIM, MXU_DIM)
    o_b, s_bd = _rwkv_mix(p3, shift_in, s0_bd, wts['rwkv_vecs'], wts['w2p'], wts['a2p'], wts['g2p'], tc, t)
    s_new = jnp.einsum('bghvjk,hj->bghvk',
                       s_bd.reshape(bsz, n_grp, HEADS_PER_GROUP, RWKV_HEAD_DIM, HEADS_PER_GROUP, RWKV_HEAD_DIM),
                       eye).reshape(bsz, rw_heads, RWKV_HEAD_DIM, RWKV_HEAD_DIM)
    o_b = o_b[:, :t].reshape(n, rw_width)

    routed = _merge_out(xf, o_a.reshape(n, fox_width), o_b, gates, wts['w_a'], wts['w_b'], wts['w_o'],
                        wts['ln_in_g'], wts['ln_in_b'], wts['ln1_g'], wts['ln1_b'],
                        wts['w_router'], wts['b_router'], alpha, _pick(n, (512, 256)))
    hd = (bsz, t, n_heads, FOX_HEAD_DIM)
    return routed, (k_f.reshape(hd), v_f.reshape(hd), logf, s_new, shift_new)


def _hier_moe(groups, wts, alpha):
    sizes = [x1.shape[0] for x1, _, _ in groups]
    n_all = sum(sizes)
    blk = MOE_BLOCK
    while blk > 8 and blk * N_EXPERTS > n_all * TOP_K:
        blk //= 2
    eid = jnp.concatenate([ids[:, :TOP_K] for _, ids, _ in groups], axis=0)
    dest, zero_blocks, blk_e, xblk, nvalid, n_rows = _dispatch_tables(eid, blk)
    offset, dests = 0, []
    for n in sizes:
        dests.append(dest[TOP_K * offset:TOP_K * (offset + n)])
        offset += n
    tiles = [_pick(n, (512, 256)) for n in sizes]
    width = TOP_K * max(tiles)
    dest_steps = jnp.concatenate([jnp.pad(dst.reshape(-1, 1, TOP_K * tm), ((0, 0), (0, 0), (0, width - TOP_K * tm)))
                                  for dst, tm in zip(dests, tiles)], axis=0)
    xs = _moe_dispatch([x1 for x1, _, _ in groups], tiles, zero_blocks, dest_steps, n_rows, blk)
    ys = _moe_experts(xs, blk_e, xblk, nvalid, wts['moe_w_gate'], wts['moe_w_up'], wts['moe_w_down'], blk)
    outs = []
    for (x1, _, rw), dst, n in zip(groups, dests, sizes):
        tm = _pick(n, (256, 128, 64, 32, 16, 8))
        outs.append(_moe_combine(x1, ys, dst.reshape(-1, 1, TOP_K * tm), rw, wts['ln2_g'], wts['ln2_b'],
                                 alpha, tm))
    return outs


def _prepare_weights(l, ln_in_g, ln_in_b, w_in, fox_b_f, fox_q_norm, fox_k_norm, rwkv_mu, rwkv_w0, rwkv_w2,
                     rwkv_a0, rwkv_a2, rwkv_g2, rwkv_k_k, rwkv_k_a, rwkv_r_k, rwkv_gn_g, rwkv_gn_b,
                     w_branch_a, w_branch_b, w_out, ln1_g, ln1_b, router_group_w, router_group_b,
                     router_expert_w, router_expert_b, moe_w_gate, moe_w_up, moe_w_down, ln2_g, ln2_b):
    d = w_in.shape[1]
    fox_width = w_branch_a.shape[1]
    rw_width = w_branch_b.shape[1]
    n_heads = fox_width // FOX_HEAD_DIM
    gate_cols = 2 * d
    fox_cols = 3 * fox_width + n_heads
    rw_cols = 3 * rw_width + RWKV_DECAY_RANK + RWKV_ICLR_RANK + RWKV_GATE_RANK
    row = lambda a: a.astype(F32).reshape(1, -1)
    w = w_in[l]
    w_fox = w[:, gate_cols:gate_cols + fox_cols]
    w_rw = w[:, gate_cols + fox_cols:]
    lora = RWKV_DECAY_RANK + RWKV_ICLR_RANK
    assert lora == LANES
    gate_pad = -(-RWKV_GATE_RANK // LANES) * LANES
    cols_pad = 3 * rw_width + lora + gate_pad
    pad_c = cols_pad - rw_cols
    zeros = lambda r: jnp.zeros((r, rw_width), F32)
    wr = jnp.concatenate([router_group_w[l], router_expert_w[l]], axis=1).astype(F32)
    wr = jnp.pad(wr, ((0, 0), (0, LANES - wr.shape[1])))
    rb = jnp.concatenate([router_group_b[l], router_expert_b[l]]).astype(F32)
    return {
        'ln_in_g': row(ln_in_g), 'ln_in_b': row(ln_in_b),
        'w_gates': w[:, :gate_cols].astype(BF16),
        'w_qkv': w_fox[:, :3 * fox_width].astype(BF16),
        'w_f': jnp.pad(w_fox[:, 3 * fox_width:], ((0, 0), (0, LANES - n_heads))).astype(BF16),
        'b_f': jnp.pad(row(fox_b_f[l]), ((0, 0), (0, LANES - n_heads))),
        'q_norm': row(fox_q_norm[l]), 'k_norm': row(fox_k_norm[l]),
        'w_rwkv': jnp.pad(w_rw, ((0, 0), (0, pad_c))).astype(BF16),
        'rwkv_cols': rw_cols,
        'rwkv_vecs': [jnp.pad(row(rwkv_mu[l]), ((0, 0), (0, pad_c))), row(rwkv_w0[l]), row(rwkv_a0[l]),
                      row(rwkv_k_k[l]), row(rwkv_k_a[l]), row(rwkv_r_k[l]), row(rwkv_gn_g[l]),
                      row(rwkv_gn_b[l])],
        'w2p': jnp.concatenate([rwkv_w2[l].astype(F32), zeros(RWKV_ICLR_RANK)]).astype(BF16),
        'a2p': jnp.concatenate([zeros(RWKV_DECAY_RANK), rwkv_a2[l].astype(F32)]).astype(BF16),
        'g2p': jnp.concatenate([rwkv_g2[l].astype(F32), zeros(gate_pad - RWKV_GATE_RANK)]).astype(BF16),
        'w_a': w_branch_a[l].astype(BF16), 'w_b': w_branch_b[l].astype(BF16), 'w_o': w_out[l].astype(BF16),
        'ln1_g': row(ln1_g[l]), 'ln1_b': row(ln1_b[l]),
        'w_router': jnp.concatenate(_split3(wr)[:2], axis=1),
        'b_router': jnp.pad(row(rb), ((0, 0), (0, LANES - rb.shape[0]))),
        'moe_w_gate': moe_w_gate[l], 'moe_w_up': moe_w_up[l], 'moe_w_down': moe_w_down[l],
        'ln2_g': row(ln2_g[l]), 'ln2_b': row(ln2_b[l]),
    }


def kernel(x_prompt, x_sample, cache_fox_k, cache_fox_v, cache_fox_logf, state_rwkv, state_rwkv_shift,
           ln_in_g, ln_in_b, w_in, fox_b_f, fox_q_norm, fox_k_norm, rwkv_mu, rwkv_w0, rwkv_w2,
           rwkv_a0, rwkv_a2, rwkv_g2, rwkv_k_k, rwkv_k_a, rwkv_r_k, rwkv_gn_g, rwkv_gn_b,
           w_branch_a, w_branch_b, w_out, ln1_g, ln1_b, router_group_w, router_group_b,
           router_expert_w, router_expert_b, moe_w_gate, moe_w_up, moe_w_down, ln2_g, ln2_b):
    depth = w_in.shape[0]
    assert depth == 1, "the entry LayerNorm is fused into the layer's projections: single-layer trunk only"
    alpha = (2.0 * depth) ** 0.25
    bp = x_prompt.shape[0]
    n_fox_heads = fox_b_f.shape[1]
    rw_heads, rw_dim = state_rwkv.shape[2], state_rwkv.shape[3]
    rw_cols = state_rwkv_shift.shape[-1]
    wts = _prepare_weights(0, ln_in_g, ln_in_b, w_in, fox_b_f, fox_q_norm, fox_k_norm, rwkv_mu, rwkv_w0,
                           rwkv_w2, rwkv_a0, rwkv_a2, rwkv_g2, rwkv_k_k, rwkv_k_a, rwkv_r_k, rwkv_gn_g,
                           rwkv_gn_b, w_branch_a, w_branch_b, w_out, ln1_g, ln1_b, router_group_w,
                           router_group_b, router_expert_w, router_expert_b, moe_w_gate, moe_w_up,
                           moe_w_down, ln2_g, ln2_b)
    routed_p, new_p = _layer(x_prompt, jnp.zeros((bp, 0, n_fox_heads, FOX_HEAD_DIM), F32),
                             jnp.zeros((bp, 0, n_fox_heads, FOX_HEAD_DIM), F32),
                             jnp.zeros((bp, 0, n_fox_heads), F32),
                             jnp.zeros((bp, rw_heads, rw_dim, rw_dim), F32),
                             jnp.zeros((bp, 1, rw_cols), F32), wts, alpha)
    routed_s, new_s = _layer(x_sample, cache_fox_k[0], cache_fox_v[0], cache_fox_logf[0], state_rwkv[0],
                             state_rwkv_shift[0], wts, alpha)
    yp, ys = _hier_moe([routed_p, routed_s], wts, alpha)
    return ((yp.reshape(x_prompt.shape), ys.reshape(x_sample.shape))
            + tuple(a[None] for a in new_p) + tuple(a[None] for a in new_s))
```

```python
import functools

import jax
import jax.numpy as jnp
from jax import lax
from jax.experimental import pallas as pl
from jax.experimental.pallas import tpu as pltpu

F32 = jnp.float32
BF16 = jnp.bfloat16

FOX_HEAD_DIM = 128
RWKV_HEAD_DIM = 64
RWKV_DECAY_RANK = 64
RWKV_ICLR_RANK = 64
RWKV_GATE_RANK = 160
RWKV_GN_EPS = 64e-5
N_GROUPS = 4
EXPERTS_PER_GROUP = 8
N_EXPERTS = N_GROUPS * EXPERTS_PER_GROUP
TOP_K = 2
LN_EPS = 1e-5
QK_EPS = 1e-6
NEG_INF = -1e30
LOG2_E = 1.4426950408889634
DECAY_SCALE = 0.6065306597126334

LANES = 128
MXU_DIM = 256
VMEM_LIMIT_BYTES = 56 * 1024 * 1024

PROJ_ROWS = 1024
ATTN_BLOCK = 512
RWKV_CHUNK = 64
HEADS_PER_GROUP = MXU_DIM // RWKV_HEAD_DIM


def _params(semantics):
    return pltpu.CompilerParams(dimension_semantics=semantics, vmem_limit_bytes=VMEM_LIMIT_BYTES)


def _dot(a, b):
    return jnp.dot(a.astype(BF16), b.astype(BF16), preferred_element_type=F32)


def _dot_nt(a, b):
    return lax.dot_general(a.astype(BF16), b.astype(BF16), (((1,), (1,)), ((), ())),
                           preferred_element_type=F32)


def _split3(x):
    h1 = x.astype(BF16)
    r1 = x - h1.astype(F32)
    h2 = r1.astype(BF16)
    h3 = (r1 - h2.astype(F32)).astype(BF16)
    return h1, h2, h3


def _dot_exact_rhs(x, m_bf16):
    h1, h2, h3 = _split3(x)
    d = lambda h: jnp.dot(h, m_bf16, preferred_element_type=F32)
    return d(h1) + d(h2) + d(h3)


def _dot_exact_lhs(m_bf16, x):
    h1, h2, h3 = _split3(x)
    d = lambda h: jnp.dot(m_bf16, h, preferred_element_type=F32)
    return d(h1) + d(h2) + d(h3)


def _layer_norm(x, g, b):
    mu = jnp.mean(x, axis=-1, keepdims=True)
    xc = x - mu
    var = jnp.mean(xc * xc, axis=-1, keepdims=True)
    return xc * lax.rsqrt(var + LN_EPS) * g + b


def _sigmoid(x):
    return 1.0 / (1.0 + jnp.exp(-x))


def _log_sigmoid(x):
    return jnp.minimum(x, 0.0) - jnp.log(1.0 + jnp.exp(-jnp.abs(x)))


def _ln_kernel(x_ref, g_ref, b_ref, o_ref):
    o_ref[...] = _layer_norm(x_ref[...], g_ref[...], b_ref[...]).astype(o_ref.dtype)


def _entry_norm(x, ln_g, ln_b, tm):
    n, d = x.shape
    return pl.pallas_call(
        _ln_kernel,
        grid=(n // tm,),
        in_specs=[pl.BlockSpec((tm, d), lambda i: (i, 0)),
                  pl.BlockSpec((1, d), lambda i: (0, 0)), pl.BlockSpec((1, d), lambda i: (0, 0))],
        out_specs=pl.BlockSpec((tm, d), lambda i: (i, 0)),
        out_shape=jax.ShapeDtypeStruct((n, d), BF16),
        compiler_params=_params(("parallel",)),
        name="entry_norm",
    )(x, ln_g, ln_b)


def _gates_kernel(x_ref, w_ref, o_ref):
    y = jnp.dot(x_ref[...], w_ref[...], preferred_element_type=F32)
    o_ref[...] = _sigmoid(y).astype(o_ref.dtype)


def _rwkv_proj_kernel(x_ref, w_ref, o_ref):
    o_ref[...] = jnp.dot(x_ref[...], w_ref[...], preferred_element_type=F32)


def _proj_matmul(body, xn, w, out_dtype, tm, tn, name):
    n, d = xn.shape
    ncol = w.shape[1]
    return pl.pallas_call(
        body,
        grid=(n // tm, ncol // tn),
        in_specs=[pl.BlockSpec((tm, d), lambda i, j: (i, 0)),
                  pl.BlockSpec((d, tn), lambda i, j: (0, j))],
        out_specs=pl.BlockSpec((tm, tn), lambda i, j: (i, j)),
        out_shape=jax.ShapeDtypeStruct((n, ncol), out_dtype),
        compiler_params=_params(("parallel", "parallel")),
        name=name,
    )(xn, w)


def _fox_proj_kernel(xn_ref, w_ref, wf_ref, bf_ref, qn_ref, kn_ref,
                     qt_ref, kf_ref, kb_ref, vf_ref, vt_ref, lf_ref, *, n_heads):
    j = pl.program_id(0)
    y = jnp.dot(xn_ref[...], w_ref[...], preferred_element_type=F32)

    def rms(yh, gain):
        ms = jnp.mean(yh * yh, axis=-1, keepdims=True)
        return yh * lax.rsqrt(ms + QK_EPS) * gain

    @pl.when(j == 0)
    def _():
        scale = FOX_HEAD_DIM ** -0.5 * LOG2_E
        for h in range(n_heads):
            yh = y[:, h * FOX_HEAD_DIM:(h + 1) * FOX_HEAD_DIM]
            qt_ref[h] = (rms(yh, qn_ref[...]) * scale).astype(BF16).T
        fl = jnp.dot(xn_ref[...], wf_ref[...], preferred_element_type=F32)
        lf_ref[...] = _log_sigmoid(fl + bf_ref[...])

    @pl.when(j == 1)
    def _():
        for h in range(n_heads):
            sl = slice(h * FOX_HEAD_DIM, (h + 1) * FOX_HEAD_DIM)
            kh = rms(y[:, sl], kn_ref[...])
            kf_ref[:, sl] = kh
            kb_ref[h] = kh.astype(BF16)

    @pl.when(j == 2)
    def _():
        vf_ref[...] = y
        for h in range(n_heads):
            vt_ref[h] = y[:, h * FOX_HEAD_DIM:(h + 1) * FOX_HEAD_DIM].astype(BF16).T


def _fox_proj(xn, w_qkv, w_f, b_f, q_norm, k_norm, tm):
    bsz, t, d = xn.shape
    width = w_qkv.shape[1] // 3
    n_heads = width // FOX_HEAD_DIM
    nt = t // tm

    def rows_of(section):
        def where(j, b, i):
            mine, early = j == section, j < section
            return (jnp.where(mine, b, jnp.where(early, 0, bsz - 1)),
                    jnp.where(mine, i, jnp.where(early, 0, nt - 1)))
        return where

    def hm(section):
        at = rows_of(section)
        return pl.BlockSpec((None, n_heads, tm, FOX_HEAD_DIM), lambda j, b, i: (at(j, b, i)[0], 0, at(j, b, i)[1], 0))

    def tr(section):
        at = rows_of(section)
        return pl.BlockSpec((None, n_heads, None, FOX_HEAD_DIM, tm),
                            lambda j, b, i: (at(j, b, i)[0], 0, at(j, b, i)[1], 0, 0))

    def tok(section, c):
        at = rows_of(section)
        return pl.BlockSpec((None, tm, c), lambda j, b, i: (at(j, b, i)[0], at(j, b, i)[1], 0))

    const = lambda r, c: pl.BlockSpec((r, c), lambda j, b, i: (0, 0))
    tr_shape = jax.ShapeDtypeStruct((bsz, n_heads, nt, FOX_HEAD_DIM, tm), BF16)
    tok_shape = jax.ShapeDtypeStruct((bsz, t, width), F32)
    return pl.pallas_call(
        functools.partial(_fox_proj_kernel, n_heads=n_heads),
        grid=(3, bsz, nt),
        in_specs=[
            pl.BlockSpec((None, tm, d), lambda j, b, i: (b, i, 0)),
            pl.BlockSpec((d, width), lambda j, b, i: (0, j)),
            const(d, LANES), const(1, LANES),
            const(1, FOX_HEAD_DIM), const(1, FOX_HEAD_DIM),
        ],
        out_specs=[tr(0), tok(1, width), hm(1), tok(2, width), tr(2), tok(0, LANES)],
        out_shape=[tr_shape, tok_shape, jax.ShapeDtypeStruct((bsz, n_heads, t, FOX_HEAD_DIM), BF16),
                   tok_shape, tr_shape, jax.ShapeDtypeStruct((bsz, t, LANES), F32)],
        compiler_params=_params(("arbitrary", "arbitrary", "arbitrary")),
        name="proj_fox",
    )(xn, w_qkv, w_f, b_f, q_norm, k_norm)


def _fgate_bias_kernel(lf_ref, o_ref, carry, *, length, n_heads):
    t = pl.program_id(1)
    tt = lf_ref.shape[0]

    @pl.when(t == 0)
    def _():
        carry[...] = jnp.zeros_like(carry)

    ri = lax.broadcasted_iota(jnp.int32, (tt, tt), 0)
    rj = lax.broadcasted_iota(jnp.int32, (tt, tt), 1)
    tri = jnp.where(rj <= ri, 1.0, 0.0).astype(BF16)
    csum = _dot_exact_lhs(tri, lf_ref[...]) + carry[...]
    carry[...] = csum[tt - 1:tt, :]
    pos = t * tt + lax.broadcasted_iota(jnp.int32, csum.shape, 0)
    neg = jnp.where(pos < length, -LOG2_E * csum, NEG_INF)
    pieces = jnp.concatenate(_split3(neg), axis=1)
    sr = lax.broadcasted_iota(jnp.int32, (3 * LANES, LANES), 0)
    sc = lax.broadcasted_iota(jnp.int32, (3 * LANES, LANES), 1)
    for h in range(n_heads):
        sel = jnp.where(sr % LANES == h, jnp.where(sr // LANES == sc, 1.0, 0.0), 0.0).astype(BF16)
        o_ref[h] = jnp.dot(pieces, sel, preferred_element_type=F32).astype(BF16)


def _fgate_bias(lf, length, n_heads, tt):
    bsz, lp, _ = lf.shape
    return pl.pallas_call(
        functools.partial(_fgate_bias_kernel, length=length, n_heads=n_heads),
        grid=(bsz, lp // tt),
        in_specs=[pl.BlockSpec((None, tt, LANES), lambda b, t: (b, t, 0))],
        out_specs=pl.BlockSpec((None, n_heads, tt, LANES), lambda b, t: (b, 0, t, 0)),
        out_shape=jax.ShapeDtypeStruct((bsz, n_heads, lp, LANES), BF16),
        scratch_shapes=[pltpu.VMEM((1, LANES), F32)],
        compiler_params=_params(("parallel", "arbitrary")),
        name="fgate_bias",
    )(lf)


N_BIAS_PIECES = 3


ATTN_HEADS_PER_STEP = 2


def _attn_kernel(qt_ref, k_ref, nf_ref, vt_ref, o_ref, *, tq, tk, past):
    i = pl.program_id(2)
    n_heads = qt_ref.shape[0]
    tw = min(tq, MXU_DIM)
    chains = [(h, c) for h in range(n_heads) for c in range(tq // tw)]
    ones_rows = jnp.where(lax.broadcasted_iota(jnp.int32, (LANES, tw), 0) < N_BIAS_PIECES, 1.0, 0.0)
    qa = [jnp.concatenate([qt_ref[h, :, c * tw:(c + 1) * tw], ones_rows.astype(BF16)], axis=0)
          for h, c in chains]
    n_full = (past + i * tq) // tk

    def update(carry, s, vt):
        m, l, acc = carry
        m_new = jnp.maximum(m, jnp.max(s, axis=0, keepdims=True))
        p = jnp.exp2(s - m_new)
        alpha = jnp.exp2(m - m_new)
        l = alpha * l + jnp.sum(p, axis=0, keepdims=True)
        acc = alpha * acc + jnp.dot(vt, p.astype(BF16), preferred_element_type=F32)
        return m_new, l, acc

    def scores(j, causal=False):
        start = pl.multiple_of(j * tk, tk)
        ka = [jnp.concatenate([k_ref[h, pl.ds(start, tk), :], nf_ref[h, pl.ds(start, tk), :]], axis=1)
              for h in range(n_heads)]
        ss = [jnp.dot(ka[h], qa[n], preferred_element_type=F32) for n, (h, c) in enumerate(chains)]
        if causal:
            key = lax.broadcasted_iota(jnp.int32, (tk, tw), 0)
            qry = lax.broadcasted_iota(jnp.int32, (tk, tw), 1)
            ss = [jnp.where(key <= qry + c * tw, s, NEG_INF) for s, (h, c) in zip(ss, chains)]
        return ss

    def absorb(carry, ss, j):
        return tuple(update(carry[n], ss[n], vt_ref[h, j]) for n, (h, c) in enumerate(chains))

    def two_blocks(pair, carry):
        sa, sb = scores(2 * pair), scores(2 * pair + 1)
        return absorb(absorb(carry, sa, 2 * pair), sb, 2 * pair + 1)

    init = (jnp.full((1, tw), NEG_INF, F32), jnp.zeros((1, tw), F32), jnp.zeros((FOX_HEAD_DIM, tw), F32))
    carry = lax.fori_loop(0, n_full // 2, two_blocks, tuple(init for _ in chains))
    last_full = jnp.maximum(n_full - 1, 0)
    carry = lax.cond(n_full % 2 == 1, lambda c: absorb(c, scores(last_full), last_full), lambda c: c, carry)
    carry = absorb(carry, scores(n_full, causal=True), n_full)
    for n, (h, c) in enumerate(chains):
        _, l, acc = carry[n]
        o_ref[c * tw:(c + 1) * tw, h * FOX_HEAD_DIM:(h + 1) * FOX_HEAD_DIM] = (acc / l).T.astype(o_ref.dtype)


def _fox_attention(qt, k, nf, vt, past, tk):
    bsz, n_heads, nq, dh, tq = qt.shape
    lp = k.shape[2]
    hps = ATTN_HEADS_PER_STEP
    assert past % tk == 0 and (tq == tk or nq == 1) and tq <= tk and lp % tk == 0 and n_heads % hps == 0
    whole = lambda a: pl.BlockSpec((None, hps) + a.shape[2:], lambda b, h, i: (b, h) + (0,) * (a.ndim - 2))
    return pl.pallas_call(
        functools.partial(_attn_kernel, tq=tq, tk=tk, past=past),
        grid=(bsz, n_heads // hps, nq),
        in_specs=[pl.BlockSpec((None, hps, None, dh, tq), lambda b, h, i: (b, h, i, 0, 0)),
                  whole(k), whole(nf), whole(vt)],
        out_specs=pl.BlockSpec((None, tq, hps * dh), lambda b, h, i: (b, i, h)),
        out_shape=jax.ShapeDtypeStruct((bsz, nq * tq, n_heads * dh), BF16),
        compiler_params=_params(("parallel", "parallel", "arbitrary")),
        name="fox_attn",
    )(qt, k, nf, vt)


def _rwkv_kernel(p_ref, shift_ref, s0_ref, mu_ref, w0_ref, a0_ref, kk_ref, ka_ref, rk_ref,
                 gng_ref, gnb_ref, w2_ref, a2_ref, g2_ref, o_ref, sout_ref,
                 state, prev_row, *, tc, t_valid):
    c_len = RWKV_CHUNK
    t = pl.program_id(1)
    width = o_ref.shape[-1]
    n_groups = width // MXU_DIM

    @pl.when(t == 0)
    def _():
        state[...] = s0_ref[...]
        prev_row[...] = shift_ref[...]

    p = p_ref[...]
    ridx = lax.broadcasted_iota(jnp.int32, p.shape, 0)
    prev = jnp.where(ridx == 0, jnp.broadcast_to(prev_row[...], p.shape), pltpu.roll(p, 1, 0))
    prev_row[...] = p[tc - 1:tc, :]
    xs = p + (prev - p) * mu_ref[...]
    r = xs[:, 0:width]
    k = xs[:, width:2 * width]
    v = xs[:, 2 * width:3 * width]
    lora_in = xs[:, 3 * width:3 * width + LANES]
    gate_in = xs[:, 3 * width + LANES:]
    zw = _dot(jnp.tanh(lora_in), w2_ref[...])
    za = _dot(lora_in, a2_ref[...])
    g = _dot(_sigmoid(gate_in), g2_ref[...])
    lw = -DECAY_SCALE * _sigmoid(w0_ref[...] + zw)
    iclr = _sigmoid(a0_ref[...] + za)

    hr = lax.broadcasted_iota(jnp.int32, (MXU_DIM, MXU_DIM), 0) // RWKV_HEAD_DIM
    hc = lax.broadcasted_iota(jnp.int32, (MXU_DIM, MXU_DIM), 1) // RWKV_HEAD_DIM
    same_head = hr == hc
    bd_f32 = jnp.where(same_head, 1.0, 0.0)
    ones_bd = bd_f32.astype(BF16)

    def head_sum(x, split=True):
        hi = x.astype(BF16)
        lo = (x - hi.astype(F32)).astype(BF16) if split else None
        parts = []
        for gi in range(n_groups):
            ls = slice(gi * MXU_DIM, (gi + 1) * MXU_DIM)
            part = jnp.dot(hi[:, ls], ones_bd, preferred_element_type=F32)
            if split:
                part = part + jnp.dot(lo[:, ls], ones_bd, preferred_element_type=F32)
            parts.append(part)
        return jnp.concatenate(parts, axis=-1)

    kk = k * kk_ref[...]
    kk = kk * lax.rsqrt(jnp.maximum(head_sum(kk * kk), 1e-24))
    k = k * (1.0 + (iclr - 1.0) * ka_ref[...])
    if t_valid < tc:
        live = lax.broadcasted_iota(jnp.int32, (tc, width), 0) < t_valid
        lw = jnp.where(live, lw, 0.0)
        kk = jnp.where(live, kk, 0.0)
        k = jnp.where(live, k, 0.0)
        v = jnp.where(live, v, 0.0)

    ti = lax.broadcasted_iota(jnp.int32, (tc, tc), 0)
    tj = lax.broadcasted_iota(jnp.int32, (tc, tc), 1)
    tri = jnp.where(ti // c_len == tj // c_len, jnp.where(tj <= ti, 1.0, 0.0), 0.0).astype(BF16)
    gcum = _dot_exact_lhs(tri, lw)
    e_in = jnp.exp(gcum)
    e_inv = jnp.exp(-gcum)
    at_all = -kk * jnp.exp(gcum - lw)
    rt_all = r * e_in
    bt_all = kk * iclr * e_inv
    kt_all = k * e_inv
    bonus = head_sum(r * k * rk_ref[...]) * v

    row = lax.broadcasted_iota(jnp.int32, (c_len, MXU_DIM), 0)
    lane = lax.broadcasted_iota(jnp.int32, (c_len, MXU_DIM), 1) % c_len
    strict = jnp.where(lane < row, 1.0, 0.0)
    incl = jnp.where(lane <= row, 1.0, 0.0)
    eye_w = jnp.where(lane == row, 1.0, 0.0)

    def bd(x):
        return jnp.concatenate([x.astype(BF16)] * HEADS_PER_GROUP, axis=0) * ones_bd

    n_sq = c_len.bit_length() - 1
    n_chunks = tc // c_len
    units = [(ci, gi) for ci in range(n_chunks) for gi in range(n_groups)]

    def cut(x, u):
        ci, gi = u
        return x[ci * c_len:(ci + 1) * c_len, gi * MXU_DIM:(gi + 1) * MXU_DIM]

    at = [cut(at_all, u) for u in units]
    rt = [cut(rt_all, u) for u in units]
    bt = [cut(bt_all, u) for u in units]
    kt = [cut(kt_all, u) for u in units]
    vv = [cut(v, u) for u in units]
    ar = [jnp.concatenate([a, r_], axis=0) for a, r_ in zip(at, rt)]
    ab = [_dot_nt(x, bd(b_)) for x, b_ in zip(ar, bt)]
    ak = [_dot_nt(x, bd(k_)) for x, k_ in zip(ar, kt)]
    pw = [x[:c_len] * strict for x in ab]
    a_rb = [x[c_len:] * incl for x in ab]
    a_ak = [x[:c_len] * strict for x in ak]
    a_rk = [x[c_len:] * incl for x in ak]
    tm = [eye_w + x for x in pw]
    pw = [_dot(x, bd(x)) for x in pw]
    for js in range(1, n_sq):
        if js < n_sq - 1:
            tp = [_dot(jnp.concatenate([t_, x], axis=0), bd(x)) for t_, x in zip(tm, pw)]
            tm = [t_ + y_[:c_len] for t_, y_ in zip(tm, tp)]
            pw = [y_[c_len:] for y_ in tp]
        else:
            tm = [t_ + _dot(t_, bd(x)) for t_, x in zip(tm, pw)]
    bdv = [bd(x) for x in vv]
    a_hat = [_dot(t_, bd(a)) for t_, a in zip(tm, at)]
    av = [_dot(x, b_) for x, b_ in zip(a_ak, bdv)]
    u_hat = [_dot(t_, bd(x)) for t_, x in zip(tm, av)]
    r_hat = [r_ + _dot(x, bd(a)) for r_, x, a in zip(rt, a_rb, a_hat)]
    y_hat = [_dot(x, bd(uh)) + _dot(z, b_) for x, uh, z, b_ in zip(a_rb, u_hat, a_rk, bdv)]
    lhs = [jnp.concatenate([a, r_], axis=0) for a, r_ in zip(a_hat, r_hat)]

    st = [state[gi] for gi in range(n_groups)]
    y_rows = []
    for ci in range(n_chunks):
        gend = jnp.exp(gcum[(ci + 1) * c_len - 1:(ci + 1) * c_len, :])
        us = [ci * n_groups + gi for gi in range(n_groups)]
        ge = [gend[:, gi * MXU_DIM:(gi + 1) * MXU_DIM] for gi in range(n_groups)]
        uy = [_dot_nt(lhs[u], st[gi]) for gi, u in enumerate(us)]
        uu = [uy[gi][:c_len] + u_hat[u] for gi, u in enumerate(us)]
        y_rows.append(jnp.concatenate([uy[gi][c_len:] + y_hat[u] for gi, u in enumerate(us)], axis=1))
        uv_t = [jnp.concatenate([uu[gi], vv[u]], axis=0).T for gi, u in enumerate(us)]
        bk = [jnp.concatenate([bt[u] * ge[gi], kt[u] * ge[gi]], axis=0) for gi, u in enumerate(us)]
        st = [st[gi] * ge[gi] + _dot(uv_t[gi], bk[gi]) * bd_f32 for gi in range(n_groups)]
    for gi in range(n_groups):
        state[gi] = st[gi]

    y = jnp.concatenate(y_rows, axis=0)
    inv_n = 1.0 / RWKV_HEAD_DIM
    mean = head_sum(y, split=False) * inv_n
    yc = y - mean
    var = head_sum(yc * yc, split=False) * inv_n
    yn = yc * lax.rsqrt(var + RWKV_GN_EPS) * gng_ref[...] + gnb_ref[...]
    o_ref[...] = ((yn + bonus) * g).astype(o_ref.dtype)

    @pl.when(t == pl.num_programs(1) - 1)
    def _():
        sout_ref[...] = state[...]


def _rwkv_mix(p, shift_prev, s0_bd, vecs, w2p, a2p, g2p, tc, t_valid):
    bsz, tp, cols = p.shape
    width = w2p.shape[1]
    n_groups = width // MXU_DIM
    vec_specs = [pl.BlockSpec((1, a.shape[1]), lambda b, t: (0, 0)) for a in vecs]
    mat = lambda a: pl.BlockSpec(a.shape, lambda b, t: (0, 0))
    st_spec = pl.BlockSpec((None, n_groups, MXU_DIM, MXU_DIM), lambda b, t: (b, 0, 0, 0))
    return pl.pallas_call(
        functools.partial(_rwkv_kernel, tc=tc, t_valid=t_valid),
        grid=(bsz, tp // tc),
        in_specs=[pl.BlockSpec((None, tc, cols), lambda b, t: (b, t, 0)),
                  pl.BlockSpec((None, 1, cols), lambda b, t: (b, 0, 0)),
                  st_spec] + vec_specs + [mat(w2p), mat(a2p), mat(g2p)],
        out_specs=[pl.BlockSpec((None, tc, width), lambda b, t: (b, t, 0)), st_spec],
        out_shape=[jax.ShapeDtypeStruct((bsz, tp, width), BF16),
                   jax.ShapeDtypeStruct(s0_bd.shape, F32)],
        scratch_shapes=[pltpu.VMEM((n_groups, MXU_DIM, MXU_DIM), F32), pltpu.VMEM((1, cols), F32)],
        compiler_params=_params(("parallel", "arbitrary")),
        name="rwkv_mix",
    )(p, shift_prev, s0_bd, *vecs, w2p, a2p, g2p)


def _merge_kernel(x_ref, oa_ref, ob_ref, gt_ref, wa_ref, wb_ref, wo_ref, lig_ref, lib_ref,
                  l1g_ref, l1b_ref, wr_ref, rb_ref, x1_ref, ids_ref, wts_ref, *, alpha):
    d = x_ref.shape[-1]
    xn = _layer_norm(x_ref[...], lig_ref[...], lib_ref[...])
    ya = jnp.dot(oa_ref[...], wa_ref[...], preferred_element_type=F32)
    yb = jnp.dot(ob_ref[...], wb_ref[...], preferred_element_type=F32)
    merged = gt_ref[:, :d].astype(F32) * ya + gt_ref[:, d:].astype(F32) * yb
    out = jnp.dot(merged.astype(BF16), wo_ref[...], preferred_element_type=F32)
    x1 = _layer_norm(alpha * xn + out, l1g_ref[...], l1b_ref[...])
    x1_ref[...] = x1

    h1, h2, _ = _split3(x1)
    r1 = jnp.dot(h1, wr_ref[...], preferred_element_type=F32)
    r2 = jnp.dot(h2, wr_ref[...], preferred_element_type=F32)
    logits = (r1[:, :LANES] + (r1[:, LANES:] + r2[:, :LANES]) + r2[:, LANES:]) + rb_ref[...]
    lane = lax.broadcasted_iota(jnp.int32, logits.shape, 1).astype(F32)
    big = 1e9

    def first_max(vals):
        mx = jnp.max(vals, axis=-1, keepdims=True)
        idx = jnp.min(jnp.where(vals == mx, lane, big), axis=-1, keepdims=True)
        return mx, idx

    is_grp = lane < N_GROUPS
    gmax, grp = first_max(jnp.where(is_grp, logits, NEG_INF))
    p_grp = 1.0 / jnp.sum(jnp.where(is_grp, jnp.exp(logits - gmax), 0.0), axis=-1, keepdims=True)
    lo = N_GROUPS + grp * EXPERTS_PER_GROUP
    elog = jnp.where(lane >= lo, jnp.where(lane < lo + EXPERTS_PER_GROUP, logits, NEG_INF), NEG_INF)
    v1, i1 = first_max(elog)
    v2, i2 = first_max(jnp.where(lane == i1, NEG_INF, elog))
    e2 = jnp.exp(v2 - v1)
    w1 = p_grp / (1.0 + e2)
    w2 = p_grp * e2 / (1.0 + e2)
    ids = jnp.where(lane == 0, i1 - N_GROUPS, jnp.where(lane == 1, i2 - N_GROUPS, 0.0))
    ids_ref[...] = ids.astype(jnp.int32)
    wts_ref[...] = jnp.where(lane == 0, w1, jnp.where(lane == 1, w2, 0.0))


def _merge_out(x, oa, ob, gates, wa, wb, wo, lig, lib, l1g, l1b, w_router, rbias, alpha, tm):
    n, d = x.shape
    half = oa.shape[1]
    row = lambda c: pl.BlockSpec((tm, c), lambda i: (i, 0))
    const = lambda a: pl.BlockSpec(a.shape, lambda i: (0, 0), pipeline_mode=pl.Buffered(1))
    return pl.pallas_call(
        functools.partial(_merge_kernel, alpha=alpha),
        grid=(n // tm,),
        in_specs=[row(d), row(half), row(half), row(2 * d), const(wa), const(wb), const(wo),
                  const(lig), const(lib), const(l1g), const(l1b), const(w_router), const(rbias)],
        out_specs=[row(d), row(LANES), row(LANES)],
        out_shape=[jax.ShapeDtypeStruct((n, d), F32), jax.ShapeDtypeStruct((n, LANES), jnp.int32),
                   jax.ShapeDtypeStruct((n, LANES), F32)],
        compiler_params=_params(("parallel",)),
        name="merge_out",
    )(x, oa, ob, gates, wa, wb, wo, lig, lib, l1g, l1b, w_router, rbias)


DMA_ISSUE_UNROLL = 8
MOE_BLOCK = 512


def _wait_rows(make_row_copy, count):
    lax.fori_loop(0, count, lambda q, c: (make_row_copy(0).wait(), c)[1], 0, unroll=DMA_ISSUE_UNROLL)


def _dispatch_kernel(zblk_ref, dest_ref, *rest, steps):
    xs_hbm, zeros, sem, zsem = rest[-4:]
    x_refs = rest[:-4]
    i = pl.program_id(0)
    blk = zeros.shape[0]

    def zero_block(z):
        start = pl.multiple_of(zblk_ref[z] * blk, blk)
        return pltpu.make_async_copy(zeros, xs_hbm.at[pl.ds(start, blk), :], zsem)

    @pl.when(i == 0)
    def _():
        zeros[...] = jnp.zeros_like(zeros)
        for z in range(zblk_ref.shape[0]):
            pl.when(zblk_ref[z] >= 0)(lambda z=z: zero_block(z).start())
        for z in range(zblk_ref.shape[0]):
            pl.when(zblk_ref[z] >= 0)(lambda z=z: zero_block(z).wait())

    def scatter_rows(x_ref):
        tm = x_ref.shape[0]

        def to_slot(r, kslot):
            return pltpu.make_async_copy(x_ref.at[pl.ds(r, 1), :],
                                         xs_hbm.at[pl.ds(dest_ref[0, TOP_K * r + kslot], 1), :], sem)

        def body(r, c):
            for kslot in range(TOP_K):
                to_slot(r, kslot).start()
            return c
        lax.fori_loop(0, tm, body, 0, unroll=DMA_ISSUE_UNROLL)
        _wait_rows(lambda q: to_slot(0, 0), TOP_K * tm)

    for g, x_ref in enumerate(x_refs):
        in_group = jnp.logical_and(i >= steps[g], i < steps[g + 1])
        pl.when(in_group)(functools.partial(scatter_rows, x_ref))


def _moe_dispatch(x1s, tiles, zero_blocks, dest_steps, n_rows, blk):
    d = x1s[0].shape[1]
    steps = [0]
    for x1, tm in zip(x1s, tiles):
        steps.append(steps[-1] + x1.shape[0] // tm)

    def x_spec(g, tm):
        last = steps[g + 1] - steps[g] - 1
        return pl.BlockSpec((tm, d), lambda i, zb: (jnp.clip(i - steps[g], 0, last), 0))

    grid_spec = pltpu.PrefetchScalarGridSpec(
        num_scalar_prefetch=1,
        grid=(steps[-1],),
        in_specs=[pl.BlockSpec((None, 1, dest_steps.shape[-1]), lambda i, zb: (i, 0, 0),
                               memory_space=pltpu.SMEM)] + [x_spec(g, tm) for g, tm in enumerate(tiles)],
        out_specs=pl.BlockSpec(memory_space=pl.ANY),
        scratch_shapes=[pltpu.VMEM((blk, d), F32), pltpu.SemaphoreType.DMA(()), pltpu.SemaphoreType.DMA(())],
    )
    return pl.pallas_call(
        functools.partial(_dispatch_kernel, steps=tuple(steps)),
        grid_spec=grid_spec,
        out_shape=jax.ShapeDtypeStruct((n_rows, d), F32),
        compiler_params=_params(("arbitrary",)),
        name="moe_dispatch",
    )(zero_blocks, dest_steps, *x1s)


def _experts_kernel(blk_e_ref, xblk_ref, nvalid_ref, xs_ref, wg_ref, wu_ref, wd_ref, ys_ref):
    j = pl.program_id(0)

    @pl.when(nvalid_ref[j] > 0)
    def _():
        xv = xs_ref[...].astype(BF16)
        hg = jnp.dot(xv, wg_ref[...].astype(BF16), preferred_element_type=F32)
        hu = jnp.dot(xv, wu_ref[...].astype(BF16), preferred_element_type=F32)
        h = hg * _sigmoid(hg) * hu
        ys_ref[...] = jnp.dot(h.astype(BF16), wd_ref[...].astype(BF16), preferred_element_type=F32)

    @pl.when(nvalid_ref[j] == 0)
    def _():
        ys_ref[...] = jnp.zeros_like(ys_ref)


def _moe_experts(xs, blk_e, xblk, nvalid, w_gate, w_up, w_down, blk):
    d = xs.shape[1]
    nb = blk_e.shape[0]
    de = w_gate.shape[-1]
    grid_spec = pltpu.PrefetchScalarGridSpec(
        num_scalar_prefetch=3,
        grid=(nb,),
        in_specs=[
            pl.BlockSpec((blk, d), lambda j, be, xb, nv: (xb[j], 0)),
            pl.BlockSpec((None, d, de), lambda j, be, xb, nv: (be[j], 0, 0)),
            pl.BlockSpec((None, d, de), lambda j, be, xb, nv: (be[j], 0, 0)),
            pl.BlockSpec((None, de, d), lambda j, be, xb, nv: (be[j], 0, 0)),
        ],
        out_specs=pl.BlockSpec((blk, d), lambda j, be, xb, nv: (j, 0)),
    )
    return pl.pallas_call(
        _experts_kernel,
        grid_spec=grid_spec,
        out_shape=jax.ShapeDtypeStruct((nb * blk, d), F32),
        compiler_params=_params(("arbitrary",)),
        name="moe_experts",
    )(blk_e, xblk, nvalid, xs, w_gate, w_up, w_down)


def _combine_kernel(cur_ref, nxt_ref, x1_ref, wts_ref, g_ref, b_ref, ys_hbm, o_ref, ybuf, sem, *, alpha):
    i = pl.program_id(0)
    tm = x1_ref.shape[0]
    slot = i % 2

    def row(tbl_ref, s, r, kslot):
        return pltpu.make_async_copy(ys_hbm.at[pl.ds(tbl_ref[0, TOP_K * r + kslot], 1), :],
                                     ybuf.at[s, pl.ds(kslot * tm + r, 1), :], sem.at[s])

    def fetch(tbl_ref, s):
        def body(r, c):
            for kslot in range(TOP_K):
                row(tbl_ref, s, r, kslot).start()
            return c
        lax.fori_loop(0, tm, body, 0, unroll=DMA_ISSUE_UNROLL)

    @pl.when(i == 0)
    def _():
        fetch(cur_ref, 0)

    @pl.when(i + 1 < pl.num_programs(0))
    def _():
        fetch(nxt_ref, 1 - slot)

    _wait_rows(lambda q: row(cur_ref, slot, 0, 0), TOP_K * tm)
    w = wts_ref[...]
    y = w[:, 0:1] * ybuf[slot, 0:tm, :] + w[:, 1:2] * ybuf[slot, tm:2 * tm, :]
    o_ref[...] = _layer_norm(alpha * x1_ref[...] + y, g_ref[...], b_ref[...])


def _moe_combine(x1, ys, dest_tiles, wts, g, b, alpha, tm):
    n, d = x1.shape
    last = n // tm - 1
    tbl = lambda f: pl.BlockSpec((None, 1, TOP_K * tm), lambda i: (f(i), 0, 0), memory_space=pltpu.SMEM)
    return pl.pallas_call(
        functools.partial(_combine_kernel, alpha=alpha),
        grid=(n // tm,),
        in_specs=[tbl(lambda i: i), tbl(lambda i: jnp.minimum(i + 1, last)),
                  pl.BlockSpec((tm, d), lambda i: (i, 0)),
                  pl.BlockSpec((tm, LANES), lambda i: (i, 0)),
                  pl.BlockSpec((1, d), lambda i: (0, 0)), pl.BlockSpec((1, d), lambda i: (0, 0)),
                  pl.BlockSpec(memory_space=pl.ANY)],
        out_specs=pl.BlockSpec((tm, d), lambda i: (i, 0)),
        out_shape=jax.ShapeDtypeStruct((n, d), F32),
        scratch_shapes=[pltpu.VMEM((2, TOP_K * tm, d), F32), pltpu.SemaphoreType.DMA((2,))],
        compiler_params=_params(("arbitrary",)),
        name="moe_combine",
    )(dest_tiles, dest_tiles, x1, wts, g, b, ys)


def _dispatch_tables(eid, blk):
    n = eid.shape[0]
    nk = n * TOP_K
    nb = -(-(nk + N_EXPERTS * (blk - 1)) // blk)
    experts = jnp.arange(N_EXPERTS, dtype=jnp.int32)
    chunk = LANES
    e2 = eid.reshape(nk // chunk, chunk)
    onehot = (e2[:, :, None] == experts).astype(jnp.int32)
    chunk_counts = onehot.sum(axis=1)
    chunk_base = jnp.cumsum(chunk_counts, axis=0) - chunk_counts
    earlier = jnp.arange(chunk)[None, :] < jnp.arange(chunk)[:, None]
    in_chunk = ((e2[:, :, None] == e2[:, None, :]) & earlier[None]).sum(axis=2)
    counts = chunk_counts.sum(axis=0)
    nblk_e = (counts + blk - 1) // blk
    blk_end = jnp.cumsum(nblk_e)
    blk_start = blk_end - nblk_e
    base = chunk_base + (blk_start * blk)[None, :]
    dest = ((onehot * base[:, None, :]).sum(axis=2) + in_chunk).reshape(-1)
    n_used = blk_end[-1]
    tail = n_used + experts
    zero_blocks = jnp.concatenate([jnp.where(nblk_e > 0, blk_end - 1, -1), jnp.where(tail < nb, tail, -1)])
    bidx = jnp.arange(nb, dtype=jnp.int32)
    blk_e = jnp.minimum((blk_end[None, :] <= bidx[:, None]).sum(axis=1), N_EXPERTS - 1).astype(jnp.int32)
    mine = (blk_e[:, None] == experts[None, :]).astype(jnp.int32)
    left = (mine * (counts + blk_start * blk)[None, :]).sum(axis=1) - bidx * blk
    nvalid = jnp.clip(left, 0, blk).astype(jnp.int32)
    nvalid = jnp.where(bidx < n_used, nvalid, 0)
    xblk = jnp.minimum(bidx, n_used - 1).astype(jnp.int32)
    return dest.astype(jnp.int32), zero_blocks.astype(jnp.int32), blk_e, xblk, nvalid, nb * blk


def _pick(n, prefs):
    for p in prefs:
        if n % p == 0:
            return p
    return n


def _layer(x, past_k, past_v, past_logf, s0, shift_prev, wts, alpha):
    bsz, t, d = x.shape
    n = bsz * t
    xf = x.reshape(n, d)
    fox_width = wts['w_qkv'].shape[1] // 3
    n_heads = fox_width // FOX_HEAD_DIM
    rw_width = wts['w2p'].shape[1]
    rw_heads = rw_width // RWKV_HEAD_DIM
    rw_cols = wts['rwkv_cols']
    past = past_k.shape[1]

    xn = _entry_norm(xf, wts['ln_in_g'], wts['ln_in_b'], _pick(n, (512, 256)))
    tm = _pick(n, (PROJ_ROWS, 512, 256))
    gates = _proj_matmul(_gates_kernel, xn, wts['w_gates'], BF16, tm, 1024, "proj_gates")
    p_rw = _proj_matmul(_rwkv_proj_kernel, xn, wts['w_rwkv'], F32, tm, wts['w_rwkv'].shape[1] // 3, "proj_rwkv")
    t_fox = -(-t // LANES) * LANES
    tq = _pick(t_fox, (ATTN_BLOCK, 256, LANES))
    xn_fox = xn.reshape(bsz, t, d)
    if t_fox != t:
        xn_fox = jnp.pad(xn_fox, ((0, 0), (0, t_fox - t), (0, 0)))
    qt, k_f, k_b, v_f, vt, lf = _fox_proj(xn_fox, wts['w_qkv'], wts['w_f'], wts['b_f'], wts['q_norm'],
                                          wts['k_norm'], tq)
    k_f, v_f, logf = k_f[:, :t], v_f[:, :t], lf[:, :t, :n_heads]
    if past:
        tk = ATTN_BLOCK
        lpad = -(-(past + t) // tk) * tk
        grow = lambda a, ax: jnp.pad(a, [(0, lpad - a.shape[ax]) if i == ax else (0, 0) for i in range(a.ndim)])
        lf_past = jnp.pad(past_logf.astype(F32), ((0, 0), (0, 0), (0, LANES - n_heads)))
        lf_all = grow(jnp.concatenate([lf_past, lf[:, :t]], axis=1), 1)
        k_all = grow(jnp.concatenate([jnp.swapaxes(past_k, 1, 2).astype(BF16), k_b[:, :, :t]], axis=2), 2)
        vt_past = jnp.transpose(past_v, (0, 2, 3, 1)).astype(BF16)
        vt_all = grow(jnp.concatenate([vt_past, vt[:, :, 0, :, :t]], axis=3), 3)
        vt_all = jnp.swapaxes(vt_all.reshape(bsz, n_heads, FOX_HEAD_DIM, lpad // tk, tk), 2, 3)
    else:
        tk, lf_all, k_all, vt_all = tq, lf, k_b, vt
    nf = _fgate_bias(lf_all, past + t, n_heads, tk)
    o_a = _fox_attention(qt, k_all, nf, vt_all, past, tk)[:, :t]

    tp = -(-t // RWKV_CHUNK) * RWKV_CHUNK
    tc = _pick(tp, (256, 128, 64))
    p3 = p_rw.reshape(bsz, t, -1)
    shift_new = p3[:, t - 1:t, :rw_cols]
    if tp != t:
        p3 = jnp.pad(p3, ((0, 0), (0, tp - t), (0, 0)))
    shift_in = jnp.pad(shift_prev.astype(F32), ((0, 0), (0, 0), (0, p3.shape[-1] - rw_cols)))
    n_grp = rw_heads // HEADS_PER_GROUP
    eye = jnp.eye(HEADS_PER_GROUP, dtype=F32)
    s0_g = s0.astype(F32).reshape(bsz, n_grp, HEADS_PER_GROUP, RWKV_HEAD_DIM, RWKV_HEAD_DIM)
    s0_bd = jnp.einsum('bghvk,hj->bghvjk', s0_g, eye).reshape(bsz, n_grp, MXU_DIM, MXU_DIM)
    o_b, s_bd = _rwkv_mix(p3, shift_in, s0_bd, wts['rwkv_vecs'], wts['w2p'], wts['a2p'], wts['g2p'], tc, t)
    s_new = jnp.einsum('bghvjk,hj->bghvk',
                       s_bd.reshape(bsz, n_grp, HEADS_PER_GROUP, RWKV_HEAD_DIM, HEADS_PER_GROUP, RWKV_HEAD_DIM),
                       eye).reshape(bsz, rw_heads, RWKV_HEAD_DIM, RWKV_HEAD_DIM)
    o_b = o_b[:, :t].reshape(n, rw_width)

    routed = _merge_out(xf, o_a.reshape(n, fox_width), o_b, gates, wts['w_a'], wts['w_b'], wts['w_o'],
                        wts['ln_in_g'], wts['ln_in_b'], wts['ln1_g'], wts['ln1_b'],
                        wts['w_router'], wts['b_router'], alpha, _pick(n, (512, 256)))
    hd = (bsz, t, n_heads, FOX_HEAD_DIM)
    return routed, (k_f.reshape(hd), v_f.reshape(hd), logf, s_new, shift_new)


def _hier_moe(groups, wts, alpha):
    sizes = [x1.shape[0] for x1, _, _ in groups]
    n_all = sum(sizes)
    blk = MOE_BLOCK
    while blk > 8 and blk * N_EXPERTS > n_all * TOP_K:
        blk //= 2
    eid = jnp.concatenate([ids[:, :TOP_K] for _, ids, _ in groups], axis=0)
    dest, zero_blocks, blk_e, xblk, nvalid, n_rows = _dispatch_tables(eid, blk)
    offset, dests = 0, []
    for n in sizes:
        dests.append(dest[TOP_K * offset:TOP_K * (offset + n)])
        offset += n
    tiles = [_pick(n, (512, 256)) for n in sizes]
    width = TOP_K * max(tiles)
    dest_steps = jnp.concatenate([jnp.pad(dst.reshape(-1, 1, TOP_K * tm), ((0, 0), (0, 0), (0, width - TOP_K * tm)))
                                  for dst, tm in zip(dests, tiles)], axis=0)
    xs = _moe_dispatch([x1 for x1, _, _ in groups], tiles, zero_blocks, dest_steps, n_rows, blk)
    ys = _moe_experts(xs, blk_e, xblk, nvalid, wts['moe_w_gate'], wts['moe_w_up'], wts['moe_w_down'], blk)
    outs = []
    for (x1, _, rw), dst, n in zip(groups, dests, sizes):
        tm = _pick(n, (256, 128, 64, 32, 16, 8))
        outs.append(_moe_combine(x1, ys, dst.reshape(-1, 1, TOP_K * tm), rw, wts['ln2_g'], wts['ln2_b'],
                                 alpha, tm))
    return outs


def _prepare_weights(l, ln_in_g, ln_in_b, w_in, fox_b_f, fox_q_norm, fox_k_norm, rwkv_mu, rwkv_w0, rwkv_w2,
                     rwkv_a0, rwkv_a2, rwkv_g2, rwkv_k_k, rwkv_k_a, rwkv_r_k, rwkv_gn_g, rwkv_gn_b,
                     w_branch_a, w_branch_b, w_out, ln1_g, ln1_b, router_group_w, router_group_b,
                     router_expert_w, router_expert_b, moe_w_gate, moe_w_up, moe_w_down, ln2_g, ln2_b):
    d = w_in.shape[1]
    fox_width = w_branch_a.shape[1]
    rw_width = w_branch_b.shape[1]
    n_heads = fox_width // FOX_HEAD_DIM
    gate_cols = 2 * d
    fox_cols = 3 * fox_width + n_heads
    rw_cols = 3 * rw_width + RWKV_DECAY_RANK + RWKV_ICLR_RANK + RWKV_GATE_RANK
    row = lambda a: a.astype(F32).reshape(1, -1)
    w = w_in[l]
    w_fox = w[:, gate_cols:gate_cols + fox_cols]
    w_rw = w[:, gate_cols + fox_cols:]
    lora = RWKV_DECAY_RANK + RWKV_ICLR_RANK
    assert lora == LANES
    gate_pad = -(-RWKV_GATE_RANK // LANES) * LANES
    cols_pad = 3 * rw_width + lora + gate_pad
    pad_c = cols_pad - rw_cols
    zeros = lambda r: jnp.zeros((r, rw_width), F32)
    wr = jnp.concatenate([router_group_w[l], router_expert_w[l]], axis=1).astype(F32)
    wr = jnp.pad(wr, ((0, 0), (0, LANES - wr.shape[1])))
    rb = jnp.concatenate([router_group_b[l], router_expert_b[l]]).astype(F32)
    return {
        'ln_in_g': row(ln_in_g), 'ln_in_b': row(ln_in_b),
        'w_gates': w[:, :gate_cols].astype(BF16),
        'w_qkv': w_fox[:, :3 * fox_width].astype(BF16),
        'w_f': jnp.pad(w_fox[:, 3 * fox_width:], ((0, 0), (0, LANES - n_heads))).astype(BF16),
        'b_f': jnp.pad(row(fox_b_f[l]), ((0, 0), (0, LANES - n_heads))),
        'q_norm': row(fox_q_norm[l]), 'k_norm': row(fox_k_norm[l]),
        'w_rwkv': jnp.pad(w_rw, ((0, 0), (0, pad_c))).astype(BF16),
        'rwkv_cols': rw_cols,
        'rwkv_vecs': [jnp.pad(row(rwkv_mu[l]), ((0, 0), (0, pad_c))), row(rwkv_w0[l]), row(rwkv_a0[l]),
                      row(rwkv_k_k[l]), row(rwkv_k_a[l]), row(rwkv_r_k[l]), row(rwkv_gn_g[l]),
                      row(rwkv_gn_b[l])],
        'w2p': jnp.concatenate([rwkv_w2[l].astype(F32), zeros(RWKV_ICLR_RANK)]).astype(BF16),
        'a2p': jnp.concatenate([zeros(RWKV_DECAY_RANK), rwkv_a2[l].astype(F32)]).astype(BF16),
        'g2p': jnp.concatenate([rwkv_g2[l].astype(F32), zeros(gate_pad - RWKV_GATE_RANK)]).astype(BF16),
        'w_a': w_branch_a[l].astype(BF16), 'w_b': w_branch_b[l].astype(BF16), 'w_o': w_out[l].astype(BF16),
        'ln1_g': row(ln1_g[l]), 'ln1_b': row(ln1_b[l]),
        'w_router': jnp.concatenate(_split3(wr)[:2], axis=1),
        'b_router': jnp.pad(row(rb), ((0, 0), (0, LANES - rb.shape[0]))),
        'moe_w_gate': moe_w_gate[l], 'moe_w_up': moe_w_up[l], 'moe_w_down': moe_w_down[l],
        'ln2_g': row(ln2_g[l]), 'ln2_b': row(ln2_b[l]),
    }


def kernel(x_prompt, x_sample, cache_fox_k, cache_fox_v, cache_fox_logf, state_rwkv, state_rwkv_shift,
           ln_in_g, ln_in_b, w_in, fox_b_f, fox_q_norm, fox_k_norm, rwkv_mu, rwkv_w0, rwkv_w2,
           rwkv_a0, rwkv_a2, rwkv_g2, rwkv_k_k, rwkv_k_a, rwkv_r_k, rwkv_gn_g, rwkv_gn_b,
           w_branch_a, w_branch_b, w_out, ln1_g, ln1_b, router_group_w, router_group_b,
           router_expert_w, router_expert_b, moe_w_gate, moe_w_up, moe_w_down, ln2_g, ln2_b):
    depth = w_in.shape[0]
    assert depth == 1, "the entry LayerNorm is fused into the layer's projections: single-layer trunk only"
    alpha = (2.0 * depth) ** 0.25
    bp = x_prompt.shape[0]
    n_fox_heads = fox_b_f.shape[1]
    rw_heads, rw_dim = state_rwkv.shape[2], state_rwkv.shape[3]
    rw_cols = state_rwkv_shift.shape[-1]
    wts = _prepare_weights(0, ln_in_g, ln_in_b, w_in, fox_b_f, fox_q_norm, fox_k_norm, rwkv_mu, rwkv_w0,
                           rwkv_w2, rwkv_a0, rwkv_a2, rwkv_g2, rwkv_k_k, rwkv_k_a, rwkv_r_k, rwkv_gn_g,
                           rwkv_gn_b, w_branch_a, w_branch_b, w_out, ln1_g, ln1_b, router_group_w,
                           router_group_b, router_expert_w, router_expert_b, moe_w_gate, moe_w_up,
                           moe_w_down, ln2_g, ln2_b)
    routed_p, new_p = _layer(x_prompt, jnp.zeros((bp, 0, n_fox_heads, FOX_HEAD_DIM), F32),
                             jnp.zeros((bp, 0, n_fox_heads, FOX_HEAD_DIM), F32),
                             jnp.zeros((bp, 0, n_fox_heads), F32),
                             jnp.zeros((bp, rw_heads, rw_dim, rw_dim), F32),
                             jnp.zeros((bp, 1, rw_cols), F32), wts, alpha)
    routed_s, new_s = _layer(x_sample, cache_fox_k[0], cache_fox_v[0], cache_fox_logf[0], state_rwkv[0],
                             state_rwkv_shift[0], wts, alpha)
    yp, ys = _hier_moe([routed_p, routed_s], wts, alpha)
    return ((yp.reshape(x_prompt.shape), ys.reshape(x_sample.shape))
            + tuple(a[None] for a in new_p) + tuple(a[None] for a in new_s))
```

```python
import functools

import jax
import jax.numpy as jnp
from jax import lax
from jax.experimental import pallas as pl
from jax.experimental.pallas import tpu as pltpu

F32 = jnp.float32
BF16 = jnp.bfloat16

FOX_HEAD_DIM = 128
RWKV_HEAD_DIM = 64
RWKV_DECAY_RANK = 64
RWKV_ICLR_RANK = 64
RWKV_GATE_RANK = 160
RWKV_GN_EPS = 64e-5
N_GROUPS = 4
EXPERTS_PER_GROUP = 8
N_EXPERTS = N_GROUPS * EXPERTS_PER_GROUP
TOP_K = 2
LN_EPS = 1e-5
QK_EPS = 1e-6
NEG_INF = -1e30
LOG2_E = 1.4426950408889634
DECAY_SCALE = 0.6065306597126334

LANES = 128
MXU_DIM = 256
VMEM_LIMIT_BYTES = 56 * 1024 * 1024

PROJ_ROWS = 1024
ATTN_BLOCK = 512
RWKV_CHUNK = 64
HEADS_PER_GROUP = MXU_DIM // RWKV_HEAD_DIM


def _params(semantics):
    return pltpu.CompilerParams(dimension_semantics=semantics, vmem_limit_bytes=VMEM_LIMIT_BYTES)


def _dot(a, b):
    return jnp.dot(a.astype(BF16), b.astype(BF16), preferred_element_type=F32)


def _dot_nt(a, b):
    return lax.dot_general(a.astype(BF16), b.astype(BF16), (((1,), (1,)), ((), ())),
                           preferred_element_type=F32)


def _split3(x):
    h1 = x.astype(BF16)
    r1 = x - h1.astype(F32)
    h2 = r1.astype(BF16)
    h3 = (r1 - h2.astype(F32)).astype(BF16)
    return h1, h2, h3


def _dot_exact_rhs(x, m_bf16):
    h1, h2, h3 = _split3(x)
    d = lambda h: jnp.dot(h, m_bf16, preferred_element_type=F32)
    return d(h1) + d(h2) + d(h3)


def _dot_exact_lhs(m_bf16, x):
    h1, h2, h3 = _split3(x)
    d = lambda h: jnp.dot(m_bf16, h, preferred_element_type=F32)
    return d(h1) + d(h2) + d(h3)


def _layer_norm(x, g, b):
    mu = jnp.mean(x, axis=-1, keepdims=True)
    xc = x - mu
    var = jnp.mean(xc * xc, axis=-1, keepdims=True)
    return xc * lax.rsqrt(var + LN_EPS) * g + b


def _sigmoid(x):
    return 1.0 / (1.0 + jnp.exp(-x))


def _log_sigmoid(x):
    return jnp.minimum(x, 0.0) - jnp.log(1.0 + jnp.exp(-jnp.abs(x)))


def _ln_kernel(x_ref, g_ref, b_ref, o_ref):
    o_ref[...] = _layer_norm(x_ref[...], g_ref[...], b_ref[...]).astype(o_ref.dtype)


def _entry_norm(x, ln_g, ln_b, tm):
    n, d = x.shape
    return pl.pallas_call(
        _ln_kernel,
        grid=(n // tm,),
        in_specs=[pl.BlockSpec((tm, d), lambda i: (i, 0)),
                  pl.BlockSpec((1, d), lambda i: (0, 0)), pl.BlockSpec((1, d), lambda i: (0, 0))],
        out_specs=pl.BlockSpec((tm, d), lambda i: (i, 0)),
        out_shape=jax.ShapeDtypeStruct((n, d), BF16),
        compiler_params=_params(("parallel",)),
        name="entry_norm",
    )(x, ln_g, ln_b)


def _gates_kernel(x_ref, w_ref, o_ref):
    y = jnp.dot(x_ref[...], w_ref[...], preferred_element_type=F32)
    o_ref[...] = _sigmoid(y).astype(o_ref.dtype)


def _rwkv_proj_kernel(x_ref, w_ref, o_ref):
    o_ref[...] = jnp.dot(x_ref[...], w_ref[...], preferred_element_type=F32)


def _proj_matmul(body, xn, w, out_dtype, tm, tn, name):
    n, d = xn.shape
    ncol = w.shape[1]
    return pl.pallas_call(
        body,
        grid=(n // tm, ncol // tn),
        in_specs=[pl.BlockSpec((tm, d), lambda i, j: (i, 0)),
                  pl.BlockSpec((d, tn), lambda i, j: (0, j))],
        out_specs=pl.BlockSpec((tm, tn), lambda i, j: (i, j)),
        out_shape=jax.ShapeDtypeStruct((n, ncol), out_dtype),
        compiler_params=_params(("parallel", "parallel")),
        name=name,
    )(xn, w)


def _fox_proj_kernel(xn_ref, w_ref, wf_ref, bf_ref, qn_ref, kn_ref,
                     qt_ref, kf_ref, kb_ref, vf_ref, vt_ref, lf_ref, *, n_heads):
    j = pl.program_id(0)
    y = jnp.dot(xn_ref[...], w_ref[...], preferred_element_type=F32)

    def rms(yh, gain):
        ms = jnp.mean(yh * yh, axis=-1, keepdims=True)
        return yh * lax.rsqrt(ms + QK_EPS) * gain

    @pl.when(j == 0)
    def _():
        scale = FOX_HEAD_DIM ** -0.5 * LOG2_E
        for h in range(n_heads):
            yh = y[:, h * FOX_HEAD_DIM:(h + 1) * FOX_HEAD_DIM]
            qt_ref[h] = (rms(yh, qn_ref[...]) * scale).astype(BF16).T
        fl = jnp.dot(xn_ref[...], wf_ref[...], preferred_element_type=F32)
        lf_ref[...] = _log_sigmoid(fl + bf_ref[...])

    @pl.when(j == 1)
    def _():
        for h in range(n_heads):
            sl = slice(h * FOX_HEAD_DIM, (h + 1) * FOX_HEAD_DIM)
            kh = rms(y[:, sl], kn_ref[...])
            kf_ref[:, sl] = kh
            kb_ref[h] = kh.astype(BF16)

    @pl.when(j == 2)
    def _():
        vf_ref[...] = y
        for h in range(n_heads):
            vt_ref[h] = y[:, h * FOX_HEAD_DIM:(h + 1) * FOX_HEAD_DIM].astype(BF16).T


def _fox_proj(xn, w_qkv, w_f, b_f, q_norm, k_norm, tm):
    bsz, t, d = xn.shape
    width = w_qkv.shape[1] // 3
    n_heads = width // FOX_HEAD_DIM
    nt = t // tm

    def rows_of(section):
        def where(j, b, i):
            mine, early = j == section, j < section
            return (jnp.where(mine, b, jnp.where(early, 0, bsz - 1)),
                    jnp.where(mine, i, jnp.where(early, 0, nt - 1)))
        return where

    def hm(section):
        at = rows_of(section)
        return pl.BlockSpec((None, n_heads, tm, FOX_HEAD_DIM), lambda j, b, i: (at(j, b, i)[0], 0, at(j, b, i)[1], 0))

    def tr(section):
        at = rows_of(section)
        return pl.BlockSpec((None, n_heads, None, FOX_HEAD_DIM, tm),
                            lambda j, b, i: (at(j, b, i)[0], 0, at(j, b, i)[1], 0, 0))

    def tok(section, c):
        at = rows_of(section)
        return pl.BlockSpec((None, tm, c), lambda j, b, i: (at(j, b, i)[0], at(j, b, i)[1], 0))

    const = lambda r, c: pl.BlockSpec((r, c), lambda j, b, i: (0, 0))
    tr_shape = jax.ShapeDtypeStruct((bsz, n_heads, nt, FOX_HEAD_DIM, tm), BF16)
    tok_shape = jax.ShapeDtypeStruct((bsz, t, width), F32)
    return pl.pallas_call(
        functools.partial(_fox_proj_kernel, n_heads=n_heads),
        grid=(3, bsz, nt),
        in_specs=[
            pl.BlockSpec((None, tm, d), lambda j, b, i: (b, i, 0)),
            pl.BlockSpec((d, width), lambda j, b, i: (0, j)),
            const(d, LANES), const(1, LANES),
            const(1, FOX_HEAD_DIM), const(1, FOX_HEAD_DIM),
        ],
        out_specs=[tr(0), tok(1, width), hm(1), tok(2, width), tr(2), tok(0, LANES)],
        out_shape=[tr_shape, tok_shape, jax.ShapeDtypeStruct((bsz, n_heads, t, FOX_HEAD_DIM), BF16),
                   tok_shape, tr_shape, jax.ShapeDtypeStruct((bsz, t, LANES), F32)],
        compiler_params=_params(("arbitrary", "arbitrary", "arbitrary")),
        name="proj_fox",
    )(xn, w_qkv, w_f, b_f, q_norm, k_norm)


def _fgate_bias_kernel(lf_ref, o_ref, carry, *, length, n_heads):
    t = pl.program_id(1)
    tt = lf_ref.shape[0]

    @pl.when(t == 0)
    def _():
        carry[...] = jnp.zeros_like(carry)

    ri = lax.broadcasted_iota(jnp.int32, (tt, tt), 0)
    rj = lax.broadcasted_iota(jnp.int32, (tt, tt), 1)
    tri = jnp.where(rj <= ri, 1.0, 0.0).astype(BF16)
    csum = _dot_exact_lhs(tri, lf_ref[...]) + carry[...]
    carry[...] = csum[tt - 1:tt, :]
    pos = t * tt + lax.broadcasted_iota(jnp.int32, csum.shape, 0)
    neg = jnp.where(pos < length, -LOG2_E * csum, NEG_INF)
    pieces = jnp.concatenate(_split3(neg), axis=1)
    sr = lax.broadcasted_iota(jnp.int32, (3 * LANES, LANES), 0)
    sc = lax.broadcasted_iota(jnp.int32, (3 * LANES, LANES), 1)
    for h in range(n_heads):
        sel = jnp.where(sr % LANES == h, jnp.where(sr // LANES == sc, 1.0, 0.0), 0.0).astype(BF16)
        o_ref[h] = jnp.dot(pieces, sel, preferred_element_type=F32).astype(BF16)


def _fgate_bias(lf, length, n_heads, tt):
    bsz, lp, _ = lf.shape
    return pl.pallas_call(
        functools.partial(_fgate_bias_kernel, length=length, n_heads=n_heads),
        grid=(bsz, lp // tt),
        in_specs=[pl.BlockSpec((None, tt, LANES), lambda b, t: (b, t, 0))],
        out_specs=pl.BlockSpec((None, n_heads, tt, LANES), lambda b, t: (b, 0, t, 0)),
        out_shape=jax.ShapeDtypeStruct((bsz, n_heads, lp, LANES), BF16),
        scratch_shapes=[pltpu.VMEM((1, LANES), F32)],
        compiler_params=_params(("parallel", "arbitrary")),
        name="fgate_bias",
    )(lf)


N_BIAS_PIECES = 3


ATTN_HEADS_PER_STEP = 2


def _attn_kernel(qt_ref, k_ref, nf_ref, vt_ref, o_ref, *, tq, tk, past):
    i = pl.program_id(2)
    n_heads = qt_ref.shape[0]
    tw = min(tq, MXU_DIM)
    chains = [(h, c) for h in range(n_heads) for c in range(tq // tw)]
    ones_rows = jnp.where(lax.broadcasted_iota(jnp.int32, (LANES, tw), 0) < N_BIAS_PIECES, 1.0, 0.0)
    qa = [jnp.concatenate([qt_ref[h, :, c * tw:(c + 1) * tw], ones_rows.astype(BF16)], axis=0)
          for h, c in chains]
    n_full = (past + i * tq) // tk

    def update(carry, s, vt):
        m, l, acc = carry
        m_new = jnp.maximum(m, jnp.max(s, axis=0, keepdims=True))
        p = jnp.exp2(s - m_new)
        alpha = jnp.exp2(m - m_new)
        l = alpha * l + jnp.sum(p, axis=0, keepdims=True)
        acc = alpha * acc + jnp.dot(vt, p.astype(BF16), preferred_element_type=F32)
        return m_new, l, acc

    def scores(j, causal=False):
        start = pl.multiple_of(j * tk, tk)
        ka = [jnp.concatenate([k_ref[h, pl.ds(start, tk), :], nf_ref[h, pl.ds(start, tk), :]], axis=1)
              for h in range(n_heads)]
        ss = [jnp.dot(ka[h], qa[n], preferred_element_type=F32) for n, (h, c) in enumerate(chains)]
        if causal:
            key = lax.broadcasted_iota(jnp.int32, (tk, tw), 0)
            qry = lax.broadcasted_iota(jnp.int32, (tk, tw), 1)
            ss = [jnp.where(key <= qry + c * tw, s, NEG_INF) for s, (h, c) in zip(ss, chains)]
        return ss

    def absorb(carry, ss, j):
        return tuple(update(carry[n], ss[n], vt_ref[h, j]) for n, (h, c) in enumerate(chains))

    def two_blocks(pair, carry):
        sa, sb = scores(2 * pair), scores(2 * pair + 1)
        return absorb(absorb(carry, sa, 2 * pair), sb, 2 * pair + 1)

    init = (jnp.full((1, tw), NEG_INF, F32), jnp.zeros((1, tw), F32), jnp.zeros((FOX_HEAD_DIM, tw), F32))
    carry = lax.fori_loop(0, n_full // 2, two_blocks, tuple(init for _ in chains))
    last_full = jnp.maximum(n_full - 1, 0)
    carry = lax.cond(n_full % 2 == 1, lambda c: absorb(c, scores(last_full), last_full), lambda c: c, carry)
    carry = absorb(carry, scores(n_full, causal=True), n_full)
    for n, (h, c) in enumerate(chains):
        _, l, acc = carry[n]
        o_ref[c * tw:(c + 1) * tw, h * FOX_HEAD_DIM:(h + 1) * FOX_HEAD_DIM] = (acc / l).T.astype(o_ref.dtype)


def _fox_attention(qt, k, nf, vt, past, tk):
    bsz, n_heads, nq, dh, tq = qt.shape
    lp = k.shape[2]
    hps = ATTN_HEADS_PER_STEP
    assert past % tk == 0 and (tq == tk or nq == 1) and tq <= tk and lp % tk == 0 and n_heads % hps == 0
    whole = lambda a: pl.BlockSpec((None, hps) + a.shape[2:], lambda b, h, i: (b, h) + (0,) * (a.ndim - 2))
    return pl.pallas_call(
        functools.partial(_attn_kernel, tq=tq, tk=tk, past=past),
        grid=(bsz, n_heads // hps, nq),
        in_specs=[pl.BlockSpec((None, hps, None, dh, tq), lambda b, h, i: (b, h, i, 0, 0)),
                  whole(k), whole(nf), whole(vt)],
        out_specs=pl.BlockSpec((None, tq, hps * dh), lambda b, h, i: (b, i, h)),
        out_shape=jax.ShapeDtypeStruct((bsz, nq * tq, n_heads * dh), BF16),
        compiler_params=_params(("parallel", "parallel", "arbitrary")),
        name="fox_attn",
    )(qt, k, nf, vt)


def _rwkv_kernel(p_ref, shift_ref, s0_ref, mu_ref, w0_ref, a0_ref, kk_ref, ka_ref, rk_ref,
                 gng_ref, gnb_ref, w2_ref, a2_ref, g2_ref, o_ref, sout_ref,
                 state, prev_row, *, tc, t_valid):
    c_len = RWKV_CHUNK
    t = pl.program_id(1)
    width = o_ref.shape[-1]
    n_groups = width // MXU_DIM

    @pl.when(t == 0)
    def _():
        state[...] = s0_ref[...]
        prev_row[...] = shift_ref[...]

    p = p_ref[...]
    ridx = lax.broadcasted_iota(jnp.int32, p.shape, 0)
    prev = jnp.where(ridx == 0, jnp.broadcast_to(prev_row[...], p.shape), pltpu.roll(p, 1, 0))
    prev_row[...] = p[tc - 1:tc, :]
    xs = p + (prev - p) * mu_ref[...]
    r = xs[:, 0:width]
    k = xs[:, width:2 * width]
    v = xs[:, 2 * width:3 * width]
    lora_in = xs[:, 3 * width:3 * width + LANES]
    gate_in = xs[:, 3 * width + LANES:]
    zw = _dot(jnp.tanh(lora_in), w2_ref[...])
    za = _dot(lora_in, a2_ref[...])
    g = _dot(_sigmoid(gate_in), g2_ref[...])
    lw = -DECAY_SCALE * _sigmoid(w0_ref[...] + zw)
    iclr = _sigmoid(a0_ref[...] + za)

    hr = lax.broadcasted_iota(jnp.int32, (MXU_DIM, MXU_DIM), 0) // RWKV_HEAD_DIM
    hc = lax.broadcasted_iota(jnp.int32, (MXU_DIM, MXU_DIM), 1) // RWKV_HEAD_DIM
    same_head = hr == hc
    bd_f32 = jnp.where(same_head, 1.0, 0.0)
    ones_bd = bd_f32.astype(BF16)

    def head_sum(x, split=True):
        hi = x.astype(BF16)
        lo = (x - hi.astype(F32)).astype(BF16) if split else None
        parts = []
        for gi in range(n_groups):
            ls = slice(gi * MXU_DIM, (gi + 1) * MXU_DIM)
            part = jnp.dot(hi[:, ls], ones_bd, preferred_element_type=F32)
            if split:
                part = part + jnp.dot(lo[:, ls], ones_bd, preferred_element_type=F32)
            parts.append(part)
        return jnp.concatenate(parts, axis=-1)

    kk = k * kk_ref[...]
    kk = kk * lax.rsqrt(jnp.maximum(head_sum(kk * kk), 1e-24))
    k = k * (1.0 + (iclr - 1.0) * ka_ref[...])
    if t_valid < tc:
        live = lax.broadcasted_iota(jnp.int32, (tc, width), 0) < t_valid
        lw = jnp.where(live, lw, 0.0)
        kk = jnp.where(live, kk, 0.0)
        k = jnp.where(live, k, 0.0)
        v = jnp.where(live, v, 0.0)

    ti = lax.broadcasted_iota(jnp.int32, (tc, tc), 0)
    tj = lax.broadcasted_iota(jnp.int32, (tc, tc), 1)
    tri = jnp.where(ti // c_len == tj // c_len, jnp.where(tj <= ti, 1.0, 0.0), 0.0).astype(BF16)
    gcum = _dot_exact_lhs(tri, lw)
    e_in = jnp.exp(gcum)
    e_inv = jnp.exp(-gcum)
    at_all = -kk * jnp.exp(gcum - lw)
    rt_all = r * e_in
    bt_all = kk * iclr * e_inv
    kt_all = k * e_inv
    bonus = head_sum(r * k * rk_ref[...]) * v

    row = lax.broadcasted_iota(jnp.int32, (c_len, MXU_DIM), 0)
    lane = lax.broadcasted_iota(jnp.int32, (c_len, MXU_DIM), 1) % c_len
    strict = jnp.where(lane < row, 1.0, 0.0)
    incl = jnp.where(lane <= row, 1.0, 0.0)
    eye_w = jnp.where(lane == row, 1.0, 0.0)

    def bd(x):
        return jnp.concatenate([x.astype(BF16)] * HEADS_PER_GROUP, axis=0) * ones_bd

    n_sq = c_len.bit_length() - 1
    n_chunks = tc // c_len
    units = [(ci, gi) for ci in range(n_chunks) for gi in range(n_groups)]

    def cut(x, u):
        ci, gi = u
        return x[ci * c_len:(ci + 1) * c_len, gi * MXU_DIM:(gi + 1) * MXU_DIM]

    at = [cut(at_all, u) for u in units]
    rt = [cut(rt_all, u) for u in units]
    bt = [cut(bt_all, u) for u in units]
    kt = [cut(kt_all, u) for u in units]
    vv = [cut(v, u) for u in units]
    ar = [jnp.concatenate([a, r_], axis=0) for a, r_ in zip(at, rt)]
    ab = [_dot_nt(x, bd(b_)) for x, b_ in zip(ar, bt)]
    ak = [_dot_nt(x, bd(k_)) for x, k_ in zip(ar, kt)]
    pw = [x[:c_len] * strict for x in ab]
    a_rb = [x[c_len:] * incl for x in ab]
    a_ak = [x[:c_len] * strict for x in ak]
    a_rk = [x[c_len:] * incl for x in ak]
    tm = [eye_w + x for x in pw]
    pw = [_dot(x, bd(x)) for x in pw]
    for js in range(1, n_sq):
        if js < n_sq - 1:
            tp = [_dot(jnp.concatenate([t_, x], axis=0), bd(x)) for t_, x in zip(tm, pw)]
            tm = [t_ + y_[:c_len] for t_, y_ in zip(tm, tp)]
            pw = [y_[c_len:] for y_ in tp]
        else:
            tm = [t_ + _dot(t_, bd(x)) for t_, x in zip(tm, pw)]
    bdv = [bd(x) for x in vv]
    a_hat = [_dot(t_, bd(a)) for t_, a in zip(tm, at)]
    av = [_dot(x, b_) for x, b_ in zip(a_ak, bdv)]
    u_hat = [_dot(t_, bd(x)) for t_, x in zip(tm, av)]
    r_hat = [r_ + _dot(x, bd(a)) for r_, x, a in zip(rt, a_rb, a_hat)]
    y_hat = [_dot(x, bd(uh)) + _dot(z, b_) for x, uh, z, b_ in zip(a_rb, u_hat, a_rk, bdv)]
    lhs = [jnp.concatenate([a, r_], axis=0) for a, r_ in zip(a_hat, r_hat)]

    st = [state[gi] for gi in range(n_groups)]
    y_rows = []
    for ci in range(n_chunks):
        gend = jnp.exp(gcum[(ci + 1) * c_len - 1:(ci + 1) * c_len, :])
        us = [ci * n_groups + gi for gi in range(n_groups)]
        ge = [gend[:, gi * MXU_DIM:(gi + 1) * MXU_DIM] for gi in range(n_groups)]
        uy = [_dot_nt(lhs[u], st[gi]) for gi, u in enumerate(us)]
        uu = [uy[gi][:c_len] + u_hat[u] for gi, u in enumerate(us)]
        y_rows.append(jnp.concatenate([uy[gi][c_len:] + y_hat[u] for gi, u in enumerate(us)], axis=1))
        uv_t = [jnp.concatenate([uu[gi], vv[u]], axis=0).T for gi, u in enumerate(us)]
        bk = [jnp.concatenate([bt[u] * ge[gi], kt[u] * ge[gi]], axis=0) for gi, u in enumerate(us)]
        st = [st[gi] * ge[gi] + _dot(uv_t[gi], bk[gi]) * bd_f32 for gi in range(n_groups)]
    for gi in range(n_groups):
        state[gi] = st[gi]

    y = jnp.concatenate(y_rows, axis=0)
    inv_n = 1.0 / RWKV_HEAD_DIM
    mean = head_sum(y, split=False) * inv_n
    yc = y - mean
    var = head_sum(yc * yc, split=False) * inv_n
    yn = yc * lax.rsqrt(var + RWKV_GN_EPS) * gng_ref[...] + gnb_ref[...]
    o_ref[...] = ((yn + bonus) * g).astype(o_ref.dtype)

    @pl.when(t == pl.num_programs(1) - 1)
    def _():
        sout_ref[...] = state[...]


def _rwkv_mix(p, shift_prev, s0_bd, vecs, w2p, a2p, g2p, tc, t_valid):
    bsz, tp, cols = p.shape
    width = w2p.shape[1]
    n_groups = width // MXU_DIM
    vec_specs = [pl.BlockSpec((1, a.shape[1]), lambda b, t: (0, 0)) for a in vecs]
    mat = lambda a: pl.BlockSpec(a.shape, lambda b, t: (0, 0))
    st_spec = pl.BlockSpec((None, n_groups, MXU_DIM, MXU_DIM), lambda b, t: (b, 0, 0, 0))
    return pl.pallas_call(
        functools.partial(_rwkv_kernel, tc=tc, t_valid=t_valid),
        grid=(bsz, tp // tc),
        in_specs=[pl.BlockSpec((None, tc, cols), lambda b, t: (b, t, 0)),
                  pl.BlockSpec((None, 1, cols), lambda b, t: (b, 0, 0)),
                  st_spec] + vec_specs + [mat(w2p), mat(a2p), mat(g2p)],
        out_specs=[pl.BlockSpec((None, tc, width), lambda b, t: (b, t, 0)), st_spec],
        out_shape=[jax.ShapeDtypeStruct((bsz, tp, width), BF16),
                   jax.ShapeDtypeStruct(s0_bd.shape, F32)],
        scratch_shapes=[pltpu.VMEM((n_groups, MXU_DIM, MXU_DIM), F32), pltpu.VMEM((1, cols), F32)],
        compiler_params=_params(("parallel", "arbitrary")),
        name="rwkv_mix",
    )(p, shift_prev, s0_bd, *vecs, w2p, a2p, g2p)


def _merge_kernel(x_ref, oa_ref, ob_ref, gt_ref, wa_ref, wb_ref, wo_ref, lig_ref, lib_ref,
                  l1g_ref, l1b_ref, wr_ref, rb_ref, x1_ref, ids_ref, wts_ref, *, alpha):
    d = x_ref.shape[-1]
    xn = _layer_norm(x_ref[...], lig_ref[...], lib_ref[...])
    ya = jnp.dot(oa_ref[...], wa_ref[...], preferred_element_type=F32)
    yb = jnp.dot(ob_ref[...], wb_ref[...], preferred_element_type=F32)
    merged = gt_ref[:, :d].astype(F32) * ya + gt_ref[:, d:].astype(F32) * yb
    out = jnp.dot(merged.astype(BF16), wo_ref[...], preferred_element_type=F32)
    x1 = _layer_norm(alpha * xn + out, l1g_ref[...], l1b_ref[...])
    x1_ref[...] = x1

    h1, h2, _ = _split3(x1)
    r1 = jnp.dot(h1, wr_ref[...], preferred_element_type=F32)
    r2 = jnp.dot(h2, wr_ref[...], preferred_element_type=F32)
    logits = (r1[:, :LANES] + (r1[:, LANES:] + r2[:, :LANES]) + r2[:, LANES:]) + rb_ref[...]
    lane = lax.broadcasted_iota(jnp.int32, logits.shape, 1).astype(F32)
    big = 1e9

    def first_max(vals):
        mx = jnp.max(vals, axis=-1, keepdims=True)
        idx = jnp.min(jnp.where(vals == mx, lane, big), axis=-1, keepdims=True)
        return mx, idx

    is_grp = lane < N_GROUPS
    gmax, grp = first_max(jnp.where(is_grp, logits, NEG_INF))
    p_grp = 1.0 / jnp.sum(jnp.where(is_grp, jnp.exp(logits - gmax), 0.0), axis=-1, keepdims=True)
    lo = N_GROUPS + grp * EXPERTS_PER_GROUP
    elog = jnp.where(lane >= lo, jnp.where(lane < lo + EXPERTS_PER_GROUP, logits, NEG_INF), NEG_INF)
    v1, i1 = first_max(elog)
    v2, i2 = first_max(jnp.where(lane == i1, NEG_INF, elog))
    e2 = jnp.exp(v2 - v1)
    w1 = p_grp / (1.0 + e2)
    w2 = p_grp * e2 / (1.0 + e2)
    ids = jnp.where(lane == 0, i1 - N_GROUPS, jnp.where(lane == 1, i2 - N_GROUPS, 0.0))
    ids_ref[...] = ids.astype(jnp.int32)
    wts_ref[...] = jnp.where(lane == 0, w1, jnp.where(lane == 1, w2, 0.0))


def _merge_out(x, oa, ob, gates, wa, wb, wo, lig, lib, l1g, l1b, w_router, rbias, alpha, tm):
    n, d = x.shape
    half = oa.shape[1]
    row = lambda c: pl.BlockSpec((tm, c), lambda i: (i, 0))
    const = lambda a: pl.BlockSpec(a.shape, lambda i: (0, 0), pipeline_mode=pl.Buffered(1))
    return pl.pallas_call(
        functools.partial(_merge_kernel, alpha=alpha),
        grid=(n // tm,),
        in_specs=[row(d), row(half), row(half), row(2 * d), const(wa), const(wb), const(wo),
                  const(lig), const(lib), const(l1g), const(l1b), const(w_router), const(rbias)],
        out_specs=[row(d), row(LANES), row(LANES)],
        out_shape=[jax.ShapeDtypeStruct((n, d), F32), jax.ShapeDtypeStruct((n, LANES), jnp.int32),
                   jax.ShapeDtypeStruct((n, LANES), F32)],
        compiler_params=_params(("parallel",)),
        name="merge_out",
    )(x, oa, ob, gates, wa, wb, wo, lig, lib, l1g, l1b, w_router, rbias)


DMA_ISSUE_UNROLL = 8
MOE_BLOCK = 512
COMBINE_PIECE = 32


def _wait_rows(make_row_copy, count):
    lax.fori_loop(0, count, lambda q, c: (make_row_copy(0).wait(), c)[1], 0, unroll=DMA_ISSUE_UNROLL)


def _dispatch_kernel(zblk_ref, dest_ref, *rest, steps):
    xs_hbm, zeros, sem, zsem = rest[-4:]
    x_refs = rest[:-4]
    i = pl.program_id(0)
    blk = zeros.shape[0]

    def zero_block(z):
        start = pl.multiple_of(zblk_ref[z] * blk, blk)
        return pltpu.make_async_copy(zeros, xs_hbm.at[pl.ds(start, blk), :], zsem)

    @pl.when(i == 0)
    def _():
        zeros[...] = jnp.zeros_like(zeros)
        for z in range(zblk_ref.shape[0]):
            pl.when(zblk_ref[z] >= 0)(lambda z=z: zero_block(z).start())
        for z in range(zblk_ref.shape[0]):
            pl.when(zblk_ref[z] >= 0)(lambda z=z: zero_block(z).wait())

    def scatter_rows(x_ref):
        tm = x_ref.shape[0]

        def to_slot(r, kslot):
            return pltpu.make_async_copy(x_ref.at[pl.ds(r, 1), :],
                                         xs_hbm.at[pl.ds(dest_ref[0, TOP_K * r + kslot], 1), :], sem)

        def body(r, c):
            for kslot in range(TOP_K):
                to_slot(r, kslot).start()
            return c
        lax.fori_loop(0, tm, body, 0, unroll=DMA_ISSUE_UNROLL)
        _wait_rows(lambda q: to_slot(0, 0), TOP_K * tm)

    for g, x_ref in enumerate(x_refs):
        in_group = jnp.logical_and(i >= steps[g], i < steps[g + 1])
        pl.when(in_group)(functools.partial(scatter_rows, x_ref))


def _moe_dispatch(x1s, tiles, zero_blocks, dest_steps, n_rows, blk):
    d = x1s[0].shape[1]
    steps = [0]
    for x1, tm in zip(x1s, tiles):
        steps.append(steps[-1] + x1.shape[0] // tm)

    def x_spec(g, tm):
        last = steps[g + 1] - steps[g] - 1
        return pl.BlockSpec((tm, d), lambda i, zb: (jnp.clip(i - steps[g], 0, last), 0))

    grid_spec = pltpu.PrefetchScalarGridSpec(
        num_scalar_prefetch=1,
        grid=(steps[-1],),
        in_specs=[pl.BlockSpec((None, 1, dest_steps.shape[-1]), lambda i, zb: (i, 0, 0),
                               memory_space=pltpu.SMEM)] + [x_spec(g, tm) for g, tm in enumerate(tiles)],
        out_specs=pl.BlockSpec(memory_space=pl.ANY),
        scratch_shapes=[pltpu.VMEM((blk, d), F32), pltpu.SemaphoreType.DMA(()), pltpu.SemaphoreType.DMA(())],
    )
    return pl.pallas_call(
        functools.partial(_dispatch_kernel, steps=tuple(steps)),
        grid_spec=grid_spec,
        out_shape=jax.ShapeDtypeStruct((n_rows, d), F32),
        compiler_params=_params(("arbitrary",)),
        name="moe_dispatch",
    )(zero_blocks, dest_steps, *x1s)


def _experts_kernel(blk_e_ref, xblk_ref, nvalid_ref, xs_ref, wg_ref, wu_ref, wd_ref, ys_ref):
    j = pl.program_id(0)

    @pl.when(nvalid_ref[j] > 0)
    def _():
        xv = xs_ref[...].astype(BF16)
        hg = jnp.dot(xv, wg_ref[...].astype(BF16), preferred_element_type=F32)
        hu = jnp.dot(xv, wu_ref[...].astype(BF16), preferred_element_type=F32)
        h = hg * _sigmoid(hg) * hu
        ys_ref[...] = jnp.dot(h.astype(BF16), wd_ref[...].astype(BF16), preferred_element_type=F32)

    @pl.when(nvalid_ref[j] == 0)
    def _():
        ys_ref[...] = jnp.zeros_like(ys_ref)


def _moe_experts(xs, blk_e, xblk, nvalid, w_gate, w_up, w_down, blk):
    d = xs.shape[1]
    nb = blk_e.shape[0]
    de = w_gate.shape[-1]
    grid_spec = pltpu.PrefetchScalarGridSpec(
        num_scalar_prefetch=3,
        grid=(nb,),
        in_specs=[
            pl.BlockSpec((blk, d), lambda j, be, xb, nv: (xb[j], 0)),
            pl.BlockSpec((None, d, de), lambda j, be, xb, nv: (be[j], 0, 0)),
            pl.BlockSpec((None, d, de), lambda j, be, xb, nv: (be[j], 0, 0)),
            pl.BlockSpec((None, de, d), lambda j, be, xb, nv: (be[j], 0, 0)),
        ],
        out_specs=pl.BlockSpec((blk, d), lambda j, be, xb, nv: (j, 0)),
    )
    return pl.pallas_call(
        _experts_kernel,
        grid_spec=grid_spec,
        out_shape=jax.ShapeDtypeStruct((nb * blk, d), F32),
        compiler_params=_params(("arbitrary",)),
        name="moe_experts",
    )(blk_e, xblk, nvalid, xs, w_gate, w_up, w_down)


def _combine_kernel(cur_ref, nxt_ref, x1_ref, wts_ref, g_ref, b_ref, ys_hbm, o_ref, ybuf, sem, *, alpha):
    i = pl.program_id(0)
    tm = x1_ref.shape[0]
    piece = min(tm, COMBINE_PIECE)

    def row(tbl_ref, s, r, kslot):
        return pltpu.make_async_copy(ys_hbm.at[pl.ds(tbl_ref[0, TOP_K * r + kslot], 1), :],
                                     ybuf.at[s, pl.ds(kslot * tm + r, 1), :], sem.at[s])

    @pl.when(i == 0)
    def _():
        def body(r, c):
            for kslot in range(TOP_K):
                row(cur_ref, 0, r, kslot).start()
            return c
        lax.fori_loop(0, tm, body, 0, unroll=DMA_ISSUE_UNROLL)

    def step(slot):
        _wait_rows(lambda q: row(cur_ref, slot, 0, 0), TOP_K * tm)
        for r0 in range(0, tm, piece):
            rows = slice(r0, r0 + piece)
            w = wts_ref[rows, :]
            y = w[:, 0:1] * ybuf[slot, r0:r0 + piece, :] + w[:, 1:2] * ybuf[slot, tm + r0:tm + r0 + piece, :]
            o_ref[rows, :] = _layer_norm(alpha * x1_ref[rows, :] + y, g_ref[...], b_ref[...])
            for r in range(r0, r0 + piece):
                for kslot in range(TOP_K):
                    row(nxt_ref, 1 - slot, r, kslot).start()

        @pl.when(i + 1 == pl.num_programs(0))
        def _():
            _wait_rows(lambda q: row(nxt_ref, 1 - slot, 0, 0), TOP_K * tm)

    pl.when(i % 2 == 0)(lambda: step(0))
    pl.when(i % 2 == 1)(lambda: step(1))


def _moe_combine(x1, ys, dest_tiles, wts, g, b, alpha, tm):
    n, d = x1.shape
    last = n // tm - 1
    tbl = lambda f: pl.BlockSpec((None, 1, TOP_K * tm), lambda i: (f(i), 0, 0), memory_space=pltpu.SMEM)
    return pl.pallas_call(
        functools.partial(_combine_kernel, alpha=alpha),
        grid=(n // tm,),
        in_specs=[tbl(lambda i: i), tbl(lambda i: jnp.minimum(i + 1, last)),
                  pl.BlockSpec((tm, d), lambda i: (i, 0)),
                  pl.BlockSpec((tm, LANES), lambda i: (i, 0)),
                  pl.BlockSpec((1, d), lambda i: (0, 0)), pl.BlockSpec((1, d), lambda i: (0, 0)),
                  pl.BlockSpec(memory_space=pl.ANY)],
        out_specs=pl.BlockSpec((tm, d), lambda i: (i, 0)),
        out_shape=jax.ShapeDtypeStruct((n, d), F32),
        scratch_shapes=[pltpu.VMEM((2, TOP_K * tm, d), F32), pltpu.SemaphoreType.DMA((2,))],
        compiler_params=_params(("arbitrary",)),
        name="moe_combine",
    )(dest_tiles, dest_tiles, x1, wts, g, b, ys)


def _dispatch_tables(eid, blk):
    n = eid.shape[0]
    nk = n * TOP_K
    nb = -(-(nk + N_EXPERTS * (blk - 1)) // blk)
    experts = jnp.arange(N_EXPERTS, dtype=jnp.int32)
    chunk = LANES
    e2 = eid.reshape(nk // chunk, chunk)
    onehot = (e2[:, :, None] == experts).astype(jnp.int32)
    chunk_counts = onehot.sum(axis=1)
    chunk_base = jnp.cumsum(chunk_counts, axis=0) - chunk_counts
    earlier = jnp.arange(chunk)[None, :] < jnp.arange(chunk)[:, None]
    in_chunk = ((e2[:, :, None] == e2[:, None, :]) & earlier[None]).sum(axis=2)
    counts = chunk_counts.sum(axis=0)
    nblk_e = (counts + blk - 1) // blk
    blk_end = jnp.cumsum(nblk_e)
    blk_start = blk_end - nblk_e
    base = chunk_base + (blk_start * blk)[None, :]
    dest = ((onehot * base[:, None, :]).sum(axis=2) + in_chunk).reshape(-1)
    n_used = blk_end[-1]
    tail = n_used + experts
    zero_blocks = jnp.concatenate([jnp.where(nblk_e > 0, blk_end - 1, -1), jnp.where(tail < nb, tail, -1)])
    bidx = jnp.arange(nb, dtype=jnp.int32)
    blk_e = jnp.minimum((blk_end[None, :] <= bidx[:, None]).sum(axis=1), N_EXPERTS - 1).astype(jnp.int32)
    mine = (blk_e[:, None] == experts[None, :]).astype(jnp.int32)
    left = (mine * (counts + blk_start * blk)[None, :]).sum(axis=1) - bidx * blk
    nvalid = jnp.clip(left, 0, blk).astype(jnp.int32)
    nvalid = jnp.where(bidx < n_used, nvalid, 0)
    xblk = jnp.minimum(bidx, n_used - 1).astype(jnp.int32)
    return dest.astype(jnp.int32), zero_blocks.astype(jnp.int32), blk_e, xblk, nvalid, nb * blk


def _pick(n, prefs):
    for p in prefs:
        if n % p == 0:
            return p
    return n


def _layer(x, past_k, past_v, past_logf, s0, shift_prev, wts, alpha):
    bsz, t, d = x.shape
    n = bsz * t
    xf = x.reshape(n, d)
    fox_width = wts['w_qkv'].shape[1] // 3
    n_heads = fox_width // FOX_HEAD_DIM
    rw_width = wts['w2p'].shape[1]
    rw_heads = rw_width // RWKV_HEAD_DIM
    rw_cols = wts['rwkv_cols']
    past = past_k.shape[1]

    xn = _entry_norm(xf, wts['ln_in_g'], wts['ln_in_b'], _pick(n, (512, 256)))
    tm = _pick(n, (PROJ_ROWS, 512, 256))
    gates = _proj_matmul(_gates_kernel, xn, wts['w_gates'], BF16, tm, 1024, "proj_gates")
    p_rw = _proj_matmul(_rwkv_proj_kernel, xn, wts['w_rwkv'], F32, tm, wts['w_rwkv'].shape[1] // 3, "proj_rwkv")
    t_fox = -(-t // LANES) * LANES
    tq = _pick(t_fox, (ATTN_BLOCK, 256, LANES))
    xn_fox = xn.reshape(bsz, t, d)
    if t_fox != t:
        xn_fox = jnp.pad(xn_fox, ((0, 0), (0, t_fox - t), (0, 0)))
    qt, k_f, k_b, v_f, vt, lf = _fox_proj(xn_fox, wts['w_qkv'], wts['w_f'], wts['b_f'], wts['q_norm'],
                                          wts['k_norm'], tq)
    k_f, v_f, logf = k_f[:, :t], v_f[:, :t], lf[:, :t, :n_heads]
    if past:
        tk = ATTN_BLOCK
        lpad = -(-(past + t) // tk) * tk
        grow = lambda a, ax: jnp.pad(a, [(0, lpad - a.shape[ax]) if i == ax else (0, 0) for i in range(a.ndim)])
        lf_past = jnp.pad(past_logf.astype(F32), ((0, 0), (0, 0), (0, LANES - n_heads)))
        lf_all = grow(jnp.concatenate([lf_past, lf[:, :t]], axis=1), 1)
        k_all = grow(jnp.concatenate([jnp.swapaxes(past_k, 1, 2).astype(BF16), k_b[:, :, :t]], axis=2), 2)
        vt_past = jnp.transpose(past_v, (0, 2, 3, 1)).astype(BF16)
        vt_all = grow(jnp.concatenate([vt_past, vt[:, :, 0, :, :t]], axis=3), 3)
        vt_all = jnp.swapaxes(vt_all.reshape(bsz, n_heads, FOX_HEAD_DIM, lpad // tk, tk), 2, 3)
    else:
        tk, lf_all, k_all, vt_all = tq, lf, k_b, vt
    nf = _fgate_bias(lf_all, past + t, n_heads, tk)
    o_a = _fox_attention(qt, k_all, nf, vt_all, past, tk)[:, :t]

    tp = -(-t // RWKV_CHUNK) * RWKV_CHUNK
    tc = _pick(tp, (256, 128, 64))
    p3 = p_rw.reshape(bsz, t, -1)
    shift_new = p3[:, t - 1:t, :rw_cols]
    if tp != t:
        p3 = jnp.pad(p3, ((0, 0), (0, tp - t), (0, 0)))
    shift_in = jnp.pad(shift_prev.astype(F32), ((0, 0), (0, 0), (0, p3.shape[-1] - rw_cols)))
    n_grp = rw_heads // HEADS_PER_GROUP
    eye = jnp.eye(HEADS_PER_GROUP, dtype=F32)
    s0_g = s0.astype(F32).reshape(bsz, n_grp, HEADS_PER_GROUP, RWKV_HEAD_DIM, RWKV_HEAD_DIM)
    s0_bd = jnp.einsum('bghvk,hj->bghvjk', s0_g, eye).reshape(bsz, n_grp, MXU_DIM, MXU_DIM)
    o_b, s_bd = _rwkv_mix(p3, shift_in, s0_bd, wts['rwkv_vecs'], wts['w2p'], wts['a2p'], wts['g2p'], tc, t)
    s_new = jnp.einsum('bghvjk,hj->bghvk',
                       s_bd.reshape(bsz, n_grp, HEADS_PER_GROUP, RWKV_HEAD_DIM, HEADS_PER_GROUP, RWKV_HEAD_DIM),
                       eye).reshape(bsz, rw_heads, RWKV_HEAD_DIM, RWKV_HEAD_DIM)
    o_b = o_b[:, :t].reshape(n, rw_width)

    routed = _merge_out(xf, o_a.reshape(n, fox_width), o_b, gates, wts['w_a'], wts['w_b'], wts['w_o'],
                        wts['ln_in_g'], wts['ln_in_b'], wts['ln1_g'], wts['ln1_b'],
                        wts['w_router'], wts['b_router'], alpha, _pick(n, (512, 256)))
    hd = (bsz, t, n_heads, FOX_HEAD_DIM)
    return routed, (k_f.reshape(hd), v_f.reshape(hd), logf, s_new, shift_new)


def _hier_moe(groups, wts, alpha):
    sizes = [x1.shape[0] for x1, _, _ in groups]
    n_all = sum(sizes)
    blk = MOE_BLOCK
    while blk > 8 and blk * N_EXPERTS > n_all * TOP_K:
        blk //= 2
    eid = jnp.concatenate([ids[:, :TOP_K] for _, ids, _ in groups], axis=0)
    dest, zero_blocks, blk_e, xblk, nvalid, n_rows = _dispatch_tables(eid, blk)
    offset, dests = 0, []
    for n in sizes:
        dests.append(dest[TOP_K * offset:TOP_K * (offset + n)])
        offset += n
    tiles = [_pick(n, (512, 256)) for n in sizes]
    width = TOP_K * max(tiles)
    dest_steps = jnp.concatenate([jnp.pad(dst.reshape(-1, 1, TOP_K * tm), ((0, 0), (0, 0), (0, width - TOP_K * tm)))
                                  for dst, tm in zip(dests, tiles)], axis=0)
    xs = _moe_dispatch([x1 for x1, _, _ in groups], tiles, zero_blocks, dest_steps, n_rows, blk)
    ys = _moe_experts(xs, blk_e, xblk, nvalid, wts['moe_w_gate'], wts['moe_w_up'], wts['moe_w_down'], blk)
    outs = []
    for (x1, _, rw), dst, n in zip(groups, dests, sizes):
        tm = _pick(n, (256, 128, 64, 32, 16, 8))
        outs.append(_moe_combine(x1, ys, dst.reshape(-1, 1, TOP_K * tm), rw, wts['ln2_g'], wts['ln2_b'],
                                 alpha, tm))
    return outs


def _prepare_weights(l, ln_in_g, ln_in_b, w_in, fox_b_f, fox_q_norm, fox_k_norm, rwkv_mu, rwkv_w0, rwkv_w2,
                     rwkv_a0, rwkv_a2, rwkv_g2, rwkv_k_k, rwkv_k_a, rwkv_r_k, rwkv_gn_g, rwkv_gn_b,
                     w_branch_a, w_branch_b, w_out, ln1_g, ln1_b, router_group_w, router_group_b,
                     router_expert_w, router_expert_b, moe_w_gate, moe_w_up, moe_w_down, ln2_g, ln2_b):
    d = w_in.shape[1]
    fox_width = w_branch_a.shape[1]
    rw_width = w_branch_b.shape[1]
    n_heads = fox_width // FOX_HEAD_DIM
    gate_cols = 2 * d
    fox_cols = 3 * fox_width + n_heads
    rw_cols = 3 * rw_width + RWKV_DECAY_RANK + RWKV_ICLR_RANK + RWKV_GATE_RANK
    row = lambda a: a.astype(F32).reshape(1, -1)
    w = w_in[l]
    w_fox = w[:, gate_cols:gate_cols + fox_cols]
    w_rw = w[:, gate_cols + fox_cols:]
    lora = RWKV_DECAY_RANK + RWKV_ICLR_RANK
    assert lora == LANES
    gate_pad = -(-RWKV_GATE_RANK // LANES) * LANES
    cols_pad = 3 * rw_width + lora + gate_pad
    pad_c = cols_pad - rw_cols
    zeros = lambda r: jnp.zeros((r, rw_width), F32)
    wr = jnp.concatenate([router_group_w[l], router_expert_w[l]], axis=1).astype(F32)
    wr = jnp.pad(wr, ((0, 0), (0, LANES - wr.shape[1])))
    rb = jnp.concatenate([router_group_b[l], router_expert_b[l]]).astype(F32)
    return {
        'ln_in_g': row(ln_in_g), 'ln_in_b': row(ln_in_b),
        'w_gates': w[:, :gate_cols].astype(BF16),
        'w_qkv': w_fox[:, :3 * fox_width].astype(BF16),
        'w_f': jnp.pad(w_fox[:, 3 * fox_width:], ((0, 0), (0, LANES - n_heads))).astype(BF16),
        'b_f': jnp.pad(row(fox_b_f[l]), ((0, 0), (0, LANES - n_heads))),
        'q_norm': row(fox_q_norm[l]), 'k_norm': row(fox_k_norm[l]),
        'w_rwkv': jnp.pad(w_rw, ((0, 0), (0, pad_c))).astype(BF16),
        'rwkv_cols': rw_cols,
        'rwkv_vecs': [jnp.pad(row(rwkv_mu[l]), ((0, 0), (0, pad_c))), row(rwkv_w0[l]), row(rwkv_a0[l]),
                      row(rwkv_k_k[l]), row(rwkv_k_a[l]), row(rwkv_r_k[l]), row(rwkv_gn_g[l]),
                      row(rwkv_gn_b[l])],
        'w2p': jnp.concatenate([rwkv_w2[l].astype(F32), zeros(RWKV_ICLR_RANK)]).astype(BF16),
        'a2p': jnp.concatenate([zeros(RWKV_DECAY_RANK), rwkv_a2[l].astype(F32)]).astype(BF16),
        'g2p': jnp.concatenate([rwkv_g2[l].astype(F32), zeros(gate_pad - RWKV_GATE_RANK)]).astype(BF16),
        'w_a': w_branch_a[l].astype(BF16), 'w_b': w_branch_b[l].astype(BF16), 'w_o': w_out[l].astype(BF16),
        'ln1_g': row(ln1_g[l]), 'ln1_b': row(ln1_b[l]),
        'w_router': jnp.concatenate(_split3(wr)[:2], axis=1),
        'b_router': jnp.pad(row(rb), ((0, 0), (0, LANES - rb.shape[0]))),
        'moe_w_gate': moe_w_gate[l], 'moe_w_up': moe_w_up[l], 'moe_w_down': moe_w_down[l],
        'ln2_g': row(ln2_g[l]), 'ln2_b': row(ln2_b[l]),
    }


def kernel(x_prompt, x_sample, cache_fox_k, cache_fox_v, cache_fox_logf, state_rwkv, state_rwkv_shift,
           ln_in_g, ln_in_b, w_in, fox_b_f, fox_q_norm, fox_k_norm, rwkv_mu, rwkv_w0, rwkv_w2,
           rwkv_a0, rwkv_a2, rwkv_g2, rwkv_k_k, rwkv_k_a, rwkv_r_k, rwkv_gn_g, rwkv_gn_b,
           w_branch_a, w_branch_b, w_out, ln1_g, ln1_b, router_group_w, router_group_b,
           router_expert_w, router_expert_b, moe_w_gate, moe_w_up, moe_w_down, ln2_g, ln2_b):
    depth = w_in.shape[0]
    assert depth == 1, "the entry LayerNorm is fused into the layer's projections: single-layer trunk only"
    alpha = (2.0 * depth) ** 0.25
    bp = x_prompt.shape[0]
    n_fox_heads = fox_b_f.shape[1]
    rw_heads, rw_dim = state_rwkv.shape[2], state_rwkv.shape[3]
    rw_cols = state_rwkv_shift.shape[-1]
    wts = _prepare_weights(0, ln_in_g, ln_in_b, w_in, fox_b_f, fox_q_norm, fox_k_norm, rwkv_mu, rwkv_w0,
                           rwkv_w2, rwkv_a0, rwkv_a2, rwkv_g2, rwkv_k_k, rwkv_k_a, rwkv_r_k, rwkv_gn_g,
                           rwkv_gn_b, w_branch_a, w_branch_b, w_out, ln1_g, ln1_b, router_group_w,
                           router_group_b, router_expert_w, router_expert_b, moe_w_gate, moe_w_up,
                           moe_w_down, ln2_g, ln2_b)
    routed_p, new_p = _layer(x_prompt, jnp.zeros((bp, 0, n_fox_heads, FOX_HEAD_DIM), F32),
                             jnp.zeros((bp, 0, n_fox_heads, FOX_HEAD_DIM), F32),
                             jnp.zeros((bp, 0, n_fox_heads), F32),
                             jnp.zeros((bp, rw_heads, rw_dim, rw_dim), F32),
                             jnp.zeros((bp, 1, rw_cols), F32), wts, alpha)
    routed_s, new_s = _layer(x_sample, cache_fox_k[0], cache_fox_v[0], cache_fox_logf[0], state_rwkv[0],
                             state_rwkv_shift[0], wts, alpha)
    yp, ys = _hier_moe([routed_p, routed_s], wts, alpha)
    return ((yp.reshape(x_prompt.shape), ys.reshape(x_sample.shape))
            + tuple(a[None] for a in new_p) + tuple(a[None] for a in new_s))
```

```python
import functools

import jax
import jax.numpy as jnp
from jax import lax
from jax.experimental import pallas as pl
from jax.experimental.pallas import tpu as pltpu

F32 = jnp.float32
BF16 = jnp.bfloat16

FOX_HEAD_DIM = 128
RWKV_HEAD_DIM = 64
RWKV_DECAY_RANK = 64
RWKV_ICLR_RANK = 64
RWKV_GATE_RANK = 160
RWKV_GN_EPS = 64e-5
N_GROUPS = 4
EXPERTS_PER_GROUP = 8
N_EXPERTS = N_GROUPS * EXPERTS_PER_GROUP
TOP_K = 2
LN_EPS = 1e-5
QK_EPS = 1e-6
NEG_INF = -1e30
LOG2_E = 1.4426950408889634
DECAY_SCALE = 0.6065306597126334

LANES = 128
MXU_DIM = 256
VMEM_LIMIT_BYTES = 56 * 1024 * 1024

PROJ_ROWS = 1024
ATTN_BLOCK = 512
RWKV_CHUNK = 64
HEADS_PER_GROUP = MXU_DIM // RWKV_HEAD_DIM


def _params(semantics):
    return pltpu.CompilerParams(dimension_semantics=semantics, vmem_limit_bytes=VMEM_LIMIT_BYTES)


def _dot(a, b):
    return jnp.dot(a.astype(BF16), b.astype(BF16), preferred_element_type=F32)


def _dot_nt(a, b):
    return lax.dot_general(a.astype(BF16), b.astype(BF16), (((1,), (1,)), ((), ())),
                           preferred_element_type=F32)


def _split3(x):
    h1 = x.astype(BF16)
    r1 = x - h1.astype(F32)
    h2 = r1.astype(BF16)
    h3 = (r1 - h2.astype(F32)).astype(BF16)
    return h1, h2, h3


def _dot_exact_rhs(x, m_bf16):
    h1, h2, h3 = _split3(x)
    d = lambda h: jnp.dot(h, m_bf16, preferred_element_type=F32)
    return d(h1) + d(h2) + d(h3)


def _dot_exact_lhs(m_bf16, x):
    h1, h2, h3 = _split3(x)
    d = lambda h: jnp.dot(m_bf16, h, preferred_element_type=F32)
    return d(h1) + d(h2) + d(h3)


def _layer_norm(x, g, b):
    mu = jnp.mean(x, axis=-1, keepdims=True)
    xc = x - mu
    var = jnp.mean(xc * xc, axis=-1, keepdims=True)
    return xc * lax.rsqrt(var + LN_EPS) * g + b


def _sigmoid(x):
    return 1.0 / (1.0 + jnp.exp(-x))


def _log_sigmoid(x):
    return jnp.minimum(x, 0.0) - jnp.log(1.0 + jnp.exp(-jnp.abs(x)))


def _ln_kernel(x_ref, g_ref, b_ref, o_ref):
    o_ref[...] = _layer_norm(x_ref[...], g_ref[...], b_ref[...]).astype(o_ref.dtype)


def _entry_norm(x, ln_g, ln_b, tm):
    n, d = x.shape
    return pl.pallas_call(
        _ln_kernel,
        grid=(n // tm,),
        in_specs=[pl.BlockSpec((tm, d), lambda i: (i, 0)),
                  pl.BlockSpec((1, d), lambda i: (0, 0)), pl.BlockSpec((1, d), lambda i: (0, 0))],
        out_specs=pl.BlockSpec((tm, d), lambda i: (i, 0)),
        out_shape=jax.ShapeDtypeStruct((n, d), BF16),
        compiler_params=_params(("parallel",)),
        name="entry_norm",
    )(x, ln_g, ln_b)


def _gates_kernel(x_ref, w_ref, o_ref):
    y = jnp.dot(x_ref[...], w_ref[...], preferred_element_type=F32)
    o_ref[...] = _sigmoid(y).astype(o_ref.dtype)


def _rwkv_proj_kernel(x_ref, w_ref, o_ref):
    o_ref[...] = jnp.dot(x_ref[...], w_ref[...], preferred_element_type=F32)


def _proj_matmul(body, xn, w, out_dtype, tm, tn, name):
    n, d = xn.shape
    ncol = w.shape[1]
    return pl.pallas_call(
        body,
        grid=(n // tm, ncol // tn),
        in_specs=[pl.BlockSpec((tm, d), lambda i, j: (i, 0)),
                  pl.BlockSpec((d, tn), lambda i, j: (0, j))],
        out_specs=pl.BlockSpec((tm, tn), lambda i, j: (i, j)),
        out_shape=jax.ShapeDtypeStruct((n, ncol), out_dtype),
        compiler_params=_params(("parallel", "parallel")),
        name=name,
    )(xn, w)


def _fox_proj_kernel(xn_ref, w_ref, wf_ref, bf_ref, qn_ref, kn_ref,
                     qt_ref, kf_ref, kb_ref, vf_ref, vt_ref, lf_ref, *, n_heads):
    j = pl.program_id(0)
    y = jnp.dot(xn_ref[...], w_ref[...], preferred_element_type=F32)

    def rms(yh, gain):
        ms = jnp.mean(yh * yh, axis=-1, keepdims=True)
        return yh * lax.rsqrt(ms + QK_EPS) * gain

    @pl.when(j == 0)
    def _():
        scale = FOX_HEAD_DIM ** -0.5 * LOG2_E
        for h in range(n_heads):
            yh = y[:, h * FOX_HEAD_DIM:(h + 1) * FOX_HEAD_DIM]
            qt_ref[h] = (rms(yh, qn_ref[...]) * scale).astype(BF16).T
        fl = jnp.dot(xn_ref[...], wf_ref[...], preferred_element_type=F32)
        lf_ref[...] = _log_sigmoid(fl + bf_ref[...])

    @pl.when(j == 1)
    def _():
        for h in range(n_heads):
            sl = slice(h * FOX_HEAD_DIM, (h + 1) * FOX_HEAD_DIM)
            kh = rms(y[:, sl], kn_ref[...])
            kf_ref[:, sl] = kh
            kb_ref[h] = kh.astype(BF16)

    @pl.when(j == 2)
    def _():
        vf_ref[...] = y
        for h in range(n_heads):
            vt_ref[h] = y[:, h * FOX_HEAD_DIM:(h + 1) * FOX_HEAD_DIM].astype(BF16).T


def _fox_proj(xn, w_qkv, w_f, b_f, q_norm, k_norm, tm):
    bsz, t, d = xn.shape
    width = w_qkv.shape[1] // 3
    n_heads = width // FOX_HEAD_DIM
    nt = t // tm

    def rows_of(section):
        def where(j, b, i):
            mine, early = j == section, j < section
            return (jnp.where(mine, b, jnp.where(early, 0, bsz - 1)),
                    jnp.where(mine, i, jnp.where(early, 0, nt - 1)))
        return where

    def hm(section):
        at = rows_of(section)
        return pl.BlockSpec((None, n_heads, tm, FOX_HEAD_DIM), lambda j, b, i: (at(j, b, i)[0], 0, at(j, b, i)[1], 0))

    def tr(section):
        at = rows_of(section)
        return pl.BlockSpec((None, n_heads, None, FOX_HEAD_DIM, tm),
                            lambda j, b, i: (at(j, b, i)[0], 0, at(j, b, i)[1], 0, 0))

    def tok(section, c):
        at = rows_of(section)
        return pl.BlockSpec((None, tm, c), lambda j, b, i: (at(j, b, i)[0], at(j, b, i)[1], 0))

    const = lambda r, c: pl.BlockSpec((r, c), lambda j, b, i: (0, 0))
    tr_shape = jax.ShapeDtypeStruct((bsz, n_heads, nt, FOX_HEAD_DIM, tm), BF16)
    tok_shape = jax.ShapeDtypeStruct((bsz, t, width), F32)
    return pl.pallas_call(
        functools.partial(_fox_proj_kernel, n_heads=n_heads),
        grid=(3, bsz, nt),
        in_specs=[
            pl.BlockSpec((None, tm, d), lambda j, b, i: (b, i, 0)),
            pl.BlockSpec((d, width), lambda j, b, i: (0, j)),
            const(d, LANES), const(1, LANES),
            const(1, FOX_HEAD_DIM), const(1, FOX_HEAD_DIM),
        ],
        out_specs=[tr(0), tok(1, width), hm(1), tok(2, width), tr(2), tok(0, LANES)],
        out_shape=[tr_shape, tok_shape, jax.ShapeDtypeStruct((bsz, n_heads, t, FOX_HEAD_DIM), BF16),
                   tok_shape, tr_shape, jax.ShapeDtypeStruct((bsz, t, LANES), F32)],
        compiler_params=_params(("arbitrary", "arbitrary", "arbitrary")),
        name="proj_fox",
    )(xn, w_qkv, w_f, b_f, q_norm, k_norm)


def _fgate_bias_kernel(lf_ref, o_ref, carry, *, length, n_heads):
    t = pl.program_id(1)
    tt = lf_ref.shape[0]

    @pl.when(t == 0)
    def _():
        carry[...] = jnp.zeros_like(carry)

    ri = lax.broadcasted_iota(jnp.int32, (tt, tt), 0)
    rj = lax.broadcasted_iota(jnp.int32, (tt, tt), 1)
    tri = jnp.where(rj <= ri, 1.0, 0.0).astype(BF16)
    csum = _dot_exact_lhs(tri, lf_ref[...]) + carry[...]
    carry[...] = csum[tt - 1:tt, :]
    pos = t * tt + lax.broadcasted_iota(jnp.int32, csum.shape, 0)
    neg = jnp.where(pos < length, -LOG2_E * csum, NEG_INF)
    pieces = jnp.concatenate(_split3(neg), axis=1)
    sr = lax.broadcasted_iota(jnp.int32, (3 * LANES, LANES), 0)
    sc = lax.broadcasted_iota(jnp.int32, (3 * LANES, LANES), 1)
    for h in range(n_heads):
        sel = jnp.where(sr % LANES == h, jnp.where(sr // LANES == sc, 1.0, 0.0), 0.0).astype(BF16)
        o_ref[h] = jnp.dot(pieces, sel, preferred_element_type=F32).astype(BF16)


def _fgate_bias(lf, length, n_heads, tt):
    bsz, lp, _ = lf.shape
    return pl.pallas_call(
        functools.partial(_fgate_bias_kernel, length=length, n_heads=n_heads),
        grid=(bsz, lp // tt),
        in_specs=[pl.BlockSpec((None, tt, LANES), lambda b, t: (b, t, 0))],
        out_specs=pl.BlockSpec((None, n_heads, tt, LANES), lambda b, t: (b, 0, t, 0)),
        out_shape=jax.ShapeDtypeStruct((bsz, n_heads, lp, LANES), BF16),
        scratch_shapes=[pltpu.VMEM((1, LANES), F32)],
        compiler_params=_params(("parallel", "arbitrary")),
        name="fgate_bias",
    )(lf)


N_BIAS_PIECES = 3


ATTN_HEADS_PER_STEP = 2


def _attn_kernel(qt_ref, k_ref, nf_ref, vt_ref, o_ref, *, tq, tk, past):
    i = pl.program_id(2)
    n_heads = qt_ref.shape[0]
    tw = min(tq, MXU_DIM)
    chains = [(h, c) for h in range(n_heads) for c in range(tq // tw)]
    ones_rows = jnp.where(lax.broadcasted_iota(jnp.int32, (LANES, tw), 0) < N_BIAS_PIECES, 1.0, 0.0)
    qa = [jnp.concatenate([qt_ref[h, :, c * tw:(c + 1) * tw], ones_rows.astype(BF16)], axis=0)
          for h, c in chains]
    n_full = (past + i * tq) // tk

    def update(carry, s, vt):
        m, l, acc = carry
        m_new = jnp.maximum(m, jnp.max(s, axis=0, keepdims=True))
        p = jnp.exp2(s - m_new)
        alpha = jnp.exp2(m - m_new)
        l = alpha * l + jnp.sum(p, axis=0, keepdims=True)
        acc = alpha * acc + jnp.dot(vt, p.astype(BF16), preferred_element_type=F32)
        return m_new, l, acc

    def scores(j, causal=False):
        start = pl.multiple_of(j * tk, tk)
        ka = [jnp.concatenate([k_ref[h, pl.ds(start, tk), :], nf_ref[h, pl.ds(start, tk), :]], axis=1)
              for h in range(n_heads)]
        ss = [jnp.dot(ka[h], qa[n], preferred_element_type=F32) for n, (h, c) in enumerate(chains)]
        if causal:
            key = lax.broadcasted_iota(jnp.int32, (tk, tw), 0)
            qry = lax.broadcasted_iota(jnp.int32, (tk, tw), 1)
            ss = [jnp.where(key <= qry + c * tw, s, NEG_INF) for s, (h, c) in zip(ss, chains)]
        return ss

    def absorb(carry, ss, j):
        return tuple(update(carry[n], ss[n], vt_ref[h, j]) for n, (h, c) in enumerate(chains))

    def two_blocks(pair, carry):
        sa, sb = scores(2 * pair), scores(2 * pair + 1)
        return absorb(absorb(carry, sa, 2 * pair), sb, 2 * pair + 1)

    init = (jnp.full((1, tw), NEG_INF, F32), jnp.zeros((1, tw), F32), jnp.zeros((FOX_HEAD_DIM, tw), F32))
    carry = lax.fori_loop(0, n_full // 2, two_blocks, tuple(init for _ in chains))
    last_full = jnp.maximum(n_full - 1, 0)
    carry = lax.cond(n_full % 2 == 1, lambda c: absorb(c, scores(last_full), last_full), lambda c: c, carry)
    carry = absorb(carry, scores(n_full, causal=True), n_full)
    for n, (h, c) in enumerate(chains):
        _, l, acc = carry[n]
        o_ref[c * tw:(c + 1) * tw, h * FOX_HEAD_DIM:(h + 1) * FOX_HEAD_DIM] = (acc / l).T.astype(o_ref.dtype)


def _fox_attention(qt, k, nf, vt, past, tk):
    bsz, n_heads, nq, dh, tq = qt.shape
    lp = k.shape[2]
    hps = ATTN_HEADS_PER_STEP
    assert past % tk == 0 and (tq == tk or nq == 1) and tq <= tk and lp % tk == 0 and n_heads % hps == 0
    whole = lambda a: pl.BlockSpec((None, hps) + a.shape[2:], lambda b, h, i: (b, h) + (0,) * (a.ndim - 2))
    return pl.pallas_call(
        functools.partial(_attn_kernel, tq=tq, tk=tk, past=past),
        grid=(bsz, n_heads // hps, nq),
        in_specs=[pl.BlockSpec((None, hps, None, dh, tq), lambda b, h, i: (b, h, i, 0, 0)),
                  whole(k), whole(nf), whole(vt)],
        out_specs=pl.BlockSpec((None, tq, hps * dh), lambda b, h, i: (b, i, h)),
        out_shape=jax.ShapeDtypeStruct((bsz, nq * tq, n_heads * dh), BF16),
        compiler_params=_params(("parallel", "parallel", "arbitrary")),
        name="fox_attn",
    )(qt, k, nf, vt)


def _rwkv_kernel(p_ref, shift_ref, s0_ref, mu_ref, w0_ref, a0_ref, kk_ref, ka_ref, rk_ref,
                 gng_ref, gnb_ref, w2_ref, a2_ref, g2_ref, o_ref, sout_ref,
                 state, prev_row, *, tc, t_valid):
    c_len = RWKV_CHUNK
    t = pl.program_id(1)
    width = o_ref.shape[-1]
    n_groups = width // MXU_DIM

    @pl.when(t == 0)
    def _():
        state[...] = s0_ref[...]
        prev_row[...] = shift_ref[...]

    p = p_ref[...]
    ridx = lax.broadcasted_iota(jnp.int32, p.shape, 0)
    prev = jnp.where(ridx == 0, jnp.broadcast_to(prev_row[...], p.shape), pltpu.roll(p, 1, 0))
    prev_row[...] = p[tc - 1:tc, :]
    xs = p + (prev - p) * mu_ref[...]
    r = xs[:, 0:width]
    k = xs[:, width:2 * width]
    v = xs[:, 2 * width:3 * width]
    lora_in = xs[:, 3 * width:3 * width + LANES]
    gate_in = xs[:, 3 * width + LANES:]
    zw = _dot(jnp.tanh(lora_in), w2_ref[...])
    za = _dot(lora_in, a2_ref[...])
    g = _dot(_sigmoid(gate_in), g2_ref[...])
    lw = -DECAY_SCALE * _sigmoid(w0_ref[...] + zw)
    iclr = _sigmoid(a0_ref[...] + za)

    hr = lax.broadcasted_iota(jnp.int32, (MXU_DIM, MXU_DIM), 0) // RWKV_HEAD_DIM
    hc = lax.broadcasted_iota(jnp.int32, (MXU_DIM, MXU_DIM), 1) // RWKV_HEAD_DIM
    same_head = hr == hc
    bd_f32 = jnp.where(same_head, 1.0, 0.0)
    ones_bd = bd_f32.astype(BF16)

    def head_sum(x, split=True):
        hi = x.astype(BF16)
        lo = (x - hi.astype(F32)).astype(BF16) if split else None
        parts = []
        for gi in range(n_groups):
            ls = slice(gi * MXU_DIM, (gi + 1) * MXU_DIM)
            part = jnp.dot(hi[:, ls], ones_bd, preferred_element_type=F32)
            if split:
                part = part + jnp.dot(lo[:, ls], ones_bd, preferred_element_type=F32)
            parts.append(part)
        return jnp.concatenate(parts, axis=-1)

    kk = k * kk_ref[...]
    kk = kk * lax.rsqrt(jnp.maximum(head_sum(kk * kk), 1e-24))
    k = k * (1.0 + (iclr - 1.0) * ka_ref[...])
    if t_valid < tc:
        live = lax.broadcasted_iota(jnp.int32, (tc, width), 0) < t_valid
        lw = jnp.where(live, lw, 0.0)
        kk = jnp.where(live, kk, 0.0)
        k = jnp.where(live, k, 0.0)
        v = jnp.where(live, v, 0.0)

    ti = lax.broadcasted_iota(jnp.int32, (tc, tc), 0)
    tj = lax.broadcasted_iota(jnp.int32, (tc, tc), 1)
    tri = jnp.where(ti // c_len == tj // c_len, jnp.where(tj <= ti, 1.0, 0.0), 0.0).astype(BF16)
    gcum = _dot_exact_lhs(tri, lw)
    e_in = jnp.exp(gcum)
    e_inv = jnp.exp(-gcum)
    at_all = -kk * jnp.exp(gcum - lw)
    rt_all = r * e_in
    bt_all = kk * iclr * e_inv
    kt_all = k * e_inv
    bonus = head_sum(r * k * rk_ref[...]) * v

    row = lax.broadcasted_iota(jnp.int32, (c_len, MXU_DIM), 0)
    lane = lax.broadcasted_iota(jnp.int32, (c_len, MXU_DIM), 1) % c_len
    strict = jnp.where(lane < row, 1.0, 0.0)
    incl = jnp.where(lane <= row, 1.0, 0.0)
    eye_w = jnp.where(lane == row, 1.0, 0.0)

    def bd(x):
        return jnp.concatenate([x.astype(BF16)] * HEADS_PER_GROUP, axis=0) * ones_bd

    n_sq = c_len.bit_length() - 1
    n_chunks = tc // c_len
    units = [(ci, gi) for ci in range(n_chunks) for gi in range(n_groups)]

    def cut(x, u):
        ci, gi = u
        return x[ci * c_len:(ci + 1) * c_len, gi * MXU_DIM:(gi + 1) * MXU_DIM]

    at = [cut(at_all, u) for u in units]
    rt = [cut(rt_all, u) for u in units]
    bt = [cut(bt_all, u) for u in units]
    kt = [cut(kt_all, u) for u in units]
    vv = [cut(v, u) for u in units]
    ar = [jnp.concatenate([a, r_], axis=0) for a, r_ in zip(at, rt)]
    ab = [_dot_nt(x, bd(b_)) for x, b_ in zip(ar, bt)]
    ak = [_dot_nt(x, bd(k_)) for x, k_ in zip(ar, kt)]
    pw = [x[:c_len] * strict for x in ab]
    a_rb = [x[c_len:] * incl for x in ab]
    a_ak = [x[:c_len] * strict for x in ak]
    a_rk = [x[c_len:] * incl for x in ak]
    tm = [eye_w + x for x in pw]
    pw = [_dot(x, bd(x)) for x in pw]
    for js in range(1, n_sq):
        if js < n_sq - 1:
            tp = [_dot(jnp.concatenate([t_, x], axis=0), bd(x)) for t_, x in zip(tm, pw)]
            tm = [t_ + y_[:c_len] for t_, y_ in zip(tm, tp)]
            pw = [y_[c_len:] for y_ in tp]
        else:
            tm = [t_ + _dot(t_, bd(x)) for t_, x in zip(tm, pw)]
    bdv = [bd(x) for x in vv]
    a_hat = [_dot(t_, bd(a)) for t_, a in zip(tm, at)]
    av = [_dot(x, b_) for x, b_ in zip(a_ak, bdv)]
    u_hat = [_dot(t_, bd(x)) for t_, x in zip(tm, av)]
    r_hat = [r_ + _dot(x, bd(a)) for r_, x, a in zip(rt, a_rb, a_hat)]
    y_hat = [_dot(x, bd(uh)) + _dot(z, b_) for x, uh, z, b_ in zip(a_rb, u_hat, a_rk, bdv)]
    lhs = [jnp.concatenate([a, r_], axis=0) for a, r_ in zip(a_hat, r_hat)]

    st = [state[gi] for gi in range(n_groups)]
    y_rows = []
    for ci in range(n_chunks):
        gend = jnp.exp(gcum[(ci + 1) * c_len - 1:(ci + 1) * c_len, :])
        us = [ci * n_groups + gi for gi in range(n_groups)]
        ge = [gend[:, gi * MXU_DIM:(gi + 1) * MXU_DIM] for gi in range(n_groups)]
        uy = [_dot_nt(lhs[u], st[gi]) for gi, u in enumerate(us)]
        uu = [uy[gi][:c_len] + u_hat[u] for gi, u in enumerate(us)]
        y_rows.append(jnp.concatenate([uy[gi][c_len:] + y_hat[u] for gi, u in enumerate(us)], axis=1))
        uv_t = [jnp.concatenate([uu[gi], vv[u]], axis=0).T for gi, u in enumerate(us)]
        bk = [jnp.concatenate([bt[u] * ge[gi], kt[u] * ge[gi]], axis=0) for gi, u in enumerate(us)]
        st = [st[gi] * ge[gi] + _dot(uv_t[gi], bk[gi]) * bd_f32 for gi in range(n_groups)]
    for gi in range(n_groups):
        state[gi] = st[gi]

    y = jnp.concatenate(y_rows, axis=0)
    inv_n = 1.0 / RWKV_HEAD_DIM
    mean = head_sum(y, split=False) * inv_n
    yc = y - mean
    var = head_sum(yc * yc, split=False) * inv_n
    yn = yc * lax.rsqrt(var + RWKV_GN_EPS) * gng_ref[...] + gnb_ref[...]
    o_ref[...] = ((yn + bonus) * g).astype(o_ref.dtype)

    @pl.when(t == pl.num_programs(1) - 1)
    def _():
        sout_ref[...] = state[...]


def _rwkv_mix(p, shift_prev, s0_bd, vecs, w2p, a2p, g2p, tc, t_valid):
    bsz, tp, cols = p.shape
    width = w2p.shape[1]
    n_groups = width // MXU_DIM
    vec_specs = [pl.BlockSpec((1, a.shape[1]), lambda b, t: (0, 0)) for a in vecs]
    mat = lambda a: pl.BlockSpec(a.shape, lambda b, t: (0, 0))
    st_spec = pl.BlockSpec((None, n_groups, MXU_DIM, MXU_DIM), lambda b, t: (b, 0, 0, 0))
    return pl.pallas_call(
        functools.partial(_rwkv_kernel, tc=tc, t_valid=t_valid),
        grid=(bsz, tp // tc),
        in_specs=[pl.BlockSpec((None, tc, cols), lambda b, t: (b, t, 0)),
                  pl.BlockSpec((None, 1, cols), lambda b, t: (b, 0, 0)),
                  st_spec] + vec_specs + [mat(w2p), mat(a2p), mat(g2p)],
        out_specs=[pl.BlockSpec((None, tc, width), lambda b, t: (b, t, 0)), st_spec],
        out_shape=[jax.ShapeDtypeStruct((bsz, tp, width), BF16),
                   jax.ShapeDtypeStruct(s0_bd.shape, F32)],
        scratch_shapes=[pltpu.VMEM((n_groups, MXU_DIM, MXU_DIM), F32), pltpu.VMEM((1, cols), F32)],
        compiler_params=_params(("parallel", "arbitrary")),
        name="rwkv_mix",
    )(p, shift_prev, s0_bd, *vecs, w2p, a2p, g2p)


def _merge_kernel(x_ref, oa_ref, ob_ref, gt_ref, wa_ref, wb_ref, wo_ref, lig_ref, lib_ref,
                  l1g_ref, l1b_ref, wr_ref, rb_ref, x1_ref, ids_ref, wts_ref, *, alpha):
    d = x_ref.shape[-1]
    xn = _layer_norm(x_ref[...], lig_ref[...], lib_ref[...])
    ya = jnp.dot(oa_ref[...], wa_ref[...], preferred_element_type=F32)
    yb = jnp.dot(ob_ref[...], wb_ref[...], preferred_element_type=F32)
    merged = gt_ref[:, :d].astype(F32) * ya + gt_ref[:, d:].astype(F32) * yb
    out = jnp.dot(merged.astype(BF16), wo_ref[...], preferred_element_type=F32)
    x1 = _layer_norm(alpha * xn + out, l1g_ref[...], l1b_ref[...])
    x1_ref[...] = x1

    h1, h2, _ = _split3(x1)
    r1 = jnp.dot(h1, wr_ref[...], preferred_element_type=F32)
    r2 = jnp.dot(h2, wr_ref[...], preferred_element_type=F32)
    logits = (r1[:, :LANES] + (r1[:, LANES:] + r2[:, :LANES]) + r2[:, LANES:]) + rb_ref[...]
    lane = lax.broadcasted_iota(jnp.int32, logits.shape, 1).astype(F32)
    big = 1e9

    def first_max(vals):
        mx = jnp.max(vals, axis=-1, keepdims=True)
        idx = jnp.min(jnp.where(vals == mx, lane, big), axis=-1, keepdims=True)
        return mx, idx

    is_grp = lane < N_GROUPS
    gmax, grp = first_max(jnp.where(is_grp, logits, NEG_INF))
    p_grp = 1.0 / jnp.sum(jnp.where(is_grp, jnp.exp(logits - gmax), 0.0), axis=-1, keepdims=True)
    lo = N_GROUPS + grp * EXPERTS_PER_GROUP
    elog = jnp.where(lane >= lo, jnp.where(lane < lo + EXPERTS_PER_GROUP, logits, NEG_INF), NEG_INF)
    v1, i1 = first_max(elog)
    v2, i2 = first_max(jnp.where(lane == i1, NEG_INF, elog))
    e2 = jnp.exp(v2 - v1)
    w1 = p_grp / (1.0 + e2)
    w2 = p_grp * e2 / (1.0 + e2)
    ids = jnp.where(lane == 0, i1 - N_GROUPS, jnp.where(lane == 1, i2 - N_GROUPS, 0.0))
    ids_ref[...] = ids.astype(jnp.int32)
    wts_ref[...] = jnp.where(lane == 0, w1, jnp.where(lane == 1, w2, 0.0))


def _merge_out(x, oa, ob, gates, wa, wb, wo, lig, lib, l1g, l1b, w_router, rbias, alpha, tm):
    n, d = x.shape
    half = oa.shape[1]
    row = lambda c: pl.BlockSpec((tm, c), lambda i: (i, 0))
    const = lambda a: pl.BlockSpec(a.shape, lambda i: (0, 0), pipeline_mode=pl.Buffered(1))
    return pl.pallas_call(
        functools.partial(_merge_kernel, alpha=alpha),
        grid=(n // tm,),
        in_specs=[row(d), row(half), row(half), row(2 * d), const(wa), const(wb), const(wo),
                  const(lig), const(lib), const(l1g), const(l1b), const(w_router), const(rbias)],
        out_specs=[row(d), row(LANES), row(LANES)],
        out_shape=[jax.ShapeDtypeStruct((n, d), F32), jax.ShapeDtypeStruct((n, LANES), jnp.int32),
                   jax.ShapeDtypeStruct((n, LANES), F32)],
        compiler_params=_params(("parallel",)),
        name="merge_out",
    )(x, oa, ob, gates, wa, wb, wo, lig, lib, l1g, l1b, w_router, rbias)


DMA_ISSUE_UNROLL = 8
MOE_BLOCK = 512
COMBINE_PIECE = 32


def _wait_rows(make_row_copy, count):
    lax.fori_loop(0, count, lambda q, c: (make_row_copy(0).wait(), c)[1], 0, unroll=DMA_ISSUE_UNROLL)


def _dispatch_kernel(zblk_ref, dest_ref, *rest, steps):
    xs_hbm, zeros, sem, zsem = rest[-4:]
    x_refs = rest[:-4]
    i = pl.program_id(0)
    blk = zeros.shape[0]

    def zero_block(z):
        start = pl.multiple_of(zblk_ref[z] * blk, blk)
        return pltpu.make_async_copy(zeros, xs_hbm.at[pl.ds(start, blk), :], zsem)

    @pl.when(i == 0)
    def _():
        zeros[...] = jnp.zeros_like(zeros)
        for z in range(zblk_ref.shape[0]):
            pl.when(zblk_ref[z] >= 0)(lambda z=z: zero_block(z).start())
        for z in range(zblk_ref.shape[0]):
            pl.when(zblk_ref[z] >= 0)(lambda z=z: zero_block(z).wait())

    def scatter_rows(x_ref):
        tm = x_ref.shape[0]

        def to_slot(r, kslot):
            return pltpu.make_async_copy(x_ref.at[pl.ds(r, 1), :],
                                         xs_hbm.at[pl.ds(dest_ref[0, TOP_K * r + kslot], 1), :], sem)

        def body(r, c):
            for kslot in range(TOP_K):
                to_slot(r, kslot).start()
            return c
        lax.fori_loop(0, tm, body, 0, unroll=DMA_ISSUE_UNROLL)
        _wait_rows(lambda q: to_slot(0, 0), TOP_K * tm)

    for g, x_ref in enumerate(x_refs):
        in_group = jnp.logical_and(i >= steps[g], i < steps[g + 1])
        pl.when(in_group)(functools.partial(scatter_rows, x_ref))


def _moe_dispatch(x1s, tiles, zero_blocks, dest_steps, n_rows, blk):
    d = x1s[0].shape[1]
    steps = [0]
    for x1, tm in zip(x1s, tiles):
        steps.append(steps[-1] + x1.shape[0] // tm)

    def x_spec(g, tm):
        last = steps[g + 1] - steps[g] - 1
        return pl.BlockSpec((tm, d), lambda i, zb: (jnp.clip(i - steps[g], 0, last), 0))

    grid_spec = pltpu.PrefetchScalarGridSpec(
        num_scalar_prefetch=1,
        grid=(steps[-1],),
        in_specs=[pl.BlockSpec((None, 1, dest_steps.shape[-1]), lambda i, zb: (i, 0, 0),
                               memory_space=pltpu.SMEM)] + [x_spec(g, tm) for g, tm in enumerate(tiles)],
        out_specs=pl.BlockSpec(memory_space=pl.ANY),
        scratch_shapes=[pltpu.VMEM((blk, d), F32), pltpu.SemaphoreType.DMA(()), pltpu.SemaphoreType.DMA(())],
    )
    return pl.pallas_call(
        functools.partial(_dispatch_kernel, steps=tuple(steps)),
        grid_spec=grid_spec,
        out_shape=jax.ShapeDtypeStruct((n_rows, d), F32),
        compiler_params=_params(("arbitrary",)),
        name="moe_dispatch",
    )(zero_blocks, dest_steps, *x1s)


def _experts_kernel(blk_e_ref, xblk_ref, nvalid_ref, xs_ref, wg_ref, wu_ref, wd_ref, ys_ref):
    j = pl.program_id(0)

    @pl.when(nvalid_ref[j] > 0)
    def _():
        xv = xs_ref[...].astype(BF16)
        hg = jnp.dot(xv, wg_ref[...].astype(BF16), preferred_element_type=F32)
        hu = jnp.dot(xv, wu_ref[...].astype(BF16), preferred_element_type=F32)
        h = hg * _sigmoid(hg) * hu
        ys_ref[...] = jnp.dot(h.astype(BF16), wd_ref[...].astype(BF16), preferred_element_type=F32)

    @pl.when(nvalid_ref[j] == 0)
    def _():
        ys_ref[...] = jnp.zeros_like(ys_ref)


def _moe_experts(xs, blk_e, xblk, nvalid, w_gate, w_up, w_down, blk):
    d = xs.shape[1]
    nb = blk_e.shape[0]
    de = w_gate.shape[-1]
    grid_spec = pltpu.PrefetchScalarGridSpec(
        num_scalar_prefetch=3,
        grid=(nb,),
        in_specs=[
            pl.BlockSpec((blk, d), lambda j, be, xb, nv: (xb[j], 0)),
            pl.BlockSpec((None, d, de), lambda j, be, xb, nv: (be[j], 0, 0)),
            pl.BlockSpec((None, d, de), lambda j, be, xb, nv: (be[j], 0, 0)),
            pl.BlockSpec((None, de, d), lambda j, be, xb, nv: (be[j], 0, 0)),
        ],
        out_specs=pl.BlockSpec((blk, d), lambda j, be, xb, nv: (j, 0)),
    )
    return pl.pallas_call(
        _experts_kernel,
        grid_spec=grid_spec,
        out_shape=jax.ShapeDtypeStruct((nb * blk, d), F32),
        compiler_params=_params(("arbitrary",)),
        name="moe_experts",
    )(blk_e, xblk, nvalid, xs, w_gate, w_up, w_down)


def _combine_kernel(cur_ref, nxt_ref, x1_ref, wts_ref, g_ref, b_ref, ys_hbm, o_ref, ybuf, sem, *, alpha):
    i = pl.program_id(0)
    tm = x1_ref.shape[0]
    piece = min(tm, COMBINE_PIECE)

    def row(tbl_ref, s, r, kslot):
        return pltpu.make_async_copy(ys_hbm.at[pl.ds(tbl_ref[0, TOP_K * r + kslot], 1), :],
                                     ybuf.at[s, pl.ds(kslot * tm + r, 1), :], sem.at[s])

    @pl.when(i == 0)
    def _():
        def body(r, c):
            for kslot in range(TOP_K):
                row(cur_ref, 0, r, kslot).start()
            return c
        lax.fori_loop(0, tm, body, 0, unroll=DMA_ISSUE_UNROLL)

    def step(slot):
        _wait_rows(lambda q: row(cur_ref, slot, 0, 0), TOP_K * tm)
        for r0 in range(0, tm, piece):
            rows = slice(r0, r0 + piece)
            w = wts_ref[rows, :]
            y = w[:, 0:1] * ybuf[slot, r0:r0 + piece, :] + w[:, 1:2] * ybuf[slot, tm + r0:tm + r0 + piece, :]
            o_ref[rows, :] = _layer_norm(alpha * x1_ref[rows, :] + y, g_ref[...], b_ref[...])
            for r in range(r0, r0 + piece):
                for kslot in range(TOP_K):
                    row(nxt_ref, 1 - slot, r, kslot).start()

        @pl.when(i + 1 == pl.num_programs(0))
        def _():
            _wait_rows(lambda q: row(nxt_ref, 1 - slot, 0, 0), TOP_K * tm)

    pl.when(i % 2 == 0)(lambda: step(0))
    pl.when(i % 2 == 1)(lambda: step(1))


def _moe_combine(x1, ys, dest_tiles, wts, g, b, alpha, tm):
    n, d = x1.shape
    last = n // tm - 1
    tbl = lambda f: pl.BlockSpec((None, 1, TOP_K * tm), lambda i: (f(i), 0, 0), memory_space=pltpu.SMEM)
    return pl.pallas_call(
        functools.partial(_combine_kernel, alpha=alpha),
        grid=(n // tm,),
        in_specs=[tbl(lambda i: i), tbl(lambda i: jnp.minimum(i + 1, last)),
                  pl.BlockSpec((tm, d), lambda i: (i, 0)),
                  pl.BlockSpec((tm, LANES), lambda i: (i, 0)),
                  pl.BlockSpec((1, d), lambda i: (0, 0)), pl.BlockSpec((1, d), lambda i: (0, 0)),
                  pl.BlockSpec(memory_space=pl.ANY)],
        out_specs=pl.BlockSpec((tm, d), lambda i: (i, 0)),
        out_shape=jax.ShapeDtypeStruct((n, d), F32),
        scratch_shapes=[pltpu.VMEM((2, TOP_K * tm, d), F32), pltpu.SemaphoreType.DMA((2,))],
        compiler_params=_params(("arbitrary",)),
        name="moe_combine",
    )(dest_tiles, dest_tiles, x1, wts, g, b, ys)


def _dispatch_tables(eid, blk):
    n = eid.shape[0]
    nk = n * TOP_K
    nb = -(-(nk + N_EXPERTS * (blk - 1)) // blk)
    experts = jnp.arange(N_EXPERTS, dtype=jnp.int32)
    chunk = LANES
    e2 = eid.reshape(nk // chunk, chunk)
    onehot = (e2[:, :, None] == experts).astype(jnp.int32)
    chunk_counts = onehot.sum(axis=1)
    chunk_base = jnp.cumsum(chunk_counts, axis=0) - chunk_counts
    earlier = jnp.arange(chunk)[None, :] < jnp.arange(chunk)[:, None]
    in_chunk = ((e2[:, :, None] == e2[:, None, :]) & earlier[None]).sum(axis=2)
    counts = chunk_counts.sum(axis=0)
    nblk_e = (counts + blk - 1) // blk
    blk_end = jnp.cumsum(nblk_e)
    blk_start = blk_end - nblk_e
    base = chunk_base + (blk_start * blk)[None, :]
    dest = ((onehot * base[:, None, :]).sum(axis=2) + in_chunk).reshape(-1)
    n_used = blk_end[-1]
    tail = n_used + experts
    zero_blocks = jnp.concatenate([jnp.where(nblk_e > 0, blk_end - 1, -1), jnp.where(tail < nb, tail, -1)])
    bidx = jnp.arange(nb, dtype=jnp.int32)
    blk_e = jnp.minimum((blk_end[None, :] <= bidx[:, None]).sum(axis=1), N_EXPERTS - 1).astype(jnp.int32)
    mine = (blk_e[:, None] == experts[None, :]).astype(jnp.int32)
    left = (mine * (counts + blk_start * blk)[None, :]).sum(axis=1) - bidx * blk
    nvalid = jnp.clip(left, 0, blk).astype(jnp.int32)
    nvalid = jnp.where(bidx < n_used, nvalid, 0)
    xblk = jnp.minimum(bidx, n_used - 1).astype(jnp.int32)
    return dest.astype(jnp.int32), zero_blocks.astype(jnp.int32), blk_e, xblk, nvalid, nb * blk


def _pick(n, prefs):
    for p in prefs:
        if n % p == 0:
            return p
    return n


def _layer(x, past_k, past_v, past_logf, s0, shift_prev, wts, alpha):
    bsz, t, d = x.shape
    n = bsz * t
    xf = x.reshape(n, d)
    fox_width = wts['w_qkv'].shape[1] // 3
    n_heads = fox_width // FOX_HEAD_DIM
    rw_width = wts['w2p'].shape[1]
    rw_heads = rw_width // RWKV_HEAD_DIM
    rw_cols = wts['rwkv_cols']
    past = past_k.shape[1]

    xn = _entry_norm(xf, wts['ln_in_g'], wts['ln_in_b'], _pick(n, (1024, 512, 256)))
    tm = _pick(n, (PROJ_ROWS, 512, 256))
    gates = _proj_matmul(_gates_kernel, xn, wts['w_gates'], BF16, tm, 1024, "proj_gates")
    p_rw = _proj_matmul(_rwkv_proj_kernel, xn, wts['w_rwkv'], F32, tm, wts['w_rwkv'].shape[1] // 3, "proj_rwkv")
    t_fox = -(-t // LANES) * LANES
    tq = _pick(t_fox, (ATTN_BLOCK, 256, LANES))
    xn_fox = xn.reshape(bsz, t, d)
    if t_fox != t:
        xn_fox = jnp.pad(xn_fox, ((0, 0), (0, t_fox - t), (0, 0)))
    qt, k_f, k_b, v_f, vt, lf = _fox_proj(xn_fox, wts['w_qkv'], wts['w_f'], wts['b_f'], wts['q_norm'],
                                          wts['k_norm'], tq)
    k_f, v_f, logf = k_f[:, :t], v_f[:, :t], lf[:, :t, :n_heads]
    if past:
        tk = ATTN_BLOCK
        lpad = -(-(past + t) // tk) * tk
        grow = lambda a, ax: jnp.pad(a, [(0, lpad - a.shape[ax]) if i == ax else (0, 0) for i in range(a.ndim)])
        lf_past = jnp.pad(past_logf.astype(F32), ((0, 0), (0, 0), (0, LANES - n_heads)))
        lf_all = grow(jnp.concatenate([lf_past, lf[:, :t]], axis=1), 1)
        k_all = grow(jnp.concatenate([jnp.swapaxes(past_k, 1, 2).astype(BF16), k_b[:, :, :t]], axis=2), 2)
        vt_past = jnp.transpose(past_v, (0, 2, 3, 1)).astype(BF16)
        vt_all = grow(jnp.concatenate([vt_past, vt[:, :, 0, :, :t]], axis=3), 3)
        vt_all = jnp.swapaxes(vt_all.reshape(bsz, n_heads, FOX_HEAD_DIM, lpad // tk, tk), 2, 3)
    else:
        tk, lf_all, k_all, vt_all = tq, lf, k_b, vt
    nf = _fgate_bias(lf_all, past + t, n_heads, tk)
    o_a = _fox_attention(qt, k_all, nf, vt_all, past, tk)[:, :t]

    tp = -(-t // RWKV_CHUNK) * RWKV_CHUNK
    tc = _pick(tp, (256, 128, 64))
    p3 = p_rw.reshape(bsz, t, -1)
    shift_new = p3[:, t - 1:t, :rw_cols]
    if tp != t:
        p3 = jnp.pad(p3, ((0, 0), (0, tp - t), (0, 0)))
    shift_in = jnp.pad(shift_prev.astype(F32), ((0, 0), (0, 0), (0, p3.shape[-1] - rw_cols)))
    n_grp = rw_heads // HEADS_PER_GROUP
    eye = jnp.eye(HEADS_PER_GROUP, dtype=F32)
    s0_g = s0.astype(F32).reshape(bsz, n_grp, HEADS_PER_GROUP, RWKV_HEAD_DIM, RWKV_HEAD_DIM)
    s0_bd = jnp.einsum('bghvk,hj->bghvjk', s0_g, eye).reshape(bsz, n_grp, MXU_DIM, MXU_DIM)
    o_b, s_bd = _rwkv_mix(p3, shift_in, s0_bd, wts['rwkv_vecs'], wts['w2p'], wts['a2p'], wts['g2p'], tc, t)
    s_new = jnp.einsum('bghvjk,hj->bghvk',
                       s_bd.reshape(bsz, n_grp, HEADS_PER_GROUP, RWKV_HEAD_DIM, HEADS_PER_GROUP, RWKV_HEAD_DIM),
                       eye).reshape(bsz, rw_heads, RWKV_HEAD_DIM, RWKV_HEAD_DIM)
    o_b = o_b[:, :t].reshape(n, rw_width)

    routed = _merge_out(xf, o_a.reshape(n, fox_width), o_b, gates, wts['w_a'], wts['w_b'], wts['w_o'],
                        wts['ln_in_g'], wts['ln_in_b'], wts['ln1_g'], wts['ln1_b'],
                        wts['w_router'], wts['b_router'], alpha, _pick(n, (512, 256)))
    hd = (bsz, t, n_heads, FOX_HEAD_DIM)
    return routed, (k_f.reshape(hd), v_f.reshape(hd), logf, s_new, shift_new)


def _hier_moe(groups, wts, alpha):
    sizes = [x1.shape[0] for x1, _, _ in groups]
    n_all = sum(sizes)
    blk = MOE_BLOCK
    while blk > 8 and blk * N_EXPERTS > n_all * TOP_K:
        blk //= 2
    eid = jnp.concatenate([ids[:, :TOP_K] for _, ids, _ in groups], axis=0)
    dest, zero_blocks, blk_e, xblk, nvalid, n_rows = _dispatch_tables(eid, blk)
    offset, dests = 0, []
    for n in sizes:
        dests.append(dest[TOP_K * offset:TOP_K * (offset + n)])
        offset += n
    tiles = [_pick(n, (1024, 512, 256)) for n in sizes]
    width = TOP_K * max(tiles)
    dest_steps = jnp.concatenate([jnp.pad(dst.reshape(-1, 1, TOP_K * tm), ((0, 0), (0, 0), (0, width - TOP_K * tm)))
                                  for dst, tm in zip(dests, tiles)], axis=0)
    xs = _moe_dispatch([x1 for x1, _, _ in groups], tiles, zero_blocks, dest_steps, n_rows, blk)
    ys = _moe_experts(xs, blk_e, xblk, nvalid, wts['moe_w_gate'], wts['moe_w_up'], wts['moe_w_down'], blk)
    outs = []
    for (x1, _, rw), dst, n in zip(groups, dests, sizes):
        tm = _pick(n, (256, 128, 64, 32, 16, 8))
        outs.append(_moe_combine(x1, ys, dst.reshape(-1, 1, TOP_K * tm), rw, wts['ln2_g'], wts['ln2_b'],
                                 alpha, tm))
    return outs


def _prepare_weights(l, ln_in_g, ln_in_b, w_in, fox_b_f, fox_q_norm, fox_k_norm, rwkv_mu, rwkv_w0, rwkv_w2,
                     rwkv_a0, rwkv_a2, rwkv_g2, rwkv_k_k, rwkv_k_a, rwkv_r_k, rwkv_gn_g, rwkv_gn_b,
                     w_branch_a, w_branch_b, w_out, ln1_g, ln1_b, router_group_w, router_group_b,
                     router_expert_w, router_expert_b, moe_w_gate, moe_w_up, moe_w_down, ln2_g, ln2_b):
    d = w_in.shape[1]
    fox_width = w_branch_a.shape[1]
    rw_width = w_branch_b.shape[1]
    n_heads = fox_width // FOX_HEAD_DIM
    gate_cols = 2 * d
    fox_cols = 3 * fox_width + n_heads
    rw_cols = 3 * rw_width + RWKV_DECAY_RANK + RWKV_ICLR_RANK + RWKV_GATE_RANK
    row = lambda a: a.astype(F32).reshape(1, -1)
    w = w_in[l]
    w_fox = w[:, gate_cols:gate_cols + fox_cols]
    w_rw = w[:, gate_cols + fox_cols:]
    lora = RWKV_DECAY_RANK + RWKV_ICLR_RANK
    assert lora == LANES
    gate_pad = -(-RWKV_GATE_RANK // LANES) * LANES
    cols_pad = 3 * rw_width + lora + gate_pad
    pad_c = cols_pad - rw_cols
    zeros = lambda r: jnp.zeros((r, rw_width), F32)
    wr = jnp.concatenate([router_group_w[l], router_expert_w[l]], axis=1).astype(F32)
    wr = jnp.pad(wr, ((0, 0), (0, LANES - wr.shape[1])))
    rb = jnp.concatenate([router_group_b[l], router_expert_b[l]]).astype(F32)
    return {
        'ln_in_g': row(ln_in_g), 'ln_in_b': row(ln_in_b),
        'w_gates': w[:, :gate_cols].astype(BF16),
        'w_qkv': w_fox[:, :3 * fox_width].astype(BF16),
        'w_f': jnp.pad(w_fox[:, 3 * fox_width:], ((0, 0), (0, LANES - n_heads))).astype(BF16),
        'b_f': jnp.pad(row(fox_b_f[l]), ((0, 0), (0, LANES - n_heads))),
        'q_norm': row(fox_q_norm[l]), 'k_norm': row(fox_k_norm[l]),
        'w_rwkv': jnp.pad(w_rw, ((0, 0), (0, pad_c))).astype(BF16),
        'rwkv_cols': rw_cols,
        'rwkv_vecs': [jnp.pad(row(rwkv_mu[l]), ((0, 0), (0, pad_c))), row(rwkv_w0[l]), row(rwkv_a0[l]),
                      row(rwkv_k_k[l]), row(rwkv_k_a[l]), row(rwkv_r_k[l]), row(rwkv_gn_g[l]),
                      row(rwkv_gn_b[l])],
        'w2p': jnp.concatenate([rwkv_w2[l].astype(F32), zeros(RWKV_ICLR_RANK)]).astype(BF16),
        'a2p': jnp.concatenate([zeros(RWKV_DECAY_RANK), rwkv_a2[l].astype(F32)]).astype(BF16),
        'g2p': jnp.concatenate([rwkv_g2[l].astype(F32), zeros(gate_pad - RWKV_GATE_RANK)]).astype(BF16),
        'w_a': w_branch_a[l].astype(BF16), 'w_b': w_branch_b[l].astype(BF16), 'w_o': w_out[l].astype(BF16),
        'ln1_g': row(ln1_g[l]), 'ln1_b': row(ln1_b[l]),
        'w_router': jnp.concatenate(_split3(wr)[:2], axis=1),
        'b_router': jnp.pad(row(rb), ((0, 0), (0, LANES - rb.shape[0]))),
        'moe_w_gate': moe_w_gate[l], 'moe_w_up': moe_w_up[l], 'moe_w_down': moe_w_down[l],
        'ln2_g': row(ln2_g[l]), 'ln2_b': row(ln2_b[l]),
    }


def kernel(x_prompt, x_sample, cache_fox_k, cache_fox_v, cache_fox_logf, state_rwkv, state_rwkv_shift,
           ln_in_g, ln_in_b, w_in, fox_b_f, fox_q_norm, fox_k_norm, rwkv_mu, rwkv_w0, rwkv_w2,
           rwkv_a0, rwkv_a2, rwkv_g2, rwkv_k_k, rwkv_k_a, rwkv_r_k, rwkv_gn_g, rwkv_gn_b,
           w_branch_a, w_branch_b, w_out, ln1_g, ln1_b, router_group_w, router_group_b,
           router_expert_w, router_expert_b, moe_w_gate, moe_w_up, moe_w_down, ln2_g, ln2_b):
    depth = w_in.shape[0]
    assert depth == 1, "the entry LayerNorm is fused into the layer's projections: single-layer trunk only"
    alpha = (2.0 * depth) ** 0.25
    bp = x_prompt.shape[0]
    n_fox_heads = fox_b_f.shape[1]
    rw_heads, rw_dim = state_rwkv.shape[2], state_rwkv.shape[3]
    rw_cols = state_rwkv_shift.shape[-1]
    wts = _prepare_weights(0, ln_in_g, ln_in_b, w_in, fox_b_f, fox_q_norm, fox_k_norm, rwkv_mu, rwkv_w0,
                           rwkv_w2, rwkv_a0, rwkv_a2, rwkv_g2, rwkv_k_k, rwkv_k_a, rwkv_r_k, rwkv_gn_g,
                           rwkv_gn_b, w_branch_a, w_branch_b, w_out, ln1_g, ln1_b, router_group_w,
                           router_group_b, router_expert_w, router_expert_b, moe_w_gate, moe_w_up,
                           moe_w_down, ln2_g, ln2_b)
    routed_p, new_p = _layer(x_prompt, jnp.zeros((bp, 0, n_fox_heads, FOX_HEAD_DIM), F32),
                             jnp.zeros((bp, 0, n_fox_heads, FOX_HEAD_DIM), F32),
                             jnp.zeros((bp, 0, n_fox_heads), F32),
                             jnp.zeros((bp, rw_heads, rw_dim, rw_dim), F32),
                             jnp.zeros((bp, 1, rw_cols), F32), wts, alpha)
    routed_s, new_s = _layer(x_sample, cache_fox_k[0], cache_fox_v[0], cache_fox_logf[0], state_rwkv[0],
                             state_rwkv_shift[0], wts, alpha)
    yp, ys = _hier_moe([routed_p, routed_s], wts, alpha)
    return ((yp.reshape(x_prompt.shape), ys.reshape(x_sample.shape))
            + tuple(a[None] for a in new_p) + tuple(a[None] for a in new_s))
```
